```python
import jax, jax.numpy as jnp
from jax import lax
import numpy as np

D_MODEL = 1024
BATCH = 4
SEQ = 4096
DEPTH = 2

D_MIX = D_MODEL
NSA_HEADS = 8
NSA_KV_GROUPS = 2
NSA_HEAD_DIM = 64
NSA_REP = NSA_HEADS // NSA_KV_GROUPS
CMP_BLOCK = 32
CMP_STRIDE = 16
CMP_HIDDEN = 128
SEL_BLOCK = 64
SEL_TOPN = 16
WINDOW = 512
Q_BLOCK = 128
HG_HEADS = 4
HG_DK = 128
HG_DV = 128
HG_CHUNK = 64
D_FF = 2816
ROPE_THETA = 10000.0
EPS = 1e-6
NEG = -1e30

NSA_Q_COLS = NSA_HEADS * NSA_HEAD_DIM
NSA_KV_COLS = NSA_KV_GROUPS * NSA_HEAD_DIM
NSA_GATE_COLS = 3 * NSA_HEADS
HG_K_COLS = HG_HEADS * HG_DK
HG_V_COLS = HG_HEADS * HG_DV
N_IN = NSA_Q_COLS + 6 * NSA_KV_COLS + NSA_GATE_COLS + 2 * HG_K_COLS + 2 * HG_V_COLS

kernel_name = "nsa_hgrn2_macaron_hybrid"


def rmsnorm(x, g):
    xf = x.astype(jnp.float32)
    y = xf * lax.rsqrt(jnp.mean(xf * xf, axis=-1, keepdims=True) + EPS) * g.astype(jnp.float32)
    return y.astype(x.dtype)


def swiglu(x, w_gate, w_up, w_down):
    return (jax.nn.silu(x @ w_gate) * (x @ w_up)) @ w_down


def rope_tables(seq, dim):
    inv = 1.0 / (ROPE_THETA ** (jnp.arange(0, dim, 2, dtype=jnp.float32) / dim))
    ang = jnp.arange(seq, dtype=jnp.float32)[:, None] * inv[None, :]
    ang = jnp.concatenate([ang, ang], axis=-1)
    return jnp.cos(ang), jnp.sin(ang)


def norm_rope(x, g, cos, sin):
    xf = x.astype(jnp.float32)
    xf = xf * lax.rsqrt(jnp.mean(xf * xf, axis=-1, keepdims=True) + EPS) * g.astype(jnp.float32)
    half = xf.shape[-1] // 2
    rot = jnp.concatenate([-xf[..., half:], xf[..., :half]], axis=-1)
    y = xf * cos[None, :, None, :] + rot * sin[None, :, None, :]
    return y.astype(x.dtype)


def compress(t, idx, pos, w1, w2):
    blk = t[:, idx] + pos[None, None, :, None, :].astype(t.dtype)
    h = jax.nn.gelu(jnp.einsum("bnlgd,lde->bnge", blk, w1))
    return jnp.einsum("bnge,ed->bngd", h, w2)


def nsa_group(q, k_cmp, v_cmp, k_sel, v_sel, k_win, v_win, gates, cos, sin,
              q_gain, k_gain, cmp_pos, cmp_w1, cmp_w2):
    B, S = q.shape[0], q.shape[1]
    G, R, dh = NSA_KV_GROUPS, NSA_REP, NSA_HEAD_DIM
    dt = q.dtype
    scale = dh ** -0.5
    q = norm_rope(q, q_gain, cos, sin)
    kc = norm_rope(k_cmp, k_gain[0], cos, sin)
    ks = norm_rope(k_sel, k_gain[1], cos, sin)
    kw = norm_rope(k_win, k_gain[2], cos, sin)

    ncb = (S - CMP_BLOCK) // CMP_STRIDE + 1
    idx = jnp.arange(ncb)[:, None] * CMP_STRIDE + jnp.arange(CMP_BLOCK)[None, :]
    kc_c = compress(kc, idx, cmp_pos[0], cmp_w1[0], cmp_w2[0])
    vc_c = compress(v_cmp, idx, cmp_pos[1], cmp_w1[1], cmp_w2[1])
    cmp_end = jnp.arange(ncb) * CMP_STRIDE + CMP_BLOCK - 1

    nsb = S // SEL_BLOCK
    n_sel = min(SEL_TOPN, nsb)
    ks_blk = ks.reshape(B, nsb, SEL_BLOCK, G, dh).transpose(0, 3, 1, 2, 4)
    vs_blk = v_sel.reshape(B, nsb, SEL_BLOCK, G, dh).transpose(0, 3, 1, 2, 4)
    ci = jnp.arange(ncb)[:, None]
    sj = jnp.arange(nsb)[None, :]
    overlap = ((ci * CMP_STRIDE <= sj * SEL_BLOCK + SEL_BLOCK - 1)
               & (ci * CMP_STRIDE + CMP_BLOCK - 1 >= sj * SEL_BLOCK)).astype(jnp.float32)

    kw_pad = jnp.pad(kw, ((0, 0), (WINDOW, 0), (0, 0), (0, 0)))
    vw_pad = jnp.pad(v_win, ((0, 0), (WINDOW, 0), (0, 0), (0, 0)))
    bi = jnp.arange(B)[:, None, None, None]
    gi = jnp.arange(G)[None, :, None, None]
    jblk = jnp.arange(nsb)

    def block_fn(bq):
        start = bq * Q_BLOCK
        t = start + jnp.arange(Q_BLOCK)
        qb = lax.dynamic_slice_in_dim(q, start, Q_BLOCK, axis=1).reshape(B, Q_BLOCK, G, R, dh)
        gb = lax.dynamic_slice_in_dim(gates, start, Q_BLOCK, axis=1).reshape(B, Q_BLOCK, G, R, 3)

        s1 = jnp.einsum("bqgrd,bngd->bgrqn", qb, kc_c).astype(jnp.float32) * scale
        m1 = cmp_end[None, :] <= t[:, None]
        p1 = jnp.where(m1, jax.nn.softmax(jnp.where(m1, s1, NEG), axis=-1), 0.0)
        o_c = jnp.einsum("bgrqn,bngd->bqgrd", p1.astype(dt), vc_c)

        imp = jnp.einsum("bgrqn,nj->bgqj", p1, overlap)
        cur = t // SEL_BLOCK
        forced = (jblk[None, :] == 0) | (jblk[None, :] == cur[:, None]) | (jblk[None, :] == cur[:, None] - 1)
        valid = jblk[None, :] * SEL_BLOCK <= t[:, None]
        imp = jnp.where(valid, jnp.where(forced, jnp.inf, imp), -jnp.inf)
        _, sel = lax.top_k(imp, n_sel)
        kg = ks_blk[bi, gi, sel]
        vg = vs_blk[bi, gi, sel]
        s2 = jnp.einsum("bqgrd,bgqnld->bgrqnl", qb, kg).astype(jnp.float32) * scale
        kpos = sel[..., None] * SEL_BLOCK + jnp.arange(SEL_BLOCK)
        m2 = (kpos <= t[None, None, :, None, None])[:, :, None]
        s2 = jnp.where(m2, s2, NEG)
        p2 = jax.nn.softmax(s2.reshape(s2.shape[:4] + (-1,)), axis=-1).reshape(s2.shape)
        o_s = jnp.einsum("bgrqnl,bgqnld->bqgrd", p2.astype(dt), vg)

        kwb = lax.dynamic_slice_in_dim(kw_pad, start, Q_BLOCK + WINDOW, axis=1)
        vwb = lax.dynamic_slice_in_dim(vw_pad, start, Q_BLOCK + WINDOW, axis=1)
        kp = start - WINDOW + jnp.arange(Q_BLOCK + WINDOW)
        diff = t[:, None] - kp[None, :]
        m3 = (diff >= 0) & (diff < WINDOW) & (kp[None, :] >= 0)
        s3 = jnp.einsum("bqgrd,bkgd->bgrqk", qb, kwb).astype(jnp.float32) * scale
        p3 = jax.nn.softmax(jnp.where(m3, s3, NEG), axis=-1)
        o_w = jnp.einsum("bgrqk,bkgd->bqgrd", p3.astype(dt), vwb)

        o = gb[..., 0:1] * o_c + gb[..., 1:2] * o_s + gb[..., 2:3] * o_w
        return o.reshape(B, Q_BLOCK, NSA_HEADS * dh)

    out = lax.map(block_fn, jnp.arange(S // Q_BLOCK))
    return out.transpose(1, 0, 2, 3).reshape(B, S, NSA_HEADS * dh)


def hgrn2_group(hq, hf, hi, hg, lb, out_gain):
    B, S = hq.shape[0], hq.shape[1]
    H, dk, dv, C = HG_HEADS, HG_DK, HG_DV, S // HG_CHUNK
    dt = hq.dtype
    q = jax.nn.silu(hq.astype(jnp.float32)).reshape(B, S, H, dk)
    f = lb[None, None] + (1.0 - lb[None, None]) * jax.nn.sigmoid(hf.astype(jnp.float32).reshape(B, S, H, dk))
    logf = jnp.log(jnp.maximum(f, 1e-30))
    k = 1.0 - f
    v = hi.astype(jnp.float32).reshape(B, S, H, dv)

    def to_chunks(a):
        return a.reshape(B, C, HG_CHUNK, H, a.shape[-1]).transpose(1, 0, 3, 2, 4)

    tri = jnp.tril(jnp.ones((HG_CHUNK, HG_CHUNK), dtype=bool))

    def step(state, xs):
        qc, kc, vc, lfc = xs
        b = jnp.cumsum(lfc, axis=-2)
        decay = jnp.exp(jnp.where(tri[:, :, None], b[..., :, None, :] - b[..., None, :, :], -jnp.inf))
        a = jnp.einsum("bhtk,bhtsk,bhsk->bhts", qc, decay, kc)
        o = jnp.einsum("bhts,bhsv->bhtv", a, vc) + jnp.einsum("bhtk,bhkv->bhtv", qc * jnp.exp(b), state)
        bl = b[..., -1:, :]
        state = jnp.exp(bl)[..., 0, :, None] * state + jnp.einsum("bhsk,bhsv->bhkv", kc * jnp.exp(bl - b), vc)
        return state, o

    s0 = jnp.zeros((B, H, dk, dv), jnp.float32)
    _, o = lax.scan(step, s0, (to_chunks(q), to_chunks(k), to_chunks(v), to_chunks(logf)))
    o = o.transpose(1, 0, 3, 2, 4).reshape(B, S, H, dv)
    o = o * lax.rsqrt(jnp.mean(o * o, axis=-1, keepdims=True) + EPS) * out_gain.astype(jnp.float32)
    o = o * jax.nn.silu(hg.astype(jnp.float32).reshape(B, S, H, dv))
    return o.reshape(B, S, H * dv).astype(dt)


def setup_inputs(seed: int = 0) -> dict:
    key = jax.random.key(seed)
    ks = jax.random.split(key, 20)
    L, D, F, dh = DEPTH, D_MODEL, D_FF, NSA_HEAD_DIM

    def nrm(k, shape, scale):
        return jax.random.normal(k, shape, jnp.float32) * scale

    def gain(k, shape):
        return 1.0 + 0.01 * jax.random.normal(k, shape, jnp.float32)

    return {
        "x": nrm(ks[0], (BATCH, SEQ, D), 1.0),
        "ffn1_norm": gain(ks[1], (L, D)),
        "ffn1_w_gate": nrm(ks[2], (L, D, F), D ** -0.5),
        "ffn1_w_up": nrm(ks[3], (L, D, F), D ** -0.5),
        "ffn1_w_down": nrm(ks[4], (L, F, D), F ** -0.5),
        "mix_norm": gain(ks[5], (L, D)),
        "w_in": nrm(ks[6], (L, D, N_IN), D ** -0.5),
        "q_norm": gain(ks[7], (L, dh)),
        "k_norm": gain(ks[8], (L, 3, dh)),
        "cmp_pos": nrm(ks[9], (L, 2, CMP_BLOCK, dh), 0.02),
        "cmp_w1": nrm(ks[10], (L, 2, CMP_BLOCK, dh, CMP_HIDDEN), (CMP_BLOCK * dh) ** -0.5),
        "cmp_w2": nrm(ks[11], (L, 2, CMP_HIDDEN, dh), CMP_HIDDEN ** -0.5),
        "hgrn_lb_logits": nrm(ks[12], (L, HG_HEADS * HG_DK), 0.5),
        "hgrn_out_norm": gain(ks[13], (L, HG_DV)),
        "w_out": nrm(ks[14], (L, D_MIX, D), D_MIX ** -0.5),
        "ffn2_norm": gain(ks[15], (L, D)),
        "ffn2_w_gate": nrm(ks[16], (L, D, F), D ** -0.5),
        "ffn2_w_up": nrm(ks[17], (L, D, F), D ** -0.5),
        "ffn2_w_down": nrm(ks[18], (L, F, D), F ** -0.5),
    }


def reference(x, ffn1_norm, ffn1_w_gate, ffn1_w_up, ffn1_w_down, mix_norm, w_in, q_norm, k_norm,
              cmp_pos, cmp_w1, cmp_w2, hgrn_lb_logits, hgrn_out_norm, w_out,
              ffn2_norm, ffn2_w_gate, ffn2_w_up, ffn2_w_down):
    B, S = x.shape[0], x.shape[1]
    cos, sin = rope_tables(S, NSA_HEAD_DIM)
    lb_sm = jax.nn.softmax(hgrn_lb_logits.astype(jnp.float32), axis=0)
    lb_all = jnp.cumsum(lb_sm, axis=0) - lb_sm[0:1]
    sizes = [NSA_Q_COLS] + [NSA_KV_COLS] * 6 + [NSA_GATE_COLS, HG_K_COLS, HG_K_COLS, HG_V_COLS, HG_V_COLS]
    offsets = [int(o) for o in np.cumsum(sizes)[:-1]]

    for l in range(DEPTH):
        x = x + 0.5 * swiglu(rmsnorm(x, ffn1_norm[l]), ffn1_w_gate[l], ffn1_w_up[l], ffn1_w_down[l])
        h = rmsnorm(x, mix_norm[l])
        proj = h @ w_in[l]
        (q, kc, vc, ksl, vsl, kw, vw, gts, hq, hf, hi, hg) = jnp.split(proj, offsets, axis=-1)
        kv = lambda a: a.reshape(B, S, NSA_KV_GROUPS, NSA_HEAD_DIM)
        gates = jax.nn.sigmoid(gts.astype(jnp.float32)).reshape(B, S, NSA_HEADS, 3).astype(x.dtype)
        o_nsa = nsa_group(q.reshape(B, S, NSA_HEADS, NSA_HEAD_DIM), kv(kc), kv(vc), kv(ksl), kv(vsl),
                          kv(kw), kv(vw), gates, cos, sin, q_norm[l], k_norm[l],
                          cmp_pos[l], cmp_w1[l], cmp_w2[l])
        o_hg = hgrn2_group(hq, hf, hi, hg, lb_all[l].reshape(HG_HEADS, HG_DK), hgrn_out_norm[l])
        x = x + jnp.concatenate([o_nsa, o_hg], axis=-1) @ w_out[l]
        x = x + 0.5 * swiglu(rmsnorm(x, ffn2_norm[l]), ffn2_w_gate[l], ffn2_w_up[l], ffn2_w_down[l])
    return x
```

```python
import functools

import jax
import jax.numpy as jnp
import numpy as np
from jax import lax
from jax.experimental import pallas as pl
from jax.experimental.pallas import tpu as pltpu

NSA_HEADS = 8
NSA_GROUPS = 2
NSA_REP = NSA_HEADS // NSA_GROUPS
HEAD_DIM = 64
CMP_BLOCK = 32
CMP_STRIDE = 16
CMP_HIDDEN = 128
SEL_BLOCK = 64
SEL_TOPN = 16
WINDOW = 512
HG_HEADS = 4
HG_DK = 128
HG_DV = 128
ROPE_THETA = 10000.0
EPS = 1e-6
NEG = -1e30

LANES = 128
Q_TILE = 128
SEL_CHUNK = 512
HG_CHUNK = 64
HG_SUB = 16
HG_ROWS = 512
VMEM_LIMIT = 56 * 1024 * 1024

BF16 = jnp.bfloat16
F32 = jnp.float32


def _dot(a, b):
    return jnp.dot(a, b, preferred_element_type=F32)


def _dot_nt(a, b):
    return lax.dot_general(a, b, (((1,), (1,)), ((), ())), preferred_element_type=F32)


def _split_bf16(x, parts):
    out = []
    rem = x
    for _ in range(parts):
        p = rem.astype(BF16)
        out.append(p)
        rem = rem - p.astype(F32)
    return out


def _ffn_body(*refs, fuse_mix):
    if fuse_mix:
        (x_ref, ma_ref, mb_ref, woa_ref, wob_ref, g_ref, wg_ref, wu_ref, wd_ref,
         o_ref, xs_ref, hn_ref, acc_ref) = refs
    else:
        x_ref, g_ref, wg_ref, wu_ref, wd_ref, o_ref, xs_ref, hn_ref, acc_ref = refs
    f = pl.program_id(1)

    @pl.when(f == 0)
    def _():
        x = x_ref[...]
        if fuse_mix:
            x = x + _dot(ma_ref[...], woa_ref[...]) + _dot(mb_ref[...], wob_ref[...])
        xs_ref[...] = x
        ms = jnp.mean(x * x, axis=-1, keepdims=True)
        hn_ref[...] = (x * lax.rsqrt(ms + EPS) * g_ref[...]).astype(BF16)
        acc_ref[...] = jnp.zeros_like(acc_ref)

    hn = hn_ref[...]
    gate = _dot(hn, wg_ref[...])
    up = _dot(hn, wu_ref[...])
    act = (gate * jax.nn.sigmoid(gate) * up).astype(BF16)
    acc_ref[...] += _dot(act, wd_ref[...])

    @pl.when(f == pl.num_programs(1) - 1)
    def _():
        o_ref[...] = xs_ref[...] + 0.5 * acc_ref[...]


def _ffn(x, g, wg, wu, wd, mix=None, *, tm=1024, tf=256):
    t, d = x.shape
    ff = wg.shape[1]
    assert t % tm == 0 and ff % tf == 0
    fuse = mix is not None
    row = lambda i, f: (i, 0)
    in_specs = [pl.BlockSpec((tm, d), row)]
    args = [x]
    if fuse:
        ma, mb, woa, wob = mix
        in_specs += [pl.BlockSpec((tm, ma.shape[1]), row), pl.BlockSpec((tm, mb.shape[1]), row),
                     pl.BlockSpec(woa.shape, lambda i, f: (0, 0)),
                     pl.BlockSpec(wob.shape, lambda i, f: (0, 0))]
        args += [ma, mb, woa, wob]
    in_specs += [pl.BlockSpec((1, d), lambda i, f: (0, 0)),
                 pl.BlockSpec((d, tf), lambda i, f: (0, f)),
                 pl.BlockSpec((d, tf), lambda i, f: (0, f)),
                 pl.BlockSpec((tf, d), lambda i, f: (f, 0))]
    args += [g.reshape(1, d), wg, wu, wd]
    return pl.pallas_call(
        functools.partial(_ffn_body, fuse_mix=fuse),
        grid=(t // tm, ff // tf),
        in_specs=in_specs,
        out_specs=pl.BlockSpec((tm, d), row),
        out_shape=jax.ShapeDtypeStruct((t, d), F32),
        scratch_shapes=[pltpu.VMEM((tm, d), F32), pltpu.VMEM((tm, d), BF16), pltpu.VMEM((tm, d), F32)],
        compiler_params=pltpu.CompilerParams(
            dimension_semantics=("parallel", "arbitrary"), vmem_limit_bytes=VMEM_LIMIT),
        name="ffn_mix" if fuse else "ffn",
    )(*args)


_C_Q = 0
_C_KV = NSA_HEADS * HEAD_DIM
_C_HG = _C_KV + 6 * LANES
_C_GATE = _C_HG + 4 * HG_HEADS * HG_DK
_C_END = _C_GATE + NSA_GROUPS * LANES


def _inproj_body(x_ref, g_ref, w_ref, cos_ref, sin_ref, qg_ref, kg_ref,
                 q_ref, cmp_ref, kv_ref, gate_ref, h_ref):
    x = x_ref[...]
    ms = jnp.mean(x * x, axis=-1, keepdims=True)
    hn = (x * lax.rsqrt(ms + EPS) * g_ref[...]).astype(BF16)
    cos = cos_ref[...]
    sin = sin_ref[...]
    tm = x.shape[0]

    def proj(c0, width):
        return _dot(hn, w_ref[:, c0:c0 + width])

    ri = lax.broadcasted_iota(jnp.int32, (LANES, LANES), 0) // HEAD_DIM
    ci = lax.broadcasted_iota(jnp.int32, (LANES, LANES), 1) // HEAD_DIM
    head_mean = jnp.where(ri == ci, 1.0 / HEAD_DIM, 0.0).astype(BF16)
    lane = lax.broadcasted_iota(jnp.int32, (tm, LANES), 1)
    first_half = (lane % HEAD_DIM) < (HEAD_DIM // 2)

    def norm_rope(y, gain):
        sq_hi, sq_lo = _split_bf16(y * y, 2)
        msq = _dot(sq_hi, head_mean) + _dot(sq_lo, head_mean)
        yn = y * lax.rsqrt(msq + EPS) * gain
        rot = jnp.where(first_half,
                        -pltpu.roll(yn, LANES - HEAD_DIM // 2, 1),
                        pltpu.roll(yn, HEAD_DIM // 2, 1))
        return yn * cos + rot * sin

    def put_heads(ref_at, y):
        yb = y.astype(BF16)
        ref_at(0)[...] = yb[:, :HEAD_DIM]
        ref_at(1)[...] = yb[:, HEAD_DIM:]

    scale = HEAD_DIM ** -0.5
    for c in range(NSA_HEADS // 2):
        yq = norm_rope(proj(_C_Q + c * LANES, LANES), qg_ref[...]) * scale
        put_heads(lambda j, c=c: q_ref.at[0, 2 * c + j], yq)

    kc = norm_rope(proj(_C_KV, LANES), kg_ref[0:1, :])
    cmp_ref[0] = kc.astype(BF16)
    cmp_ref[1] = proj(_C_KV + LANES, LANES).astype(BF16)
    ks = norm_rope(proj(_C_KV + 2 * LANES, LANES), kg_ref[1:2, :])
    put_heads(lambda j: kv_ref.at[0, 0, j], ks)
    put_heads(lambda j: kv_ref.at[1, 0, j], proj(_C_KV + 3 * LANES, LANES))
    kw = norm_rope(proj(_C_KV + 4 * LANES, LANES), kg_ref[2:3, :])
    put_heads(lambda j: kv_ref.at[2, 0, j], kw)
    put_heads(lambda j: kv_ref.at[3, 0, j], proj(_C_KV + 5 * LANES, LANES))

    hw = HG_HEADS * HG_DK
    for i in range(4):
        h_ref[i] = proj(_C_HG + i * hw, hw)
    for gi in range(NSA_GROUPS):
        gate_ref[gi] = jax.nn.sigmoid(proj(_C_GATE + gi * LANES, LANES))


def _inproj(x, g, w2, cos2, sin2, qg2, kg2, batch, seq, *, tm=256):
    t, d = x.shape
    nst = seq // tm
    hw = HG_HEADS * HG_DK
    out_shape = (
        jax.ShapeDtypeStruct((batch, NSA_HEADS, seq, HEAD_DIM), BF16),
        jax.ShapeDtypeStruct((2, t, LANES), BF16),
        jax.ShapeDtypeStruct((4, batch, NSA_GROUPS, seq, HEAD_DIM), BF16),
        jax.ShapeDtypeStruct((NSA_GROUPS, t, LANES), F32),
        jax.ShapeDtypeStruct((4, t, hw), F32),
    )
    out_specs = (
        pl.BlockSpec((1, NSA_HEADS, tm, HEAD_DIM), lambda i: (i // nst, 0, i % nst, 0)),
        pl.BlockSpec((2, tm, LANES), lambda i: (0, i, 0)),
        pl.BlockSpec((4, 1, NSA_GROUPS, tm, HEAD_DIM), lambda i: (0, i // nst, 0, i % nst, 0)),
        pl.BlockSpec((NSA_GROUPS, tm, LANES), lambda i: (0, i, 0)),
        pl.BlockSpec((4, tm, hw), lambda i: (0, i, 0)),
    )
    in_specs = [
        pl.BlockSpec((tm, d), lambda i: (i, 0)),
        pl.BlockSpec((1, d), lambda i: (0, 0)),
        pl.BlockSpec(w2.shape, lambda i: (0, 0)),
        pl.BlockSpec((tm, LANES), lambda i: (i % nst, 0)),
        pl.BlockSpec((tm, LANES), lambda i: (i % nst, 0)),
        pl.BlockSpec((1, LANES), lambda i: (0, 0)),
        pl.BlockSpec((3, LANES), lambda i: (0, 0)),
    ]
    return pl.pallas_call(
        _inproj_body,
        grid=(t // tm,),
        in_specs=in_specs,
        out_specs=out_specs,
        out_shape=out_shape,
        compiler_params=pltpu.CompilerParams(
            dimension_semantics=("parallel",), vmem_limit_bytes=VMEM_LIMIT),
        name="inproj",
    )(x, g.reshape(1, d), w2, cos2, sin2, qg2, kg2)


def _compress_body(seg_ref, w1a_ref, w1b_ref, pa_ref, pb_ref, w2_ref, o_ref):
    segs = seg_ref[0, 0]
    w1a = w1a_ref[0]
    w1b = w1b_ref[0]
    first = _dot(segs, w1a)
    second = _dot(segs, w1b)
    pa_hi, pa_lo = _split_bf16(jnp.broadcast_to(pa_ref[0], (8, pa_ref.shape[2])), 2)
    pb_hi, pb_lo = _split_bf16(jnp.broadcast_to(pb_ref[0], (8, pb_ref.shape[2])), 2)
    cpos = _dot(pa_hi, w1a) + _dot(pa_lo, w1a) + _dot(pb_hi, w1b) + _dot(pb_lo, w1b)
    nseg = first.shape[0]
    pre = first + pltpu.roll(second, nseg - 1, 0) + cpos[0:1, :]
    hid = jax.nn.gelu(pre, approximate=True).astype(BF16)
    out = _dot(hid, w2_ref[0]).astype(BF16)
    o_ref[0, 0, 0] = out[:, :HEAD_DIM]
    o_ref[0, 0, 1] = out[:, HEAD_DIM:]


def _compress(cmp_in, w1a, w1b, pa, pb, w2, batch, seq):
    nseg = seq // CMP_STRIDE
    width = CMP_STRIDE * LANES
    segs = cmp_in.reshape(2, batch, nseg, width)
    return pl.pallas_call(
        _compress_body,
        grid=(2, batch),
        in_specs=[
            pl.BlockSpec((1, 1, nseg, width), lambda s, b: (s, b, 0, 0)),
            pl.BlockSpec((1,) + w1a.shape[1:], lambda s, b: (s, 0, 0)),
            pl.BlockSpec((1,) + w1b.shape[1:], lambda s, b: (s, 0, 0)),
            pl.BlockSpec((1, 1, width), lambda s, b: (s, 0, 0)),
            pl.BlockSpec((1, 1, width), lambda s, b: (s, 0, 0)),
            pl.BlockSpec((1,) + w2.shape[1:], lambda s, b: (s, 0, 0)),
        ],
        out_specs=pl.BlockSpec((1, 1, NSA_GROUPS, nseg, HEAD_DIM), lambda s, b: (s, b, 0, 0, 0)),
        out_shape=jax.ShapeDtypeStruct((2, batch, NSA_GROUPS, nseg, HEAD_DIM), BF16),
        compiler_params=pltpu.CompilerParams(
            dimension_semantics=("parallel", "parallel"), vmem_limit_bytes=VMEM_LIMIT),
        name="compress",
    )(segs, w1a, w1b, pa, pb, w2)


def _masked_softmax(s, ok):
    sm = jnp.where(ok, s, NEG)
    mx = jnp.max(sm, axis=-1, keepdims=True)
    e = jnp.where(ok, jnp.exp(sm - mx), 0.0)
    den = jnp.sum(e, axis=-1, keepdims=True)
    return e / jnp.where(den > 0.0, den, 1.0)


def _nsa_body(q_ref, cc_k_ref, cc_v_ref, ks_ref, vs_ref, kw_ref, vw_ref, gate_ref,
              ovl_ref, exp_ref, o_ref):
    qb = pl.program_id(2)
    start = qb * Q_TILE
    rows = NSA_REP * Q_TILE
    q4 = q_ref[0].reshape(rows, HEAD_DIM)
    tq = start + lax.broadcasted_iota(jnp.int32, (1, Q_TILE, 1), 1)

    kcc = cc_k_ref[0, 0, 0]
    vcc = cc_v_ref[0, 0, 0]
    ncb = kcc.shape[0]
    s1 = _dot_nt(q4, kcc).reshape(NSA_REP, Q_TILE, ncb)
    cend = lax.broadcasted_iota(jnp.int32, (1, 1, ncb), 2) * CMP_STRIDE + (CMP_BLOCK - 1)
    p1 = _masked_softmax(s1, cend <= tq)
    o_c = _dot(p1.reshape(rows, ncb).astype(BF16), vcc)

    p1sum = p1[0]
    for r in range(1, NSA_REP):
        p1sum = p1sum + p1[r]
    ps_hi, ps_lo = _split_bf16(p1sum, 2)
    ovl = ovl_ref[...]
    imp = _dot_nt(ovl, ps_hi) + _dot_nt(ovl, ps_lo)
    nsb = imp.shape[0]
    jb = lax.broadcasted_iota(jnp.int32, (nsb, Q_TILE), 0)
    tl = start + lax.broadcasted_iota(jnp.int32, (nsb, Q_TILE), 1)
    cur = tl // SEL_BLOCK
    forced = (jb == 0) | (jb == cur) | (jb == cur - 1)
    valid = jb * SEL_BLOCK <= tl
    impm = jnp.where(valid, jnp.where(forced, jnp.inf, imp), -jnp.inf)
    rank = jnp.zeros((nsb, Q_TILE), jnp.int32)
    for i in range(nsb):
        ri = impm[i:i + 1, :]
        beats = (ri > impm) | ((ri == impm) & (jb > i))
        rank = rank + beats.astype(jnp.int32)
    sel_t = jnp.where((rank < SEL_TOPN) & valid, 1.0, 0.0)
    sel = sel_t.T.astype(BF16)

    def sel_step(c, carry):
        m, l, acc = carry
        k0 = pl.multiple_of(c * SEL_CHUNK, SEL_CHUNK)
        kch = ks_ref[0, 0, 0, pl.ds(k0, SEL_CHUNK), :]
        vch = vs_ref[0, 0, 0, pl.ds(k0, SEL_CHUNK), :]
        s = _dot_nt(q4, kch).reshape(NSA_REP, Q_TILE, SEL_CHUNK)
        picked = _dot(sel, exp_ref[c])
        kpos = k0 + lax.broadcasted_iota(jnp.int32, (1, 1, SEL_CHUNK), 2)
        ok = (picked[None] > 0.5) & (kpos <= tq)
        sm = jnp.where(ok, s, NEG)
        m_new = jnp.maximum(m, jnp.max(sm, axis=-1, keepdims=True))
        p = jnp.where(ok, jnp.exp(sm - m_new), 0.0)
        alpha = jnp.exp(m - m_new)
        l = alpha * l + jnp.sum(p, axis=-1, keepdims=True)
        pv = _dot(p.reshape(rows, SEL_CHUNK).astype(BF16), vch)
        acc = alpha * acc + pv.reshape(NSA_REP, Q_TILE, HEAD_DIM)
        return m_new, l, acc

    n_chunks = (start + Q_TILE + SEL_CHUNK - 1) // SEL_CHUNK
    init = (jnp.full((NSA_REP, Q_TILE, 1), NEG, F32), jnp.zeros((NSA_REP, Q_TILE, 1), F32),
            jnp.zeros((NSA_REP, Q_TILE, HEAD_DIM), F32))
    _, l_s, acc_s = lax.fori_loop(0, n_chunks, sel_step, init)
    o_s = acc_s / l_s

    wkeys = WINDOW + Q_TILE
    w0 = pl.multiple_of(jnp.maximum(start - WINDOW, 0), Q_TILE)
    kwin = kw_ref[0, 0, 0, pl.ds(w0, wkeys), :]
    vwin = vw_ref[0, 0, 0, pl.ds(w0, wkeys), :]
    s3 = _dot_nt(q4, kwin).reshape(NSA_REP, Q_TILE, wkeys)
    dist = tq - (w0 + lax.broadcasted_iota(jnp.int32, (1, 1, wkeys), 2))
    p3 = _masked_softmax(s3, (dist >= 0) & (dist < WINDOW))
    o_w = _dot(p3.reshape(rows, wkeys).astype(BF16), vwin).reshape(NSA_REP, Q_TILE, HEAD_DIM)

    o_c = o_c.reshape(NSA_REP, Q_TILE, HEAD_DIM)
    gates = gate_ref[0]
    for r in range(NSA_REP):
        g0 = gates[:, 3 * r:3 * r + 1]
        g1 = gates[:, 3 * r + 1:3 * r + 2]
        g2 = gates[:, 3 * r + 2:3 * r + 3]
        o = g0 * o_c[r] + g1 * o_s[r] + g2 * o_w[r]
        o_ref[:, r * HEAD_DIM:(r + 1) * HEAD_DIM] = o.astype(o_ref.dtype)


def _nsa(q, cc, kv, gates, ovl_t, expand, batch, seq):
    nqb = seq // Q_TILE
    ncb_pad = cc.shape[3]
    kvspec = lambda which: pl.BlockSpec((1, 1, 1, seq, HEAD_DIM), lambda b, g, i: (which, b, g, 0, 0))
    ccspec = lambda which: pl.BlockSpec((1, 1, 1, ncb_pad, HEAD_DIM), lambda b, g, i: (which, b, g, 0, 0))
    return pl.pallas_call(
        _nsa_body,
        grid=(batch, NSA_GROUPS, nqb),
        in_specs=[
            pl.BlockSpec((1, NSA_REP, Q_TILE, HEAD_DIM), lambda b, g, i: (b, g, i, 0)),
            ccspec(0), ccspec(1), kvspec(0), kvspec(1), kvspec(2), kvspec(3),
            pl.BlockSpec((1, Q_TILE, LANES), lambda b, g, i: (g, b * nqb + i, 0)),
            pl.BlockSpec(ovl_t.shape, lambda b, g, i: (0, 0)),
            pl.BlockSpec(expand.shape, lambda b, g, i: (0, 0, 0)),
        ],
        out_specs=pl.BlockSpec((Q_TILE, NSA_REP * HEAD_DIM), lambda b, g, i: (b * nqb + i, g)),
        out_shape=jax.ShapeDtypeStruct((batch * seq, NSA_HEADS * HEAD_DIM), BF16),
        compiler_params=pltpu.CompilerParams(
            dimension_semantics=("parallel", "parallel", "arbitrary"), vmem_limit_bytes=VMEM_LIMIT),
        name="nsa",
    )(q, cc, cc, kv, kv, kv, kv, gates, ovl_t, expand)


def _hgrn_body(hq_ref, hf_ref, hi_ref, hg_ref, lbl_ref, gain_ref, o_ref,
               state_ref, b_ref, k_ref, v_ref, q_ref, oacc_ref, *, layer):
    @pl.when(pl.program_id(2) == 0)
    def _():
        state_ref[...] = jnp.zeros_like(state_ref)

    logits = lbl_ref[...]
    ex = jnp.exp(logits - jnp.max(logits, axis=0, keepdims=True))
    sm = ex / jnp.sum(ex, axis=0, keepdims=True)
    lb = jnp.zeros((1, HG_DK), F32)
    for i in range(1, layer + 1):
        lb = lb + sm[i:i + 1, :]

    ch = HG_CHUNK
    ri = lax.broadcasted_iota(jnp.int32, (ch, ch), 0)
    ci = lax.broadcasted_iota(jnp.int32, (ch, ch), 1)
    tril = jnp.where(ri >= ci, 1.0, 0.0).astype(BF16)
    sub_row = lax.broadcasted_iota(jnp.int32, (HG_SUB, HG_DK), 0)
    gain = gain_ref[...]

    def chunk(j, carry):
        r0 = pl.multiple_of(j * ch, ch)
        hq = hq_ref[0, pl.ds(r0, ch), :]
        hf = hf_ref[0, pl.ds(r0, ch), :]
        q = hq * jax.nn.sigmoid(hq)
        f = lb + (1.0 - lb) * jax.nn.sigmoid(hf)
        logf = jnp.log(jnp.maximum(f, 1e-30))
        lf = _split_bf16(logf, 3)
        b = _dot(tril, lf[0]) + _dot(tril, lf[1]) + _dot(tril, lf[2])
        b_ref[...] = b
        k_ref[...] = 1.0 - f
        v_ref[...] = hi_ref[0, pl.ds(r0, ch), :]
        q_ref[...] = q

        state_t = state_ref[...]
        oacc_ref[...] = _dot_nt((q * jnp.exp(b)).astype(BF16), state_t.astype(BF16))

        for ib in range(ch // HG_SUB):
            i0 = ib * HG_SUB
            qi = q_ref[i0:i0 + HG_SUB, :]
            bi = b_ref[i0:i0 + HG_SUB, :]
            o_i = oacc_ref[i0:i0 + HG_SUB, :]
            if ib > 0:
                ref_b = b_ref[i0 - 1:i0, :]
                qd = (qi * jnp.exp(bi - ref_b)).astype(BF16)
                kd = (k_ref[0:i0, :] * jnp.exp(ref_b - b_ref[0:i0, :])).astype(BF16)
                a = _dot_nt(qd, kd)
                o_i = o_i + _dot(a.astype(BF16), v_ref[0:i0, :].astype(BF16))
            for s in range(HG_SUB):
                bs = b_ref[i0 + s:i0 + s + 1, :]
                ks = k_ref[i0 + s:i0 + s + 1, :]
                vs = v_ref[i0 + s:i0 + s + 1, :]
                dec = jnp.exp(jnp.where(sub_row >= s, bi - bs, NEG))
                a_s = jnp.sum(qi * dec * ks, axis=-1, keepdims=True)
                o_i = o_i + a_s * vs
            oacc_ref[i0:i0 + HG_SUB, :] = o_i

        bl = b_ref[ch - 1:ch, :]
        kd_end = (k_ref[...] * jnp.exp(bl - b_ref[...])).astype(BF16)
        vt = v_ref[...].T.astype(BF16)
        state_ref[...] = state_t * jnp.exp(bl) + _dot(vt, kd_end)

        o = oacc_ref[...]
        o = o * lax.rsqrt(jnp.mean(o * o, axis=-1, keepdims=True) + EPS) * gain
        hg = hg_ref[0, pl.ds(r0, ch), :]
        o_ref[pl.ds(r0, ch), :] = (o * (hg * jax.nn.sigmoid(hg))).astype(o_ref.dtype)
        return carry

    lax.fori_loop(0, hq_ref.shape[1] // ch, chunk, 0)


def _hgrn(h_all, lb_logits, out_gain, layer, batch, seq):
    t = h_all.shape[1]
    nblk = seq // HG_ROWS
    nl = lb_logits.shape[0]
    hspec = lambda which: pl.BlockSpec((1, HG_ROWS, HG_DK), lambda b, h, c: (which, b * nblk + c, h))
    return pl.pallas_call(
        functools.partial(_hgrn_body, layer=layer),
        grid=(batch, HG_HEADS, nblk),
        in_specs=[hspec(0), hspec(1), hspec(2), hspec(3),
                  pl.BlockSpec((nl, HG_DK), lambda b, h, c: (0, h)),
                  pl.BlockSpec((1, HG_DV), lambda b, h, c: (0, 0))],
        out_specs=pl.BlockSpec((HG_ROWS, HG_DV), lambda b, h, c: (b * nblk + c, h)),
        out_shape=jax.ShapeDtypeStruct((t, HG_HEADS * HG_DV), BF16),
        scratch_shapes=[pltpu.VMEM((HG_DV, HG_DK), F32)] + [pltpu.VMEM((HG_CHUNK, HG_DK), F32)] * 5,
        compiler_params=pltpu.CompilerParams(
            dimension_semantics=("parallel", "parallel", "arbitrary"), vmem_limit_bytes=VMEM_LIMIT),
        name="hgrn",
    )(h_all, h_all, h_all, h_all, lb_logits, out_gain.reshape(1, HG_DV))


def _relayout_w_in(w):
    d = w.shape[0]
    nq = NSA_HEADS * HEAD_DIM
    nkv = 6 * NSA_GROUPS * HEAD_DIM
    ng = 3 * NSA_HEADS
    q_kv = w[:, :nq + nkv]
    gts = w[:, nq + nkv:nq + nkv + ng]
    hg = w[:, nq + nkv + ng:]
    per = 3 * NSA_REP
    gate_blocks = [jnp.pad(gts[:, gi * per:(gi + 1) * per], ((0, 0), (0, LANES - per)))
                   for gi in range(NSA_GROUPS)]
    out = jnp.concatenate([q_kv, hg] + gate_blocks, axis=1).astype(BF16)
    assert out.shape == (d, _C_END)
    return out


def _compress_weights(pos, w1, w2):
    eye = jnp.eye(NSA_GROUPS, dtype=F32)
    e = w1.shape[-1]

    def big(w1_half):
        y = jnp.einsum("sldh,gk->slgdkh", w1_half, eye)
        return y.reshape(2, CMP_STRIDE * NSA_GROUPS * HEAD_DIM, NSA_GROUPS * e).astype(BF16)

    def pos_rows(p_half):
        y = jnp.broadcast_to(p_half[:, :, None, :], (2, CMP_STRIDE, NSA_GROUPS, HEAD_DIM))
        return y.reshape(2, 1, CMP_STRIDE * NSA_GROUPS * HEAD_DIM)

    w2b = jnp.einsum("shd,gk->sghkd", w2, eye).reshape(2, NSA_GROUPS * e, NSA_GROUPS * HEAD_DIM)
    return (big(w1[:, :CMP_STRIDE]), big(w1[:, CMP_STRIDE:]),
            pos_rows(pos[:, :CMP_STRIDE]), pos_rows(pos[:, CMP_STRIDE:]), w2b.astype(BF16))


def kernel(x, ffn1_norm, ffn1_w_gate, ffn1_w_up, ffn1_w_down, mix_norm, w_in, q_norm, k_norm,
           cmp_pos, cmp_w1, cmp_w2, hgrn_lb_logits, hgrn_out_norm, w_out,
           ffn2_norm, ffn2_w_gate, ffn2_w_up, ffn2_w_down):
    batch, seq, d = x.shape
    depth = w_in.shape[0]
    assert CMP_BLOCK == 2 * CMP_STRIDE and seq % SEL_CHUNK == 0 and seq >= WINDOW + Q_TILE
    assert seq % HG_ROWS == 0 and SEL_CHUNK % SEL_BLOCK == 0
    t = batch * seq

    inv = 1.0 / (ROPE_THETA ** (jnp.arange(0, HEAD_DIM, 2, dtype=F32) / HEAD_DIM))
    ang = jnp.arange(seq, dtype=F32)[:, None] * inv[None, :]
    ang = jnp.concatenate([ang, ang, ang, ang], axis=-1)
    cos2, sin2 = jnp.cos(ang), jnp.sin(ang)

    nseg = seq // CMP_STRIDE
    nsb = seq // SEL_BLOCK
    ci = np.arange(nseg)[None, :]
    sj = np.arange(nsb)[:, None]
    ovl_t = ((ci * CMP_STRIDE <= sj * SEL_BLOCK + SEL_BLOCK - 1)
             & (ci * CMP_STRIDE + CMP_BLOCK - 1 >= sj * SEL_BLOCK)
             & (ci < (seq - CMP_BLOCK) // CMP_STRIDE + 1))
    ovl_t = jnp.asarray(ovl_t, dtype=BF16)
    key_blk = np.arange(seq) // SEL_BLOCK
    expand = (np.arange(nsb)[:, None] == key_blk[None, :]).astype(np.float32)
    expand = jnp.asarray(expand.reshape(nsb, seq // SEL_CHUNK, SEL_CHUNK).transpose(1, 0, 2), dtype=BF16)

    xt = x.reshape(t, d)
    for l in range(depth):
        xt = _ffn(xt, ffn1_norm[l], ffn1_w_gate[l].astype(BF16), ffn1_w_up[l].astype(BF16),
                  ffn1_w_down[l].astype(BF16))
        qg2 = jnp.tile(q_norm[l], 2).reshape(1, LANES)
        kg2 = jnp.tile(k_norm[l], (1, 2))
        q, cmp_in, kv, gates, h_all = _inproj(xt, mix_norm[l], _relayout_w_in(w_in[l]),
                                              cos2, sin2, qg2, kg2, batch, seq)
        w1a, w1b, pa, pb, w2b = _compress_weights(cmp_pos[l], cmp_w1[l], cmp_w2[l])
        cc = _compress(cmp_in, w1a, w1b, pa, pb, w2b, batch, seq)
        o_nsa = _nsa(q, cc, kv, gates, ovl_t, expand, batch, seq)
        o_hg = _hgrn(h_all, hgrn_lb_logits, hgrn_out_norm[l], l, batch, seq)
        nq = NSA_HEADS * HEAD_DIM
        wo = w_out[l].astype(BF16)
        xt = _ffn(xt, ffn2_norm[l], ffn2_w_gate[l].astype(BF16), ffn2_w_up[l].astype(BF16),
                  ffn2_w_down[l].astype(BF16), mix=(o_nsa, o_hg, wo[:nq], wo[nq:]))
    return xt.reshape(batch, seq, d)
```

```python
import functools

import jax
import jax.numpy as jnp
import numpy as np
from jax import lax
from jax.experimental import pallas as pl
from jax.experimental.pallas import tpu as pltpu

NSA_HEADS = 8
NSA_GROUPS = 2
NSA_REP = NSA_HEADS // NSA_GROUPS
HEAD_DIM = 64
CMP_BLOCK = 32
CMP_STRIDE = 16
CMP_HIDDEN = 128
SEL_BLOCK = 64
SEL_TOPN = 16
WINDOW = 512
HG_HEADS = 4
HG_DK = 128
HG_DV = 128
ROPE_THETA = 10000.0
EPS = 1e-6
NEG = -1e30
LOG2E = 1.4426950408889634

LANES = 128
Q_TILE = 128
SEL_CHUNK = 512
HG_CHUNK = 64
HG_SUB = 16
HG_ROWS = 512
VMEM_LIMIT = 56 * 1024 * 1024

BF16 = jnp.bfloat16
F32 = jnp.float32


def _dot(a, b):
    return jnp.dot(a, b, preferred_element_type=F32)


def _dot_nt(a, b):
    return lax.dot_general(a, b, (((1,), (1,)), ((), ())), preferred_element_type=F32)


def _split_bf16(x, parts):
    out = []
    rem = x
    for _ in range(parts):
        p = rem.astype(BF16)
        out.append(p)
        rem = rem - p.astype(F32)
    return out


def _ffn_body(*refs, fuse_mix):
    if fuse_mix:
        (x_ref, ma_ref, mb_ref, woa_ref, wob_ref, g_ref, wg_ref, wu_ref, wd_ref,
         o_ref, xs_ref, hn_ref, acc_ref) = refs
    else:
        x_ref, g_ref, wg_ref, wu_ref, wd_ref, o_ref, xs_ref, hn_ref, acc_ref = refs
    f = pl.program_id(1)

    @pl.when(f == 0)
    def _():
        x = x_ref[...]
        if fuse_mix:
            x = x + _dot(ma_ref[...], woa_ref[...]) + _dot(mb_ref[...], wob_ref[...])
        xs_ref[...] = x
        ms = jnp.mean(x * x, axis=-1, keepdims=True)
        hn_ref[...] = (x * lax.rsqrt(ms + EPS) * g_ref[...]).astype(BF16)
        acc_ref[...] = jnp.zeros_like(acc_ref)

    hn = hn_ref[...]
    gate = _dot(hn, wg_ref[...])
    up = _dot(hn, wu_ref[...])
    act = (gate * jax.nn.sigmoid(gate) * up).astype(BF16)
    acc_ref[...] += _dot(act, wd_ref[...])

    @pl.when(f == pl.num_programs(1) - 1)
    def _():
        o_ref[...] = xs_ref[...] + 0.5 * acc_ref[...]


def _ffn(x, g, wg, wu, wd, mix=None, *, tm=1024, tf=256):
    t, d = x.shape
    ff = wg.shape[1]
    assert t % tm == 0 and ff % tf == 0
    fuse = mix is not None
    row = lambda i, f: (i, 0)
    in_specs = [pl.BlockSpec((tm, d), row)]
    args = [x]
    if fuse:
        ma, mb, woa, wob = mix
        in_specs += [pl.BlockSpec((tm, ma.shape[1]), row), pl.BlockSpec((tm, mb.shape[1]), row),
                     pl.BlockSpec(woa.shape, lambda i, f: (0, 0)),
                     pl.BlockSpec(wob.shape, lambda i, f: (0, 0))]
        args += [ma, mb, woa, wob]
    in_specs += [pl.BlockSpec((1, d), lambda i, f: (0, 0)),
                 pl.BlockSpec((d, tf), lambda i, f: (0, f)),
                 pl.BlockSpec((d, tf), lambda i, f: (0, f)),
                 pl.BlockSpec((tf, d), lambda i, f: (f, 0))]
    args += [g.reshape(1, d), wg, wu, wd]
    return pl.pallas_call(
        functools.partial(_ffn_body, fuse_mix=fuse),
        grid=(t // tm, ff // tf),
        in_specs=in_specs,
        out_specs=pl.BlockSpec((tm, d), row),
        out_shape=jax.ShapeDtypeStruct((t, d), F32),
        scratch_shapes=[pltpu.VMEM((tm, d), F32), pltpu.VMEM((tm, d), BF16), pltpu.VMEM((tm, d), F32)],
        compiler_params=pltpu.CompilerParams(
            dimension_semantics=("parallel", "arbitrary"), vmem_limit_bytes=VMEM_LIMIT),
        name="ffn_mix" if fuse else "ffn",
    )(*args)


_C_Q = 0
_C_KV = NSA_HEADS * HEAD_DIM
_C_HG = _C_KV + 6 * LANES
_C_GATE = _C_HG + 4 * HG_HEADS * HG_DK
_C_END = _C_GATE + NSA_GROUPS * LANES
GATE_ROWS = 16
Q_SCALE = HEAD_DIM ** -0.5 * LOG2E


def _inproj_body(x_ref, g_ref, w_ref, cos_ref, sin_ref, cost_ref, sint_ref, qg_ref, kg_ref,
                 q_ref, cmp_ref, k_ref, vt_ref, gate_ref, h_ref):
    x = x_ref[...]
    ms = jnp.mean(x * x, axis=-1, keepdims=True)
    hn = (x * lax.rsqrt(ms + EPS) * g_ref[...]).astype(BF16)
    cos = cos_ref[...]
    sin = sin_ref[...]
    tm = x.shape[0]
    half = HEAD_DIM // 2

    def proj(c0, width):
        return _dot(hn, w_ref[:, c0:c0 + width])

    ri = lax.broadcasted_iota(jnp.int32, (LANES, LANES), 0) // HEAD_DIM
    ci = lax.broadcasted_iota(jnp.int32, (LANES, LANES), 1) // HEAD_DIM
    head_mean = jnp.where(ri == ci, 1.0 / HEAD_DIM, 0.0).astype(BF16)
    lane = lax.broadcasted_iota(jnp.int32, (tm, LANES), 1)
    first_half = (lane % HEAD_DIM) < half

    def norm_rope(y, gain):
        sq_hi, sq_lo = _split_bf16(y * y, 2)
        msq = _dot(sq_hi, head_mean) + _dot(sq_lo, head_mean)
        yn = y * lax.rsqrt(msq + EPS) * gain
        rot = jnp.where(first_half, -pltpu.roll(yn, LANES - half, 1), pltpu.roll(yn, half, 1))
        return yn * cos + rot * sin

    def heads_t(y):
        return y.T.reshape(2, HEAD_DIM, tm)

    qgain = (qg_ref[...] * Q_SCALE)[None]
    cost = cost_ref[...][None]
    sint = sint_ref[...][None]
    for c in range(NSA_HEADS // 2):
        y3 = heads_t(proj(_C_Q + c * LANES, LANES))
        yn = y3 * lax.rsqrt(jnp.mean(y3 * y3, axis=1, keepdims=True) + EPS) * qgain
        rot = jnp.concatenate([-yn[:, half:], yn[:, :half]], axis=1)
        q_ref[0, 2 * c:2 * c + 2] = (yn * cost + rot * sint).astype(BF16)

    kc = norm_rope(proj(_C_KV, LANES), kg_ref[0:1, :])
    cmp_ref[0] = kc.astype(BF16)
    cmp_ref[1] = proj(_C_KV + LANES, LANES).astype(BF16)
    for which, col in ((0, 2), (1, 4)):
        kk = norm_rope(proj(_C_KV + col * LANES, LANES), kg_ref[which + 1:which + 2, :]).astype(BF16)
        k_ref[which, 0, 0] = kk[:, :HEAD_DIM]
        k_ref[which, 0, 1] = kk[:, HEAD_DIM:]
    for which, col in ((0, 3), (1, 5)):
        vt_ref[which, 0] = heads_t(proj(_C_KV + col * LANES, LANES)).astype(BF16)

    hw = HG_HEADS * HG_DK
    for i in range(4):
        h_ref[i] = proj(_C_HG + i * hw, hw)
    for gi in range(NSA_GROUPS):
        gt = jax.nn.sigmoid(proj(_C_GATE + gi * LANES, LANES)).T
        gate_ref[gi] = gt[:GATE_ROWS]


def _inproj(x, g, w2, cos2, sin2, cost, sint, qgt, kg2, batch, seq, *, tm=256):
    t, d = x.shape
    nst = seq // tm
    hw = HG_HEADS * HG_DK
    out_shape = (
        jax.ShapeDtypeStruct((batch, NSA_HEADS, HEAD_DIM, seq), BF16),
        jax.ShapeDtypeStruct((2, t, LANES), BF16),
        jax.ShapeDtypeStruct((2, batch, NSA_GROUPS, seq, HEAD_DIM), BF16),
        jax.ShapeDtypeStruct((2, batch, NSA_GROUPS, HEAD_DIM, seq), BF16),
        jax.ShapeDtypeStruct((NSA_GROUPS, GATE_ROWS, t), F32),
        jax.ShapeDtypeStruct((4, t, hw), F32),
    )
    out_specs = (
        pl.BlockSpec((1, NSA_HEADS, HEAD_DIM, tm), lambda i: (i // nst, 0, 0, i % nst)),
        pl.BlockSpec((2, tm, LANES), lambda i: (0, i, 0)),
        pl.BlockSpec((2, 1, NSA_GROUPS, tm, HEAD_DIM), lambda i: (0, i // nst, 0, i % nst, 0)),
        pl.BlockSpec((2, 1, NSA_GROUPS, HEAD_DIM, tm), lambda i: (0, i // nst, 0, 0, i % nst)),
        pl.BlockSpec((NSA_GROUPS, GATE_ROWS, tm), lambda i: (0, 0, i)),
        pl.BlockSpec((4, tm, hw), lambda i: (0, i, 0)),
    )
    in_specs = [
        pl.BlockSpec((tm, d), lambda i: (i, 0)),
        pl.BlockSpec((1, d), lambda i: (0, 0)),
        pl.BlockSpec(w2.shape, lambda i: (0, 0)),
        pl.BlockSpec((tm, LANES), lambda i: (i % nst, 0)),
        pl.BlockSpec((tm, LANES), lambda i: (i % nst, 0)),
        pl.BlockSpec((HEAD_DIM, tm), lambda i: (0, i % nst)),
        pl.BlockSpec((HEAD_DIM, tm), lambda i: (0, i % nst)),
        pl.BlockSpec((HEAD_DIM, tm), lambda i: (0, 0)),
        pl.BlockSpec((3, LANES), lambda i: (0, 0)),
    ]
    return pl.pallas_call(
        _inproj_body,
        grid=(t // tm,),
        in_specs=in_specs,
        out_specs=out_specs,
        out_shape=out_shape,
        compiler_params=pltpu.CompilerParams(
            dimension_semantics=("parallel",), vmem_limit_bytes=VMEM_LIMIT),
        name="inproj",
    )(x, g.reshape(1, d), w2, cos2, sin2, cost, sint, qgt, kg2)


def _compress_body(seg_ref, w1a_ref, w1b_ref, pa_ref, pb_ref, w2_ref, o_ref, ot_ref):
    segs = seg_ref[0, 0]
    w1a = w1a_ref[0]
    w1b = w1b_ref[0]
    first = _dot(segs, w1a)
    second = _dot(segs, w1b)
    pa_hi, pa_lo = _split_bf16(jnp.broadcast_to(pa_ref[0], (8, pa_ref.shape[2])), 2)
    pb_hi, pb_lo = _split_bf16(jnp.broadcast_to(pb_ref[0], (8, pb_ref.shape[2])), 2)
    cpos = _dot(pa_hi, w1a) + _dot(pa_lo, w1a) + _dot(pb_hi, w1b) + _dot(pb_lo, w1b)
    nseg = first.shape[0]
    pre = first + pltpu.roll(second, nseg - 1, 0) + cpos[0:1, :]
    hid = jax.nn.gelu(pre, approximate=True).astype(BF16)
    out = _dot(hid, w2_ref[0])
    outb = out.astype(BF16)
    o_ref[0, 0, 0] = outb[:, :HEAD_DIM]
    o_ref[0, 0, 1] = outb[:, HEAD_DIM:]
    ot_ref[0, 0] = out.T.reshape(NSA_GROUPS, HEAD_DIM, nseg).astype(BF16)


def _compress(cmp_in, w1a, w1b, pa, pb, w2, batch, seq):
    nseg = seq // CMP_STRIDE
    width = CMP_STRIDE * LANES
    segs = cmp_in.reshape(2, batch, nseg, width)
    return pl.pallas_call(
        _compress_body,
        grid=(2, batch),
        in_specs=[
            pl.BlockSpec((1, 1, nseg, width), lambda s, b: (s, b, 0, 0)),
            pl.BlockSpec((1,) + w1a.shape[1:], lambda s, b: (s, 0, 0)),
            pl.BlockSpec((1,) + w1b.shape[1:], lambda s, b: (s, 0, 0)),
            pl.BlockSpec((1, 1, width), lambda s, b: (s, 0, 0)),
            pl.BlockSpec((1, 1, width), lambda s, b: (s, 0, 0)),
            pl.BlockSpec((1,) + w2.shape[1:], lambda s, b: (s, 0, 0)),
        ],
        out_specs=(pl.BlockSpec((1, 1, NSA_GROUPS, nseg, HEAD_DIM), lambda s, b: (s, b, 0, 0, 0)),
                   pl.BlockSpec((1, 1, NSA_GROUPS, HEAD_DIM, nseg), lambda s, b: (s, b, 0, 0, 0))),
        out_shape=(jax.ShapeDtypeStruct((2, batch, NSA_GROUPS, nseg, HEAD_DIM), BF16),
                   jax.ShapeDtypeStruct((2, batch, NSA_GROUPS, HEAD_DIM, nseg), BF16)),
        compiler_params=pltpu.CompilerParams(
            dimension_semantics=("parallel", "parallel"), vmem_limit_bytes=VMEM_LIMIT),
        name="compress",
    )(segs, w1a, w1b, pa, pb, w2)


def _tile_heads(a):
    return jnp.concatenate([a] * NSA_REP, axis=1)


def _nsa_body(q_ref, kcc_ref, vcct_ref, ks_ref, kw_ref, vst_ref, vwt_ref, gate_ref, ovl_ref,
              o_ref, imp_ref, selb_ref):
    qb = pl.program_id(2)
    start = qb * Q_TILE
    q_t = jnp.concatenate([q_ref[0, r] for r in range(NSA_REP)], axis=1)
    tq = start + lax.broadcasted_iota(jnp.int32, (1, Q_TILE), 1)

    kcc = kcc_ref[0, 0, 0]
    ncb = kcc.shape[0]
    s1 = _dot(kcc, q_t)
    cend = lax.broadcasted_iota(jnp.int32, (ncb, 1), 0) * CMP_STRIDE + (CMP_BLOCK - 1)
    sm1 = s1 + _tile_heads(jnp.where(cend <= tq, 0.0, NEG))
    e1 = jnp.exp2(sm1 - jnp.max(sm1, axis=0, keepdims=True))
    den1 = jnp.sum(e1, axis=0, keepdims=True)
    p1 = e1 * jnp.where(_tile_heads(tq >= CMP_BLOCK - 1), 1.0 / den1, 0.0)
    o_c = _dot(vcct_ref[0, 0, 0], p1.astype(BF16))

    p1sum = p1[:, 0:Q_TILE]
    for r in range(1, NSA_REP):
        p1sum = p1sum + p1[:, r * Q_TILE:(r + 1) * Q_TILE]
    ps_hi, ps_lo = _split_bf16(p1sum, 2)
    ovl = ovl_ref[...]
    imp = _dot(ovl, ps_hi) + _dot(ovl, ps_lo)
    nsb = imp.shape[0]
    jb = lax.broadcasted_iota(jnp.int32, (nsb, Q_TILE), 0)
    cur = tq // SEL_BLOCK
    forced = (jb == 0) | (jb == cur) | (jb == cur - 1)
    valid = jb * SEL_BLOCK <= tq
    impm = jnp.where(valid, jnp.where(forced, jnp.inf, imp), -jnp.inf)
    imp_ref[...] = impm
    sub = lax.broadcasted_iota(jnp.int32, (8, Q_TILE), 0)
    ranks = [jnp.zeros((8, Q_TILE), F32) for _ in range(nsb // 8)]
    for i in range(nsb):
        xi = jnp.broadcast_to(imp_ref[i:i + 1, :], (8, Q_TILE))
        for v in range(nsb // 8):
            xv = impm[8 * v:8 * v + 8, :]
            ge = jnp.where(xi >= xv, 1.0, 0.0)
            gt = jnp.where(xi > xv, 1.0, 0.0)
            if 8 * v > i:
                inc = ge
            elif 8 * v + 7 < i:
                inc = gt
            else:
                inc = jnp.where(sub + 8 * v > i, ge, gt)
            ranks[v] = ranks[v] + inc
    rank = jnp.concatenate(ranks, axis=0)
    selb_ref[...] = jnp.where((rank < SEL_TOPN) & valid, 0.0, NEG)

    blocks_per_chunk = SEL_CHUNK // SEL_BLOCK
    kp_rel = lax.broadcasted_iota(jnp.int32, (SEL_CHUNK, 1), 0)

    def sel_step(c, carry):
        m, l, acc = carry
        k0 = pl.multiple_of(c * SEL_CHUNK, SEL_CHUNK)
        s = _dot(ks_ref[0, 0, 0, pl.ds(k0, SEL_CHUNK), :], q_t)
        rows = [jnp.broadcast_to(selb_ref[pl.ds(c * blocks_per_chunk + j, 1), :], (SEL_BLOCK, Q_TILE))
                for j in range(blocks_per_chunk)]
        bias = jnp.where(k0 + kp_rel <= tq, jnp.concatenate(rows, axis=0), NEG)
        sm = s + _tile_heads(bias)
        m_new = jnp.maximum(m, jnp.max(sm, axis=0, keepdims=True))
        p = jnp.exp2(sm - m_new)
        alpha = jnp.exp2(m - m_new)
        l = alpha * l + jnp.sum(p, axis=0, keepdims=True)
        acc = alpha * acc + _dot(vst_ref[0, 0, 0, :, pl.ds(k0, SEL_CHUNK)], p.astype(BF16))
        return m_new, l, acc

    n_chunks = (start + Q_TILE + SEL_CHUNK - 1) // SEL_CHUNK
    rows = NSA_REP * Q_TILE
    init = (jnp.full((1, rows), NEG, F32), jnp.zeros((1, rows), F32), jnp.zeros((HEAD_DIM, rows), F32))
    _, l_s, acc_s = lax.fori_loop(0, n_chunks, sel_step, init)

    wkeys = WINDOW + Q_TILE
    w0 = pl.multiple_of(jnp.maximum(start - WINDOW, 0), Q_TILE)
    s3 = _dot(kw_ref[0, 0, 0, pl.ds(w0, wkeys), :], q_t)
    dist = tq - (w0 + lax.broadcasted_iota(jnp.int32, (wkeys, 1), 0))
    sm3 = s3 + _tile_heads(jnp.where((dist >= 0) & (dist < WINDOW), 0.0, NEG))
    e3 = jnp.exp2(sm3 - jnp.max(sm3, axis=0, keepdims=True))
    den3 = jnp.sum(e3, axis=0, keepdims=True)
    acc_w = _dot(vwt_ref[0, 0, 0, :, pl.ds(w0, wkeys)], e3.astype(BF16))

    gates = gate_ref[0]
    outs = []
    for r in range(NSA_REP):
        sl = slice(r * Q_TILE, (r + 1) * Q_TILE)
        g_c = gates[3 * r:3 * r + 1, :]
        g_s = gates[3 * r + 1:3 * r + 2, :] / l_s[:, sl]
        g_w = gates[3 * r + 2:3 * r + 3, :] / den3[:, sl]
        outs.append(g_c * o_c[:, sl] + g_s * acc_s[:, sl] + g_w * acc_w[:, sl])
    o_ref[...] = jnp.concatenate(outs, axis=0).T.astype(o_ref.dtype)


def _nsa(q_t, cc, cc_t, kk, vt, gates_t, ovl, batch, seq):
    nqb = seq // Q_TILE
    ncb_pad = cc.shape[3]
    nsb = seq // SEL_BLOCK
    kspec = lambda which: pl.BlockSpec((1, 1, 1, seq, HEAD_DIM), lambda b, g, i: (which, b, g, 0, 0))
    vspec = lambda which: pl.BlockSpec((1, 1, 1, HEAD_DIM, seq), lambda b, g, i: (which, b, g, 0, 0))
    return pl.pallas_call(
        _nsa_body,
        grid=(batch, NSA_GROUPS, nqb),
        in_specs=[
            pl.BlockSpec((1, NSA_REP, HEAD_DIM, Q_TILE), lambda b, g, i: (b, g, 0, i)),
            pl.BlockSpec((1, 1, 1, ncb_pad, HEAD_DIM), lambda b, g, i: (0, b, g, 0, 0)),
            pl.BlockSpec((1, 1, 1, HEAD_DIM, ncb_pad), lambda b, g, i: (1, b, g, 0, 0)),
            kspec(0), kspec(1), vspec(0), vspec(1),
            pl.BlockSpec((1, GATE_ROWS, Q_TILE), lambda b, g, i: (g, 0, b * nqb + i)),
            pl.BlockSpec(ovl.shape, lambda b, g, i: (0, 0)),
        ],
        out_specs=pl.BlockSpec((Q_TILE, NSA_REP * HEAD_DIM), lambda b, g, i: (b * nqb + i, g)),
        out_shape=jax.ShapeDtypeStruct((batch * seq, NSA_HEADS * HEAD_DIM), BF16),
        scratch_shapes=[pltpu.VMEM((nsb, Q_TILE), F32), pltpu.VMEM((nsb, Q_TILE), F32)],
        compiler_params=pltpu.CompilerParams(
            dimension_semantics=("parallel", "parallel", "arbitrary"), vmem_limit_bytes=VMEM_LIMIT),
        name="nsa",
    )(q_t, cc, cc_t, kk, kk, vt, vt, gates_t, ovl)


def _hgrn_body(hq_ref, hf_ref, hi_ref, hg_ref, lbl_ref, gain_ref, o_ref,
               state_ref, b_ref, k_ref, v_ref, q_ref, oacc_ref, *, layer):
    @pl.when(pl.program_id(2) == 0)
    def _():
        state_ref[...] = jnp.zeros_like(state_ref)

    logits = lbl_ref[...]
    ex = jnp.exp(logits - jnp.max(logits, axis=0, keepdims=True))
    sm = ex / jnp.sum(ex, axis=0, keepdims=True)
    lb = jnp.zeros((1, HG_DK), F32)
    for i in range(1, layer + 1):
        lb = lb + sm[i:i + 1, :]

    ch = HG_CHUNK
    ri = lax.broadcasted_iota(jnp.int32, (ch, ch), 0)
    ci = lax.broadcasted_iota(jnp.int32, (ch, ch), 1)
    tril = jnp.where(ri >= ci, 1.0, 0.0).astype(BF16)
    sub_row = lax.broadcasted_iota(jnp.int32, (HG_SUB, HG_DK), 0)
    gain = gain_ref[...]

    def chunk(j, carry):
        r0 = pl.multiple_of(j * ch, ch)
        hq = hq_ref[0, pl.ds(r0, ch), :]
        hf = hf_ref[0, pl.ds(r0, ch), :]
        q = hq * jax.nn.sigmoid(hq)
        f = lb + (1.0 - lb) * jax.nn.sigmoid(hf)
        logf = jnp.log(jnp.maximum(f, 1e-30))
        lf = _split_bf16(logf, 3)
        b = _dot(tril, lf[0]) + _dot(tril, lf[1]) + _dot(tril, lf[2])
        b_ref[...] = b
        k_ref[...] = 1.0 - f
        v_ref[...] = hi_ref[0, pl.ds(r0, ch), :]
        q_ref[...] = q

        state_t = state_ref[...]
        oacc_ref[...] = _dot_nt((q * jnp.exp(b)).astype(BF16), state_t.astype(BF16))

        for ib in range(ch // HG_SUB):
            i0 = ib * HG_SUB
            qi = q_ref[i0:i0 + HG_SUB, :]
            bi = b_ref[i0:i0 + HG_SUB, :]
            o_i = oacc_ref[i0:i0 + HG_SUB, :]
            if ib > 0:
                ref_b = b_ref[i0 - 1:i0, :]
                qd = (qi * jnp.exp(bi - ref_b)).astype(BF16)
                kd = (k_ref[0:i0, :] * jnp.exp(ref_b - b_ref[0:i0, :])).astype(BF16)
                a = _dot_nt(qd, kd)
                o_i = o_i + _dot(a.astype(BF16), v_ref[0:i0, :].astype(BF16))
            for s in range(HG_SUB):
                bs = b_ref[i0 + s:i0 + s + 1, :]
                ks = k_ref[i0 + s:i0 + s + 1, :]
                vs = v_ref[i0 + s:i0 + s + 1, :]
                dec = jnp.exp(jnp.where(sub_row >= s, bi - bs, NEG))
                a_s = jnp.sum(qi * dec * ks, axis=-1, keepdims=True)
                o_i = o_i + a_s * vs
            oacc_ref[i0:i0 + HG_SUB, :] = o_i

        bl = b_ref[ch - 1:ch, :]
        kd_end = (k_ref[...] * jnp.exp(bl - b_ref[...])).astype(BF16)
        vt = v_ref[...].T.astype(BF16)
        state_ref[...] = state_t * jnp.exp(bl) + _dot(vt, kd_end)

        o = oacc_ref[...]
        o = o * lax.rsqrt(jnp.mean(o * o, axis=-1, keepdims=True) + EPS) * gain
        hg = hg_ref[0, pl.ds(r0, ch), :]
        o_ref[pl.ds(r0, ch), :] = (o * (hg * jax.nn.sigmoid(hg))).astype(o_ref.dtype)
        return carry

    lax.fori_loop(0, hq_ref.shape[1] // ch, chunk, 0)


def _hgrn(h_all, lb_logits, out_gain, layer, batch, seq):
    t = h_all.shape[1]
    nblk = seq // HG_ROWS
    nl = lb_logits.shape[0]
    hspec = lambda which: pl.BlockSpec((1, HG_ROWS, HG_DK), lambda b, h, c: (which, b * nblk + c, h))
    return pl.pallas_call(
        functools.partial(_hgrn_body, layer=layer),
        grid=(batch, HG_HEADS, nblk),
        in_specs=[hspec(0), hspec(1), hspec(2), hspec(3),
                  pl.BlockSpec((nl, HG_DK), lambda b, h, c: (0, h)),
                  pl.BlockSpec((1, HG_DV), lambda b, h, c: (0, 0))],
        out_specs=pl.BlockSpec((HG_ROWS, HG_DV), lambda b, h, c: (b * nblk + c, h)),
        out_shape=jax.ShapeDtypeStruct((t, HG_HEADS * HG_DV), BF16),
        scratch_shapes=[pltpu.VMEM((HG_DV, HG_DK), F32)] + [pltpu.VMEM((HG_CHUNK, HG_DK), F32)] * 5,
        compiler_params=pltpu.CompilerParams(
            dimension_semantics=("parallel", "parallel", "arbitrary"), vmem_limit_bytes=VMEM_LIMIT),
        name="hgrn",
    )(h_all, h_all, h_all, h_all, lb_logits, out_gain.reshape(1, HG_DV))


def _relayout_w_in(w):
    d = w.shape[0]
    nq = NSA_HEADS * HEAD_DIM
    nkv = 6 * NSA_GROUPS * HEAD_DIM
    ng = 3 * NSA_HEADS
    q_kv = w[:, :nq + nkv]
    gts = w[:, nq + nkv:nq + nkv + ng]
    hg = w[:, nq + nkv + ng:]
    per = 3 * NSA_REP
    gate_blocks = [jnp.pad(gts[:, gi * per:(gi + 1) * per], ((0, 0), (0, LANES - per)))
                   for gi in range(NSA_GROUPS)]
    out = jnp.concatenate([q_kv, hg] + gate_blocks, axis=1).astype(BF16)
    assert out.shape == (d, _C_END)
    return out


def _compress_weights(pos, w1, w2):
    eye = jnp.eye(NSA_GROUPS, dtype=F32)
    e = w1.shape[-1]

    def big(w1_half):
        y = jnp.einsum("sldh,gk->slgdkh", w1_half, eye)
        return y.reshape(2, CMP_STRIDE * NSA_GROUPS * HEAD_DIM, NSA_GROUPS * e).astype(BF16)

    def pos_rows(p_half):
        y = jnp.broadcast_to(p_half[:, :, None, :], (2, CMP_STRIDE, NSA_GROUPS, HEAD_DIM))
        return y.reshape(2, 1, CMP_STRIDE * NSA_GROUPS * HEAD_DIM)

    w2b = jnp.einsum("shd,gk->sghkd", w2, eye).reshape(2, NSA_GROUPS * e, NSA_GROUPS * HEAD_DIM)
    return (big(w1[:, :CMP_STRIDE]), big(w1[:, CMP_STRIDE:]),
            pos_rows(pos[:, :CMP_STRIDE]), pos_rows(pos[:, CMP_STRIDE:]), w2b.astype(BF16))


def kernel(x, ffn1_norm, ffn1_w_gate, ffn1_w_up, ffn1_w_down, mix_norm, w_in, q_norm, k_norm,
           cmp_pos, cmp_w1, cmp_w2, hgrn_lb_logits, hgrn_out_norm, w_out,
           ffn2_norm, ffn2_w_gate, ffn2_w_up, ffn2_w_down):
    batch, seq, d = x.shape
    depth = w_in.shape[0]
    assert CMP_BLOCK == 2 * CMP_STRIDE and seq % SEL_CHUNK == 0 and seq >= WINDOW + Q_TILE
    assert seq % HG_ROWS == 0 and SEL_CHUNK % SEL_BLOCK == 0
    t = batch * seq
    tm_proj = 256

    inv = 1.0 / (ROPE_THETA ** (jnp.arange(0, HEAD_DIM, 2, dtype=F32) / HEAD_DIM))
    ang = jnp.arange(seq, dtype=F32)[:, None] * inv[None, :]
    ang = jnp.concatenate([ang, ang], axis=-1)
    cos1, sin1 = jnp.cos(ang), jnp.sin(ang)
    cos2, sin2 = jnp.tile(cos1, (1, 2)), jnp.tile(sin1, (1, 2))
    cost, sint = cos1.T, sin1.T

    nseg = seq // CMP_STRIDE
    nsb = seq // SEL_BLOCK
    ci = np.arange(nseg)[None, :]
    sj = np.arange(nsb)[:, None]
    ovl = ((ci * CMP_STRIDE <= sj * SEL_BLOCK + SEL_BLOCK - 1)
           & (ci * CMP_STRIDE + CMP_BLOCK - 1 >= sj * SEL_BLOCK)
           & (ci < (seq - CMP_BLOCK) // CMP_STRIDE + 1))
    ovl = jnp.asarray(ovl, dtype=BF16)

    xt = x.reshape(t, d)
    for l in range(depth):
        xt = _ffn(xt, ffn1_norm[l], ffn1_w_gate[l].astype(BF16), ffn1_w_up[l].astype(BF16),
                  ffn1_w_down[l].astype(BF16))
        qgt = jnp.broadcast_to(q_norm[l][:, None], (HEAD_DIM, tm_proj))
        kg2 = jnp.tile(k_norm[l], (1, 2))
        q_t, cmp_in, kk, vt, gates_t, h_all = _inproj(
            xt, mix_norm[l], _relayout_w_in(w_in[l]), cos2, sin2, cost, sint, qgt, kg2,
            batch, seq, tm=tm_proj)
        w1a, w1b, pa, pb, w2b = _compress_weights(cmp_pos[l], cmp_w1[l], cmp_w2[l])
        cc, cc_t = _compress(cmp_in, w1a, w1b, pa, pb, w2b, batch, seq)
        o_nsa = _nsa(q_t, cc, cc_t, kk, vt, gates_t, ovl, batch, seq)
        o_hg = _hgrn(h_all, hgrn_lb_logits, hgrn_out_norm[l], l, batch, seq)
        nq = NSA_HEADS * HEAD_DIM
        wo = w_out[l].astype(BF16)
        xt = _ffn(xt, ffn2_norm[l], ffn2_w_gate[l].astype(BF16), ffn2_w_up[l].astype(BF16),
                  ffn2_w_down[l].astype(BF16), mix=(o_nsa, o_hg, wo[:nq], wo[nq:]))
    return xt.reshape(batch, seq, d)
```

```python
import functools

import jax
import jax.numpy as jnp
import numpy as np
from jax import lax
from jax.experimental import pallas as pl
from jax.experimental.pallas import tpu as pltpu

NSA_HEADS = 8
NSA_GROUPS = 2
NSA_REP = NSA_HEADS // NSA_GROUPS
HEAD_DIM = 64
CMP_BLOCK = 32
CMP_STRIDE = 16
CMP_HIDDEN = 128
SEL_BLOCK = 64
SEL_TOPN = 16
WINDOW = 512
HG_HEADS = 4
HG_DK = 128
HG_DV = 128
ROPE_THETA = 10000.0
EPS = 1e-6
NEG = -1e30
LOG2E = 1.4426950408889634

LANES = 128
Q_TILE = 128
SEL_CHUNK = 512
HG_CHUNK = 64
HG_SUB = 16
HG_ROWS = 512
VMEM_LIMIT = 56 * 1024 * 1024

BF16 = jnp.bfloat16
F32 = jnp.float32


def _dot(a, b):
    return jnp.dot(a, b, preferred_element_type=F32)


def _dot_nt(a, b):
    return lax.dot_general(a, b, (((1,), (1,)), ((), ())), preferred_element_type=F32)


def _split_bf16(x, parts):
    out = []
    rem = x
    for _ in range(parts):
        p = rem.astype(BF16)
        out.append(p)
        rem = rem - p.astype(F32)
    return out


def _ffn_body(*refs, fuse_mix):
    if fuse_mix:
        (x_ref, ma_ref, mb_ref, woa_ref, wob_ref, g_ref, wg_ref, wu_ref, wd_ref,
         o_ref, xs_ref, hn_ref, acc_ref) = refs
    else:
        x_ref, g_ref, wg_ref, wu_ref, wd_ref, o_ref, xs_ref, hn_ref, acc_ref = refs
    f = pl.program_id(1)

    @pl.when(f == 0)
    def _():
        x = x_ref[...]
        if fuse_mix:
            x = x + _dot(ma_ref[...], woa_ref[...]) + _dot(mb_ref[...], wob_ref[...])
        xs_ref[...] = x
        ms = jnp.mean(x * x, axis=-1, keepdims=True)
        hn_ref[...] = (x * lax.rsqrt(ms + EPS) * g_ref[...]).astype(BF16)
        acc_ref[...] = jnp.zeros_like(acc_ref)

    hn = hn_ref[...]
    gate = _dot(hn, wg_ref[...])
    up = _dot(hn, wu_ref[...])
    act = (gate * jax.nn.sigmoid(gate) * up).astype(BF16)
    acc_ref[...] += _dot(act, wd_ref[...])

    @pl.when(f == pl.num_programs(1) - 1)
    def _():
        o_ref[...] = xs_ref[...] + 0.5 * acc_ref[...]


def _ffn(x, g, wg, wu, wd, mix=None, *, tm=1024, tf=256):
    t, d = x.shape
    ff = wg.shape[1]
    assert t % tm == 0 and ff % tf == 0
    fuse = mix is not None
    row = lambda i, f: (i, 0)
    in_specs = [pl.BlockSpec((tm, d), row)]
    args = [x]
    if fuse:
        ma, mb, woa, wob = mix
        in_specs += [pl.BlockSpec((tm, ma.shape[1]), row), pl.BlockSpec((tm, mb.shape[1]), row),
                     pl.BlockSpec(woa.shape, lambda i, f: (0, 0)),
                     pl.BlockSpec(wob.shape, lambda i, f: (0, 0))]
        args += [ma, mb, woa, wob]
    in_specs += [pl.BlockSpec((1, d), lambda i, f: (0, 0)),
                 pl.BlockSpec((d, tf), lambda i, f: (0, f)),
                 pl.BlockSpec((d, tf), lambda i, f: (0, f)),
                 pl.BlockSpec((tf, d), lambda i, f: (f, 0))]
    args += [g.reshape(1, d), wg, wu, wd]
    return pl.pallas_call(
        functools.partial(_ffn_body, fuse_mix=fuse),
        grid=(t // tm, ff // tf),
        in_specs=in_specs,
        out_specs=pl.BlockSpec((tm, d), row),
        out_shape=jax.ShapeDtypeStruct((t, d), F32),
        scratch_shapes=[pltpu.VMEM((tm, d), F32), pltpu.VMEM((tm, d), BF16), pltpu.VMEM((tm, d), F32)],
        compiler_params=pltpu.CompilerParams(
            dimension_semantics=("parallel", "arbitrary"), vmem_limit_bytes=VMEM_LIMIT),
        name="ffn_mix" if fuse else "ffn",
    )(*args)


_C_Q = 0
_C_KV = NSA_HEADS * HEAD_DIM
_C_HG = _C_KV + 6 * LANES
_C_GATE = _C_HG + 4 * HG_HEADS * HG_DK
_C_END = _C_GATE + NSA_GROUPS * LANES
GATE_ROWS = 16
Q_SCALE = HEAD_DIM ** -0.5 * LOG2E


def _inproj_body(x_ref, g_ref, w_ref, cos_ref, sin_ref, cost_ref, sint_ref, qg_ref, kg_ref,
                 q_ref, cmp_ref, k_ref, vt_ref, gate_ref, h_ref):
    x = x_ref[...]
    ms = jnp.mean(x * x, axis=-1, keepdims=True)
    hn = (x * lax.rsqrt(ms + EPS) * g_ref[...]).astype(BF16)
    cos = cos_ref[...]
    sin = sin_ref[...]
    tm = x.shape[0]
    half = HEAD_DIM // 2

    def proj(c0, width):
        return _dot(hn, w_ref[:, c0:c0 + width])

    ri = lax.broadcasted_iota(jnp.int32, (LANES, LANES), 0) // HEAD_DIM
    ci = lax.broadcasted_iota(jnp.int32, (LANES, LANES), 1) // HEAD_DIM
    head_mean = jnp.where(ri == ci, 1.0 / HEAD_DIM, 0.0).astype(BF16)
    lane = lax.broadcasted_iota(jnp.int32, (tm, LANES), 1)
    first_half = (lane % HEAD_DIM) < half

    def norm_rope(y, gain):
        sq_hi, sq_lo = _split_bf16(y * y, 2)
        msq = _dot(sq_hi, head_mean) + _dot(sq_lo, head_mean)
        yn = y * lax.rsqrt(msq + EPS) * gain
        rot = jnp.where(first_half, -pltpu.roll(yn, LANES - half, 1), pltpu.roll(yn, half, 1))
        return yn * cos + rot * sin

    def heads_t(y):
        return y.T.reshape(2, HEAD_DIM, tm)

    qgain = (qg_ref[...] * Q_SCALE)[None]
    cost = cost_ref[...][None]
    sint = sint_ref[...][None]
    for c in range(NSA_HEADS // 2):
        y3 = heads_t(proj(_C_Q + c * LANES, LANES))
        yn = y3 * lax.rsqrt(jnp.mean(y3 * y3, axis=1, keepdims=True) + EPS) * qgain
        rot = jnp.concatenate([-yn[:, half:], yn[:, :half]], axis=1)
        q_ref[0, 2 * c:2 * c + 2] = (yn * cost + rot * sint).astype(BF16)

    kc = norm_rope(proj(_C_KV, LANES), kg_ref[0:1, :])
    cmp_ref[0] = kc.astype(BF16)
    cmp_ref[1] = proj(_C_KV + LANES, LANES).astype(BF16)
    for which, col in ((0, 2), (1, 4)):
        kk = norm_rope(proj(_C_KV + col * LANES, LANES), kg_ref[which + 1:which + 2, :]).astype(BF16)
        k_ref[which, 0, 0] = kk[:, :HEAD_DIM]
        k_ref[which, 0, 1] = kk[:, HEAD_DIM:]
    for which, col in ((0, 3), (1, 5)):
        vt_ref[which, 0] = heads_t(proj(_C_KV + col * LANES, LANES)).astype(BF16)

    hw = HG_HEADS * HG_DK
    for i in range(4):
        h_ref[i] = proj(_C_HG + i * hw, hw)
    for gi in range(NSA_GROUPS):
        gt = jax.nn.sigmoid(proj(_C_GATE + gi * LANES, LANES)).T
        gate_ref[gi] = gt[:GATE_ROWS]


def _inproj(x, g, w2, cos2, sin2, cost, sint, qgt, kg2, batch, seq, *, tm=256):
    t, d = x.shape
    nst = seq // tm
    hw = HG_HEADS * HG_DK
    out_shape = (
        jax.ShapeDtypeStruct((batch, NSA_HEADS, HEAD_DIM, seq), BF16),
        jax.ShapeDtypeStruct((2, t, LANES), BF16),
        jax.ShapeDtypeStruct((2, batch, NSA_GROUPS, seq, HEAD_DIM), BF16),
        jax.ShapeDtypeStruct((2, batch, NSA_GROUPS, HEAD_DIM, seq), BF16),
        jax.ShapeDtypeStruct((NSA_GROUPS, GATE_ROWS, t), F32),
        jax.ShapeDtypeStruct((4, t, hw), F32),
    )
    out_specs = (
        pl.BlockSpec((1, NSA_HEADS, HEAD_DIM, tm), lambda i: (i // nst, 0, 0, i % nst)),
        pl.BlockSpec((2, tm, LANES), lambda i: (0, i, 0)),
        pl.BlockSpec((2, 1, NSA_GROUPS, tm, HEAD_DIM), lambda i: (0, i // nst, 0, i % nst, 0)),
        pl.BlockSpec((2, 1, NSA_GROUPS, HEAD_DIM, tm), lambda i: (0, i // nst, 0, 0, i % nst)),
        pl.BlockSpec((NSA_GROUPS, GATE_ROWS, tm), lambda i: (0, 0, i)),
        pl.BlockSpec((4, tm, hw), lambda i: (0, i, 0)),
    )
    in_specs = [
        pl.BlockSpec((tm, d), lambda i: (i, 0)),
        pl.BlockSpec((1, d), lambda i: (0, 0)),
        pl.BlockSpec(w2.shape, lambda i: (0, 0)),
        pl.BlockSpec((tm, LANES), lambda i: (i % nst, 0)),
        pl.BlockSpec((tm, LANES), lambda i: (i % nst, 0)),
        pl.BlockSpec((HEAD_DIM, tm), lambda i: (0, i % nst)),
        pl.BlockSpec((HEAD_DIM, tm), lambda i: (0, i % nst)),
        pl.BlockSpec((HEAD_DIM, tm), lambda i: (0, 0)),
        pl.BlockSpec((3, LANES), lambda i: (0, 0)),
    ]
    return pl.pallas_call(
        _inproj_body,
        grid=(t // tm,),
        in_specs=in_specs,
        out_specs=out_specs,
        out_shape=out_shape,
        compiler_params=pltpu.CompilerParams(
            dimension_semantics=("parallel",), vmem_limit_bytes=VMEM_LIMIT),
        name="inproj",
    )(x, g.reshape(1, d), w2, cos2, sin2, cost, sint, qgt, kg2)


def _compress_body(seg_ref, w1a_ref, w1b_ref, pa_ref, pb_ref, w2_ref, o_ref, ot_ref):
    segs = seg_ref[0, 0]
    w1a = w1a_ref[0]
    w1b = w1b_ref[0]
    first = _dot(segs, w1a)
    second = _dot(segs, w1b)
    pa_hi, pa_lo = _split_bf16(jnp.broadcast_to(pa_ref[0], (8, pa_ref.shape[2])), 2)
    pb_hi, pb_lo = _split_bf16(jnp.broadcast_to(pb_ref[0], (8, pb_ref.shape[2])), 2)
    cpos = _dot(pa_hi, w1a) + _dot(pa_lo, w1a) + _dot(pb_hi, w1b) + _dot(pb_lo, w1b)
    nseg = first.shape[0]
    pre = first + pltpu.roll(second, nseg - 1, 0) + cpos[0:1, :]
    hid = jax.nn.gelu(pre, approximate=True).astype(BF16)
    out = _dot(hid, w2_ref[0])
    outb = out.astype(BF16)
    o_ref[0, 0, 0] = outb[:, :HEAD_DIM]
    o_ref[0, 0, 1] = outb[:, HEAD_DIM:]
    ot_ref[0, 0] = out.T.reshape(NSA_GROUPS, HEAD_DIM, nseg).astype(BF16)


def _compress(cmp_in, w1a, w1b, pa, pb, w2, batch, seq):
    nseg = seq // CMP_STRIDE
    width = CMP_STRIDE * LANES
    segs = cmp_in.reshape(2, batch, nseg, width)
    return pl.pallas_call(
        _compress_body,
        grid=(2, batch),
        in_specs=[
            pl.BlockSpec((1, 1, nseg, width), lambda s, b: (s, b, 0, 0)),
            pl.BlockSpec((1,) + w1a.shape[1:], lambda s, b: (s, 0, 0)),
            pl.BlockSpec((1,) + w1b.shape[1:], lambda s, b: (s, 0, 0)),
            pl.BlockSpec((1, 1, width), lambda s, b: (s, 0, 0)),
            pl.BlockSpec((1, 1, width), lambda s, b: (s, 0, 0)),
            pl.BlockSpec((1,) + w2.shape[1:], lambda s, b: (s, 0, 0)),
        ],
        out_specs=(pl.BlockSpec((1, 1, NSA_GROUPS, nseg, HEAD_DIM), lambda s, b: (s, b, 0, 0, 0)),
                   pl.BlockSpec((1, 1, NSA_GROUPS, HEAD_DIM, nseg), lambda s, b: (s, b, 0, 0, 0))),
        out_shape=(jax.ShapeDtypeStruct((2, batch, NSA_GROUPS, nseg, HEAD_DIM), BF16),
                   jax.ShapeDtypeStruct((2, batch, NSA_GROUPS, HEAD_DIM, nseg), BF16)),
        compiler_params=pltpu.CompilerParams(
            dimension_semantics=("parallel", "parallel"), vmem_limit_bytes=VMEM_LIMIT),
        name="compress",
    )(segs, w1a, w1b, pa, pb, w2)


def _tile_heads(a):
    return jnp.concatenate([a] * NSA_REP, axis=1)


def _nsa_body(q_ref, kcc_ref, vcct_ref, ks_ref, kw_ref, vst_ref, vwt_ref, gate_ref, ovl_ref,
              o_ref, imp_ref, selb_ref, sa_ref, sb_ref, p_ref, m_ref, l_ref, acc_ref):
    qb = pl.program_id(2)
    start = qb * Q_TILE
    q_t = jnp.concatenate([q_ref[0, r] for r in range(NSA_REP)], axis=1)
    tq = start + lax.broadcasted_iota(jnp.int32, (1, Q_TILE), 1)

    kcc = kcc_ref[0, 0, 0]
    ncb = kcc.shape[0]
    s1 = _dot(kcc, q_t)
    s_first = _dot(ks_ref[0, 0, 0, 0:SEL_CHUNK, :], q_t)
    cend = lax.broadcasted_iota(jnp.int32, (ncb, 1), 0) * CMP_STRIDE + (CMP_BLOCK - 1)
    sm1 = s1 + _tile_heads(jnp.where(cend <= tq, 0.0, NEG))
    e1 = jnp.exp2(sm1 - jnp.max(sm1, axis=0, keepdims=True))
    den1 = jnp.sum(e1, axis=0, keepdims=True)
    p1 = e1 * jnp.where(_tile_heads(tq >= CMP_BLOCK - 1), 1.0 / den1, 0.0)
    o_c = _dot(vcct_ref[0, 0, 0], p1.astype(BF16))

    p1sum = p1[:, 0:Q_TILE]
    for r in range(1, NSA_REP):
        p1sum = p1sum + p1[:, r * Q_TILE:(r + 1) * Q_TILE]
    ps_hi, ps_lo = _split_bf16(p1sum, 2)
    ovl = ovl_ref[...]
    imp = _dot(ovl, ps_hi) + _dot(ovl, ps_lo)
    nsb = imp.shape[0]
    jb = lax.broadcasted_iota(jnp.int32, (nsb, Q_TILE), 0)
    cur = tq // SEL_BLOCK
    forced = (jb == 0) | (jb == cur) | (jb == cur - 1)
    valid = jb * SEL_BLOCK <= tq
    impm = jnp.where(valid, jnp.where(forced, jnp.inf, imp), -jnp.inf)
    imp_ref[...] = impm
    sub = lax.broadcasted_iota(jnp.int32, (8, Q_TILE), 0)
    ranks = [jnp.zeros((8, Q_TILE), F32) for _ in range(nsb // 8)]
    for i in range(nsb):
        xi = jnp.broadcast_to(imp_ref[i:i + 1, :], (8, Q_TILE))
        for v in range(nsb // 8):
            xv = impm[8 * v:8 * v + 8, :]
            ge = jnp.where(xi >= xv, 1.0, 0.0)
            gt = jnp.where(xi > xv, 1.0, 0.0)
            if 8 * v > i:
                inc = ge
            elif 8 * v + 7 < i:
                inc = gt
            else:
                inc = jnp.where(sub + 8 * v > i, ge, gt)
            ranks[v] = ranks[v] + inc
    rank = jnp.concatenate(ranks, axis=0)
    selb_ref[...] = jnp.where((rank < SEL_TOPN) & valid, 0.0, NEG)

    blocks_per_chunk = SEL_CHUNK // SEL_BLOCK
    kp_rel = lax.broadcasted_iota(jnp.int32, (SEL_CHUNK, 1), 0)
    last_chunk = ks_ref.shape[3] // SEL_CHUNK - 1

    def sel_step(c, s_cur_ref, s_next_ref):
        v0 = pl.multiple_of(jnp.maximum(c - 1, 0) * SEL_CHUNK, SEL_CHUNK)
        pv = _dot(vst_ref[0, 0, 0, :, pl.ds(v0, SEL_CHUNK)], p_ref[...])
        n0 = pl.multiple_of(jnp.minimum(c + 1, last_chunk) * SEL_CHUNK, SEL_CHUNK)
        s_next_ref[...] = _dot(ks_ref[0, 0, 0, pl.ds(n0, SEL_CHUNK), :], q_t)
        rows = [jnp.broadcast_to(selb_ref[pl.ds(c * blocks_per_chunk + j, 1), :], (SEL_BLOCK, Q_TILE))
                for j in range(blocks_per_chunk)]
        bias = jnp.where(c * SEL_CHUNK + kp_rel <= tq, jnp.concatenate(rows, axis=0), NEG)
        sm = s_cur_ref[...] + _tile_heads(bias)
        m = m_ref[...]
        m_new = jnp.maximum(m, jnp.max(sm, axis=0, keepdims=True))
        p = jnp.exp2(sm - m_new)
        alpha = jnp.exp2(m - m_new)
        l_ref[...] = alpha * l_ref[...] + jnp.sum(p, axis=0, keepdims=True)
        acc_ref[...] = alpha * (acc_ref[...] + pv)
        m_ref[...] = m_new
        p_ref[...] = p.astype(BF16)

    n_chunks = (start + Q_TILE + SEL_CHUNK - 1) // SEL_CHUNK
    sa_ref[...] = s_first
    p_ref[...] = jnp.zeros_like(p_ref)
    m_ref[...] = jnp.full(m_ref.shape, NEG, F32)
    l_ref[...] = jnp.zeros_like(l_ref)
    acc_ref[...] = jnp.zeros_like(acc_ref)

    def sel_pair(i, carry):
        sel_step(2 * i, sa_ref, sb_ref)

        @pl.when(2 * i + 1 < n_chunks)
        def _():
            sel_step(2 * i + 1, sb_ref, sa_ref)
        return carry

    lax.fori_loop(0, (n_chunks + 1) // 2, sel_pair, 0)

    wkeys = WINDOW + Q_TILE
    w0 = pl.multiple_of(jnp.maximum(start - WINDOW, 0), Q_TILE)
    s3 = _dot(kw_ref[0, 0, 0, pl.ds(w0, wkeys), :], q_t)
    vl = pl.multiple_of((n_chunks - 1) * SEL_CHUNK, SEL_CHUNK)
    acc_s = acc_ref[...] + _dot(vst_ref[0, 0, 0, :, pl.ds(vl, SEL_CHUNK)], p_ref[...])
    l_s = l_ref[...]
    dist = tq - (w0 + lax.broadcasted_iota(jnp.int32, (wkeys, 1), 0))
    sm3 = s3 + _tile_heads(jnp.where((dist >= 0) & (dist < WINDOW), 0.0, NEG))
    e3 = jnp.exp2(sm3 - jnp.max(sm3, axis=0, keepdims=True))
    den3 = jnp.sum(e3, axis=0, keepdims=True)
    acc_w = _dot(vwt_ref[0, 0, 0, :, pl.ds(w0, wkeys)], e3.astype(BF16))

    gates = gate_ref[0]
    outs = []
    for r in range(NSA_REP):
        sl = slice(r * Q_TILE, (r + 1) * Q_TILE)
        g_c = gates[3 * r:3 * r + 1, :]
        g_s = gates[3 * r + 1:3 * r + 2, :] / l_s[:, sl]
        g_w = gates[3 * r + 2:3 * r + 3, :] / den3[:, sl]
        outs.append(g_c * o_c[:, sl] + g_s * acc_s[:, sl] + g_w * acc_w[:, sl])
    o_ref[...] = jnp.concatenate(outs, axis=0).T.astype(o_ref.dtype)


def _nsa(q_t, cc, cc_t, kk, vt, gates_t, ovl, batch, seq):
    nqb = seq // Q_TILE
    ncb_pad = cc.shape[3]
    nsb = seq // SEL_BLOCK
    rows = NSA_REP * Q_TILE
    kspec = lambda which: pl.BlockSpec((1, 1, 1, seq, HEAD_DIM), lambda b, g, i: (which, b, g, 0, 0))
    vspec = lambda which: pl.BlockSpec((1, 1, 1, HEAD_DIM, seq), lambda b, g, i: (which, b, g, 0, 0))
    return pl.pallas_call(
        _nsa_body,
        grid=(batch, NSA_GROUPS, nqb),
        in_specs=[
            pl.BlockSpec((1, NSA_REP, HEAD_DIM, Q_TILE), lambda b, g, i: (b, g, 0, i)),
            pl.BlockSpec((1, 1, 1, ncb_pad, HEAD_DIM), lambda b, g, i: (0, b, g, 0, 0)),
            pl.BlockSpec((1, 1, 1, HEAD_DIM, ncb_pad), lambda b, g, i: (1, b, g, 0, 0)),
            kspec(0), kspec(1), vspec(0), vspec(1),
            pl.BlockSpec((1, GATE_ROWS, Q_TILE), lambda b, g, i: (g, 0, b * nqb + i)),
            pl.BlockSpec(ovl.shape, lambda b, g, i: (0, 0)),
        ],
        out_specs=pl.BlockSpec((Q_TILE, NSA_REP * HEAD_DIM), lambda b, g, i: (b * nqb + i, g)),
        out_shape=jax.ShapeDtypeStruct((batch * seq, NSA_HEADS * HEAD_DIM), BF16),
        scratch_shapes=[pltpu.VMEM((nsb, Q_TILE), F32), pltpu.VMEM((nsb, Q_TILE), F32),
                        pltpu.VMEM((SEL_CHUNK, rows), F32), pltpu.VMEM((SEL_CHUNK, rows), F32),
                        pltpu.VMEM((SEL_CHUNK, rows), BF16),
                        pltpu.VMEM((1, rows), F32), pltpu.VMEM((1, rows), F32),
                        pltpu.VMEM((HEAD_DIM, rows), F32)],
        compiler_params=pltpu.CompilerParams(
            dimension_semantics=("parallel", "parallel", "arbitrary"), vmem_limit_bytes=VMEM_LIMIT),
        name="nsa",
    )(q_t, cc, cc_t, kk, kk, vt, vt, gates_t, ovl)


def _hgrn_body(hq_ref, hf_ref, hi_ref, hg_ref, lbl_ref, gain_ref, o_ref,
               state_ref, b_ref, k_ref, v_ref, q_ref, *, layer):
    @pl.when(pl.program_id(1) == 0)
    def _():
        state_ref[...] = jnp.zeros_like(state_ref)

    logits = lbl_ref[...]
    ex = jnp.exp(logits - jnp.max(logits, axis=0, keepdims=True))
    sm = ex / jnp.sum(ex, axis=0, keepdims=True)
    lb_all = jnp.zeros((1, HG_HEADS * HG_DK), F32)
    for i in range(1, layer + 1):
        lb_all = lb_all + sm[i:i + 1, :]

    ch, blk, half = HG_CHUNK, HG_SUB, HG_SUB // 2
    nblk = ch // blk
    ri = lax.broadcasted_iota(jnp.int32, (ch, ch), 0)
    ci = lax.broadcasted_iota(jnp.int32, (ch, ch), 1)
    tril = jnp.where(ri >= ci, 1.0, 0.0).astype(BF16)
    blk_row = lax.broadcasted_iota(jnp.int32, (blk, HG_DK), 0)
    low_rows = blk_row >= half
    rr = lax.broadcasted_iota(jnp.int32, (blk, blk), 0)
    cc = lax.broadcasted_iota(jnp.int32, (blk, blk), 1)
    cross = (rr >= half) & (cc < half)
    half_row = lax.broadcasted_iota(jnp.int32, (half, HG_DK), 0)
    gain = gain_ref[...]
    heads = range(HG_HEADS)

    def chunk(j, carry):
        r0 = pl.multiple_of(j * ch, ch)
        for h in heads:
            cols = slice(h * HG_DK, (h + 1) * HG_DK)
            lb = lb_all[:, cols]
            hq = hq_ref[0, pl.ds(r0, ch), cols]
            hf = hf_ref[0, pl.ds(r0, ch), cols]
            f = lb + (1.0 - lb) * jax.nn.sigmoid(hf)
            lf = _split_bf16(jnp.log(jnp.maximum(f, 1e-30)), 3)
            b_ref[h] = _dot(tril, lf[0]) + _dot(tril, lf[1]) + _dot(tril, lf[2])
            k_ref[h] = 1.0 - f
            v_ref[h] = hi_ref[0, pl.ds(r0, ch), cols]
            q_ref[h] = hq * jax.nn.sigmoid(hq)

        o_blocks, tables = [], []
        for h in heads:
            inter = _dot_nt((q_ref[h] * jnp.exp(b_ref[h])).astype(BF16), state_ref[h].astype(BF16))
            o_blocks.append([inter[ib * blk:(ib + 1) * blk, :] for ib in range(nblk)])
            tables.append([])
            for ib in range(nblk):
                i0 = ib * blk
                qi = q_ref[h, i0:i0 + blk, :]
                bi = b_ref[h, i0:i0 + blk, :]
                a = None
                if ib > 0:
                    ref_b = b_ref[h, i0 - 1:i0, :]
                    qd = (qi * jnp.exp(bi - ref_b)).astype(BF16)
                    kd = (k_ref[h, 0:i0, :] * jnp.exp(ref_b - b_ref[h, 0:i0, :])).astype(BF16)
                    a = _dot_nt(qd, kd).astype(BF16)
                mid_b = b_ref[h, i0 + half - 1:i0 + half, :]
                qd2 = (qi * jnp.exp(jnp.where(low_rows, bi - mid_b, 0.0))).astype(BF16)
                kd2 = (k_ref[h, i0:i0 + blk, :] * jnp.exp(jnp.where(low_rows, 0.0, mid_b - bi))).astype(BF16)
                a2 = jnp.where(cross, _dot_nt(qd2, kd2), 0.0).astype(BF16)
                tables[h].append((a, a2))
        for h in heads:
            bl = b_ref[h, ch - 1:ch, :]
            kd_end = (k_ref[h] * jnp.exp(bl - b_ref[h])).astype(BF16)
            vt = v_ref[h].T.astype(BF16)
            state_ref[h] = state_ref[h] * jnp.exp(bl) + _dot(vt, kd_end)

        for h in heads:
            for ib in range(nblk):
                i0 = ib * blk
                a, a2 = tables[h][ib]
                o_i = o_blocks[h][ib] + _dot(a2, v_ref[h, i0:i0 + blk, :].astype(BF16))
                if a is not None:
                    o_i = o_i + _dot(a, v_ref[h, 0:i0, :].astype(BF16))
                o_blocks[h][ib] = o_i

        nhalf = ch // half
        diag = [[jnp.zeros((half, HG_DV), F32) for _ in range(nhalf)] for _ in heads]
        for s in range(half):
            for h in heads:
                for ih in range(nhalf):
                    g0 = ih * half
                    bs = b_ref[h, g0 + s:g0 + s + 1, :]
                    ks = k_ref[h, g0 + s:g0 + s + 1, :]
                    vs = v_ref[h, g0 + s:g0 + s + 1, :]
                    bi = b_ref[h, g0:g0 + half, :]
                    qi = q_ref[h, g0:g0 + half, :]
                    dec = jnp.exp(jnp.where(half_row >= s, bi - bs, NEG))
                    a_s = jnp.sum(qi * dec * ks, axis=-1, keepdims=True)
                    diag[h][ih] = diag[h][ih] + a_s * vs

        for h in heads:
            cols = slice(h * HG_DK, (h + 1) * HG_DK)
            o = jnp.concatenate(o_blocks[h], axis=0) + jnp.concatenate(diag[h], axis=0)
            o = o * lax.rsqrt(jnp.mean(o * o, axis=-1, keepdims=True) + EPS) * gain
            hg = hg_ref[0, pl.ds(r0, ch), cols]
            o_ref[pl.ds(r0, ch), cols] = (o * (hg * jax.nn.sigmoid(hg))).astype(o_ref.dtype)
        return carry

    lax.fori_loop(0, hq_ref.shape[1] // ch, chunk, 0)


def _hgrn(h_all, lb_logits, out_gain, layer, batch, seq):
    t = h_all.shape[1]
    nblk = seq // HG_ROWS
    nl = lb_logits.shape[0]
    hw = HG_HEADS * HG_DK
    hspec = lambda which: pl.BlockSpec((1, HG_ROWS, hw), lambda b, c: (which, b * nblk + c, 0))
    return pl.pallas_call(
        functools.partial(_hgrn_body, layer=layer),
        grid=(batch, nblk),
        in_specs=[hspec(0), hspec(1), hspec(2), hspec(3),
                  pl.BlockSpec((nl, hw), lambda b, c: (0, 0)),
                  pl.BlockSpec((1, HG_DV), lambda b, c: (0, 0))],
        out_specs=pl.BlockSpec((HG_ROWS, HG_HEADS * HG_DV), lambda b, c: (b * nblk + c, 0)),
        out_shape=jax.ShapeDtypeStruct((t, HG_HEADS * HG_DV), BF16),
        scratch_shapes=[pltpu.VMEM((HG_HEADS, HG_DV, HG_DK), F32)]
        + [pltpu.VMEM((HG_HEADS, HG_CHUNK, HG_DK), F32)] * 4,
        compiler_params=pltpu.CompilerParams(
            dimension_semantics=("parallel", "arbitrary"), vmem_limit_bytes=VMEM_LIMIT),
        name="hgrn",
    )(h_all, h_all, h_all, h_all, lb_logits, out_gain.reshape(1, HG_DV))


def _relayout_w_in(w):
    d = w.shape[0]
    nq = NSA_HEADS * HEAD_DIM
    nkv = 6 * NSA_GROUPS * HEAD_DIM
    ng = 3 * NSA_HEADS
    q_kv = w[:, :nq + nkv]
    gts = w[:, nq + nkv:nq + nkv + ng]
    hg = w[:, nq + nkv + ng:]
    per = 3 * NSA_REP
    gate_blocks = [jnp.pad(gts[:, gi * per:(gi + 1) * per], ((0, 0), (0, LANES - per)))
                   for gi in range(NSA_GROUPS)]
    out = jnp.concatenate([q_kv, hg] + gate_blocks, axis=1).astype(BF16)
    assert out.shape == (d, _C_END)
    return out


def _compress_weights(pos, w1, w2):
    eye = jnp.eye(NSA_GROUPS, dtype=F32)
    e = w1.shape[-1]

    def big(w1_half):
        y = jnp.einsum("sldh,gk->slgdkh", w1_half, eye)
        return y.reshape(2, CMP_STRIDE * NSA_GROUPS * HEAD_DIM, NSA_GROUPS * e).astype(BF16)

    def pos_rows(p_half):
        y = jnp.broadcast_to(p_half[:, :, None, :], (2, CMP_STRIDE, NSA_GROUPS, HEAD_DIM))
        return y.reshape(2, 1, CMP_STRIDE * NSA_GROUPS * HEAD_DIM)

    w2b = jnp.einsum("shd,gk->sghkd", w2, eye).reshape(2, NSA_GROUPS * e, NSA_GROUPS * HEAD_DIM)
    return (big(w1[:, :CMP_STRIDE]), big(w1[:, CMP_STRIDE:]),
            pos_rows(pos[:, :CMP_STRIDE]), pos_rows(pos[:, CMP_STRIDE:]), w2b.astype(BF16))


def kernel(x, ffn1_norm, ffn1_w_gate, ffn1_w_up, ffn1_w_down, mix_norm, w_in, q_norm, k_norm,
           cmp_pos, cmp_w1, cmp_w2, hgrn_lb_logits, hgrn_out_norm, w_out,
           ffn2_norm, ffn2_w_gate, ffn2_w_up, ffn2_w_down):
    batch, seq, d = x.shape
    depth = w_in.shape[0]
    assert CMP_BLOCK == 2 * CMP_STRIDE and seq % SEL_CHUNK == 0 and seq >= WINDOW + Q_TILE
    assert seq % HG_ROWS == 0 and SEL_CHUNK % SEL_BLOCK == 0
    t = batch * seq
    tm_proj = 256

    inv = 1.0 / (ROPE_THETA ** (jnp.arange(0, HEAD_DIM, 2, dtype=F32) / HEAD_DIM))
    ang = jnp.arange(seq, dtype=F32)[:, None] * inv[None, :]
    ang = jnp.concatenate([ang, ang], axis=-1)
    cos1, sin1 = jnp.cos(ang), jnp.sin(ang)
    cos2, sin2 = jnp.tile(cos1, (1, 2)), jnp.tile(sin1, (1, 2))
    cost, sint = cos1.T, sin1.T

    nseg = seq // CMP_STRIDE
    nsb = seq // SEL_BLOCK
    ci = np.arange(nseg)[None, :]
    sj = np.arange(nsb)[:, None]
    ovl = ((ci * CMP_STRIDE <= sj * SEL_BLOCK + SEL_BLOCK - 1)
           & (ci * CMP_STRIDE + CMP_BLOCK - 1 >= sj * SEL_BLOCK)
           & (ci < (seq - CMP_BLOCK) // CMP_STRIDE + 1))
    ovl = jnp.asarray(ovl, dtype=BF16)

    xt = x.reshape(t, d)
    for l in range(depth):
        xt = _ffn(xt, ffn1_norm[l], ffn1_w_gate[l].astype(BF16), ffn1_w_up[l].astype(BF16),
                  ffn1_w_down[l].astype(BF16))
        qgt = jnp.broadcast_to(q_norm[l][:, None], (HEAD_DIM, tm_proj))
        kg2 = jnp.tile(k_norm[l], (1, 2))
        q_t, cmp_in, kk, vt, gates_t, h_all = _inproj(
            xt, mix_norm[l], _relayout_w_in(w_in[l]), cos2, sin2, cost, sint, qgt, kg2,
            batch, seq, tm=tm_proj)
        w1a, w1b, pa, pb, w2b = _compress_weights(cmp_pos[l], cmp_w1[l], cmp_w2[l])
        cc, cc_t = _compress(cmp_in, w1a, w1b, pa, pb, w2b, batch, seq)
        o_nsa = _nsa(q_t, cc, cc_t, kk, vt, gates_t, ovl, batch, seq)
        o_hg = _hgrn(h_all, hgrn_lb_logits, hgrn_out_norm[l], l, batch, seq)
        nq = NSA_HEADS * HEAD_DIM
        wo = w_out[l].astype(BF16)
        xt = _ffn(xt, ffn2_norm[l], ffn2_w_gate[l].astype(BF16), ffn2_w_up[l].astype(BF16),
                  ffn2_w_down[l].astype(BF16), mix=(o_nsa, o_hg, wo[:nq], wo[nq:]))
    return xt.reshape(batch, seq, d)
```

```python
import functools

import jax
import jax.numpy as jnp
import numpy as np
from jax import lax
from jax.experimental import pallas as pl
from jax.experimental.pallas import tpu as pltpu

NSA_HEADS = 8
NSA_GROUPS = 2
NSA_REP = NSA_HEADS // NSA_GROUPS
HEAD_DIM = 64
CMP_BLOCK = 32
CMP_STRIDE = 16
CMP_HIDDEN = 128
SEL_BLOCK = 64
SEL_TOPN = 16
WINDOW = 512
HG_HEADS = 4
HG_DK = 128
HG_DV = 128
ROPE_THETA = 10000.0
EPS = 1e-6
NEG = -1e30
LOG2E = 1.4426950408889634

LANES = 128
Q_TILE = 128
SEL_CHUNK = 512
HG_CHUNK = 64
HG_SUB = 16
HG_ROWS = 512
VMEM_LIMIT = 56 * 1024 * 1024

BF16 = jnp.bfloat16
F32 = jnp.float32


def _dot(a, b):
    return jnp.dot(a, b, preferred_element_type=F32)


def _dot_nt(a, b):
    return lax.dot_general(a, b, (((1,), (1,)), ((), ())), preferred_element_type=F32)


def _split_bf16(x, parts):
    out = []
    rem = x
    for _ in range(parts):
        p = rem.astype(BF16)
        out.append(p)
        rem = rem - p.astype(F32)
    return out


def _ffn_body(*refs, fuse_mix):
    if fuse_mix:
        (x_ref, ma_ref, mb_ref, woa_ref, wob_ref, g_ref, wg_ref, wu_ref, wd_ref,
         o_ref, xs_ref, hn_ref, acc_ref) = refs
    else:
        x_ref, g_ref, wg_ref, wu_ref, wd_ref, o_ref, xs_ref, hn_ref, acc_ref = refs
    f = pl.program_id(1)

    @pl.when(f == 0)
    def _():
        x = x_ref[...]
        if fuse_mix:
            x = x + _dot(ma_ref[...], woa_ref[...].astype(BF16)) + _dot(mb_ref[...], wob_ref[...].astype(BF16))
        xs_ref[...] = x
        ms = jnp.mean(x * x, axis=-1, keepdims=True)
        hn_ref[...] = (x * lax.rsqrt(ms + EPS) * g_ref[...]).astype(BF16)
        acc_ref[...] = jnp.zeros_like(acc_ref)

    hn = hn_ref[...]
    gate = _dot(hn, wg_ref[...].astype(BF16))
    up = _dot(hn, wu_ref[...].astype(BF16))
    act = (gate * jax.nn.sigmoid(gate) * up).astype(BF16)
    acc_ref[...] += _dot(act, wd_ref[...].astype(BF16))

    @pl.when(f == pl.num_programs(1) - 1)
    def _():
        o_ref[...] = xs_ref[...] + 0.5 * acc_ref[...]


def _ffn(x, g, wg, wu, wd, mix=None, *, tm=1024, tf=256):
    t, d = x.shape
    ff = wg.shape[1]
    assert t % tm == 0 and ff % tf == 0
    fuse = mix is not None
    row = lambda i, f: (i, 0)
    in_specs = [pl.BlockSpec((tm, d), row)]
    args = [x]
    if fuse:
        ma, mb, woa, wob = mix
        in_specs += [pl.BlockSpec((tm, ma.shape[1]), row), pl.BlockSpec((tm, mb.shape[1]), row),
                     pl.BlockSpec(woa.shape, lambda i, f: (0, 0)),
                     pl.BlockSpec(wob.shape, lambda i, f: (0, 0))]
        args += [ma, mb, woa, wob]
    in_specs += [pl.BlockSpec((1, d), lambda i, f: (0, 0)),
                 pl.BlockSpec((d, tf), lambda i, f: (0, f)),
                 pl.BlockSpec((d, tf), lambda i, f: (0, f)),
                 pl.BlockSpec((tf, d), lambda i, f: (f, 0))]
    args += [g.reshape(1, d), wg, wu, wd]
    return pl.pallas_call(
        functools.partial(_ffn_body, fuse_mix=fuse),
        grid=(t // tm, ff // tf),
        in_specs=in_specs,
        out_specs=pl.BlockSpec((tm, d), row),
        out_shape=jax.ShapeDtypeStruct((t, d), F32),
        scratch_shapes=[pltpu.VMEM((tm, d), F32), pltpu.VMEM((tm, d), BF16), pltpu.VMEM((tm, d), F32)],
        compiler_params=pltpu.CompilerParams(
            dimension_semantics=("parallel", "arbitrary"), vmem_limit_bytes=VMEM_LIMIT),
        name="ffn_mix" if fuse else "ffn",
    )(*args)


_C_Q = 0
_C_KV = NSA_HEADS * HEAD_DIM
_C_HG = _C_KV + 6 * LANES
_C_GATE = _C_HG + 4 * HG_HEADS * HG_DK
_C_END = _C_GATE + NSA_GROUPS * LANES
GATE_ROWS = 16
Q_SCALE = HEAD_DIM ** -0.5 * LOG2E


def _inproj_body(x_ref, g_ref, w_ref, cos_ref, sin_ref, cost_ref, sint_ref, qg_ref, kg_ref,
                 q_ref, cmp_ref, ksel_ref, kwin_ref, vt_ref, gate_ref, h_ref, *, tiles_per_seq):
    x = x_ref[...]
    ms = jnp.mean(x * x, axis=-1, keepdims=True)
    hn = (x * lax.rsqrt(ms + EPS) * g_ref[...]).astype(BF16)
    cos = cos_ref[...]
    sin = sin_ref[...]
    tm = x.shape[0]
    half = HEAD_DIM // 2

    def proj(c0, width):
        return _dot(hn, w_ref[:, c0:c0 + width])

    ri = lax.broadcasted_iota(jnp.int32, (LANES, LANES), 0) // HEAD_DIM
    ci = lax.broadcasted_iota(jnp.int32, (LANES, LANES), 1) // HEAD_DIM
    head_mean = jnp.where(ri == ci, 1.0 / HEAD_DIM, 0.0).astype(BF16)
    lane = lax.broadcasted_iota(jnp.int32, (tm, LANES), 1)
    first_half = (lane % HEAD_DIM) < half

    def norm_rope(y, gain):
        sq_hi, sq_lo = _split_bf16(y * y, 2)
        msq = _dot(sq_hi, head_mean) + _dot(sq_lo, head_mean)
        yn = y * lax.rsqrt(msq + EPS) * gain
        rot = jnp.where(first_half, -pltpu.roll(yn, LANES - half, 1), pltpu.roll(yn, half, 1))
        return yn * cos + rot * sin

    def heads_t(y):
        return y.T.reshape(2, HEAD_DIM, tm)

    qgain = (qg_ref[...] * Q_SCALE)[None]
    cost = cost_ref[...][None]
    sint = sint_ref[...][None]
    for c in range(NSA_HEADS // 2):
        y3 = heads_t(proj(_C_Q + c * LANES, LANES))
        yn = y3 * lax.rsqrt(jnp.mean(y3 * y3, axis=1, keepdims=True) + EPS) * qgain
        rot = jnp.concatenate([-yn[:, half:], yn[:, :half]], axis=1)
        q_ref[0, 2 * c:2 * c + 2] = (yn * cost + rot * sint).astype(BF16)

    kc = norm_rope(proj(_C_KV, LANES), kg_ref[0:1, :])
    cmp_ref[0] = kc.astype(BF16)
    cmp_ref[1] = proj(_C_KV + LANES, LANES).astype(BF16)
    ksel = norm_rope(proj(_C_KV + 2 * LANES, LANES), kg_ref[1:2, :])
    pos = (pl.program_id(0) % tiles_per_seq) * tm + lax.broadcasted_iota(jnp.int32, (tm, HEAD_DIM), 0)
    blk_in_chunk = (pos // SEL_BLOCK) % (SEL_CHUNK // SEL_BLOCK)
    onehot = jnp.where(lax.broadcasted_iota(jnp.int32, (tm, HEAD_DIM), 1) == blk_in_chunk, 1.0, 0.0)
    for gi in range(NSA_GROUPS):
        ksel_ref[0, gi] = jnp.concatenate(
            [ksel[:, gi * HEAD_DIM:(gi + 1) * HEAD_DIM], onehot], axis=1).astype(BF16)
    kwin = norm_rope(proj(_C_KV + 4 * LANES, LANES), kg_ref[2:3, :]).astype(BF16)
    kwin_ref[0, 0] = kwin[:, :HEAD_DIM]
    kwin_ref[0, 1] = kwin[:, HEAD_DIM:]
    for which, col in ((0, 3), (1, 5)):
        vt_ref[which, 0] = heads_t(proj(_C_KV + col * LANES, LANES)).astype(BF16)

    hw = HG_HEADS * HG_DK
    for i in range(4):
        h_ref[i] = proj(_C_HG + i * hw, hw)
    for gi in range(NSA_GROUPS):
        gt = jax.nn.sigmoid(proj(_C_GATE + gi * LANES, LANES)).T
        gate_ref[gi] = gt[:GATE_ROWS]


def _inproj(x, g, w2, cos2, sin2, cost, sint, qgt, kg2, batch, seq, *, tm=256):
    t, d = x.shape
    nst = seq // tm
    hw = HG_HEADS * HG_DK
    out_shape = (
        jax.ShapeDtypeStruct((batch, NSA_HEADS, HEAD_DIM, seq), BF16),
        jax.ShapeDtypeStruct((2, t, LANES), BF16),
        jax.ShapeDtypeStruct((batch, NSA_GROUPS, seq, LANES), BF16),
        jax.ShapeDtypeStruct((batch, NSA_GROUPS, seq, HEAD_DIM), BF16),
        jax.ShapeDtypeStruct((2, batch, NSA_GROUPS, HEAD_DIM, seq), BF16),
        jax.ShapeDtypeStruct((NSA_GROUPS, GATE_ROWS, t), F32),
        jax.ShapeDtypeStruct((4, t, hw), F32),
    )
    out_specs = (
        pl.BlockSpec((1, NSA_HEADS, HEAD_DIM, tm), lambda i: (i // nst, 0, 0, i % nst)),
        pl.BlockSpec((2, tm, LANES), lambda i: (0, i, 0)),
        pl.BlockSpec((1, NSA_GROUPS, tm, LANES), lambda i: (i // nst, 0, i % nst, 0)),
        pl.BlockSpec((1, NSA_GROUPS, tm, HEAD_DIM), lambda i: (i // nst, 0, i % nst, 0)),
        pl.BlockSpec((2, 1, NSA_GROUPS, HEAD_DIM, tm), lambda i: (0, i // nst, 0, 0, i % nst)),
        pl.BlockSpec((NSA_GROUPS, GATE_ROWS, tm), lambda i: (0, 0, i)),
        pl.BlockSpec((4, tm, hw), lambda i: (0, i, 0)),
    )
    in_specs = [
        pl.BlockSpec((tm, d), lambda i: (i, 0)),
        pl.BlockSpec((1, d), lambda i: (0, 0)),
        pl.BlockSpec(w2.shape, lambda i: (0, 0)),
        pl.BlockSpec((tm, LANES), lambda i: (i % nst, 0)),
        pl.BlockSpec((tm, LANES), lambda i: (i % nst, 0)),
        pl.BlockSpec((HEAD_DIM, tm), lambda i: (0, i % nst)),
        pl.BlockSpec((HEAD_DIM, tm), lambda i: (0, i % nst)),
        pl.BlockSpec((HEAD_DIM, tm), lambda i: (0, 0)),
        pl.BlockSpec((3, LANES), lambda i: (0, 0)),
    ]
    return pl.pallas_call(
        functools.partial(_inproj_body, tiles_per_seq=nst),
        grid=(t // tm,),
        in_specs=in_specs,
        out_specs=out_specs,
        out_shape=out_shape,
        compiler_params=pltpu.CompilerParams(
            dimension_semantics=("parallel",), vmem_limit_bytes=VMEM_LIMIT),
        name="inproj",
    )(x, g.reshape(1, d), w2, cos2, sin2, cost, sint, qgt, kg2)


def _compress_body(seg_ref, w1a_ref, w1b_ref, pa_ref, pb_ref, w2_ref, o_ref, ot_ref):
    segs = seg_ref[0, 0]
    w1a = w1a_ref[0]
    w1b = w1b_ref[0]
    first = _dot(segs, w1a)
    second = _dot(segs, w1b)
    pa_hi, pa_lo = _split_bf16(jnp.broadcast_to(pa_ref[0], (8, pa_ref.shape[2])), 2)
    pb_hi, pb_lo = _split_bf16(jnp.broadcast_to(pb_ref[0], (8, pb_ref.shape[2])), 2)
    cpos = _dot(pa_hi, w1a) + _dot(pa_lo, w1a) + _dot(pb_hi, w1b) + _dot(pb_lo, w1b)
    nseg = first.shape[0]
    pre = first + pltpu.roll(second, nseg - 1, 0) + cpos[0:1, :]
    hid = jax.nn.gelu(pre, approximate=True).astype(BF16)
    out = _dot(hid, w2_ref[0])
    outb = out.astype(BF16)
    o_ref[0, 0, 0] = outb[:, :HEAD_DIM]
    o_ref[0, 0, 1] = outb[:, HEAD_DIM:]
    ot_ref[0, 0] = out.T.reshape(NSA_GROUPS, HEAD_DIM, nseg).astype(BF16)


def _compress(cmp_in, w1a, w1b, pa, pb, w2, batch, seq):
    nseg = seq // CMP_STRIDE
    width = CMP_STRIDE * LANES
    segs = cmp_in.reshape(2, batch, nseg, width)
    return pl.pallas_call(
        _compress_body,
        grid=(2, batch),
        in_specs=[
            pl.BlockSpec((1, 1, nseg, width), lambda s, b: (s, b, 0, 0)),
            pl.BlockSpec((1,) + w1a.shape[1:], lambda s, b: (s, 0, 0)),
            pl.BlockSpec((1,) + w1b.shape[1:], lambda s, b: (s, 0, 0)),
            pl.BlockSpec((1, 1, width), lambda s, b: (s, 0, 0)),
            pl.BlockSpec((1, 1, width), lambda s, b: (s, 0, 0)),
            pl.BlockSpec((1,) + w2.shape[1:], lambda s, b: (s, 0, 0)),
        ],
        out_specs=(pl.BlockSpec((1, 1, NSA_GROUPS, nseg, HEAD_DIM), lambda s, b: (s, b, 0, 0, 0)),
                   pl.BlockSpec((1, 1, NSA_GROUPS, HEAD_DIM, nseg), lambda s, b: (s, b, 0, 0, 0))),
        out_shape=(jax.ShapeDtypeStruct((2, batch, NSA_GROUPS, nseg, HEAD_DIM), BF16),
                   jax.ShapeDtypeStruct((2, batch, NSA_GROUPS, HEAD_DIM, nseg), BF16)),
        compiler_params=pltpu.CompilerParams(
            dimension_semantics=("parallel", "parallel"), vmem_limit_bytes=VMEM_LIMIT),
        name="compress",
    )(segs, w1a, w1b, pa, pb, w2)


def _tile_heads(a):
    return jnp.concatenate([a] * NSA_REP, axis=1)


def _nsa_body(q_ref, kcc_ref, vcct_ref, ks_ref, kw_ref, vst_ref, vwt_ref, gate_ref, ovl_ref,
              o_ref, imp_ref, selb_ref, qa_ref, sa_ref, sb_ref, p_ref, m_ref, l_ref, acc_ref,
              oc_ref, p3_ref, den3_ref):
    qb = pl.program_id(2)
    start = qb * Q_TILE
    q_t = jnp.concatenate([q_ref[0, r] for r in range(NSA_REP)], axis=1)
    tq = start + lax.broadcasted_iota(jnp.int32, (1, Q_TILE), 1)

    kcc = kcc_ref[0, 0, 0]
    ncb = kcc.shape[0]
    s1 = _dot(kcc, q_t)
    wkeys = WINDOW + Q_TILE
    w0 = pl.multiple_of(jnp.maximum(start - WINDOW, 0), Q_TILE)
    s3 = _dot(kw_ref[0, 0, 0, pl.ds(w0, wkeys), :], q_t)
    cend = lax.broadcasted_iota(jnp.int32, (ncb, 1), 0) * CMP_STRIDE + (CMP_BLOCK - 1)
    sm1 = s1 + _tile_heads(jnp.where(cend <= tq, 0.0, NEG))
    e1 = jnp.exp2(sm1 - jnp.max(sm1, axis=0, keepdims=True))
    den1 = jnp.sum(e1, axis=0, keepdims=True)
    p1 = e1 * jnp.where(_tile_heads(tq >= CMP_BLOCK - 1), 1.0 / den1, 0.0)
    o_c = _dot(vcct_ref[0, 0, 0], p1.astype(BF16))

    p1sum = p1[:, 0:Q_TILE]
    for r in range(1, NSA_REP):
        p1sum = p1sum + p1[:, r * Q_TILE:(r + 1) * Q_TILE]
    ps_hi, ps_lo = _split_bf16(p1sum, 2)
    ovl = ovl_ref[...]
    imp = _dot(ovl, ps_hi) + _dot(ovl, ps_lo)
    nsb = imp.shape[0]
    jb = lax.broadcasted_iota(jnp.int32, (nsb, Q_TILE), 0)
    cur = tq // SEL_BLOCK
    forced = (jb == 0) | (jb == cur) | (jb == cur - 1)
    valid = jb * SEL_BLOCK <= tq
    impm = jnp.where(valid, jnp.where(forced, jnp.inf, imp), -jnp.inf)
    imp_ref[...] = impm
    sub = lax.broadcasted_iota(jnp.int32, (8, Q_TILE), 0)
    ranks = [jnp.zeros((8, Q_TILE), F32) for _ in range(nsb // 8)]
    for i in range(nsb):
        xi = jnp.broadcast_to(imp_ref[i:i + 1, :], (8, Q_TILE))
        for v in range(nsb // 8):
            xv = impm[8 * v:8 * v + 8, :]
            ge = jnp.where(xi >= xv, 1.0, 0.0)
            gt = jnp.where(xi > xv, 1.0, 0.0)
            if 8 * v > i:
                inc = ge
            elif 8 * v + 7 < i:
                inc = gt
            else:
                inc = jnp.where(sub + 8 * v > i, ge, gt)
            ranks[v] = ranks[v] + inc
    rank = jnp.concatenate(ranks, axis=0)
    selb_ref[...] = jnp.where((rank < SEL_TOPN) & valid, 0.0, NEG)

    blocks_per_chunk = SEL_CHUNK // SEL_BLOCK
    last_chunk = ks_ref.shape[3] // SEL_CHUNK - 1
    n_chunks = (start + Q_TILE + SEL_CHUNK - 1) // SEL_CHUNK
    own_row = pl.multiple_of(start - (n_chunks - 1) * SEL_CHUNK, Q_TILE)
    own_mask = _tile_heads(jnp.where(
        lax.broadcasted_iota(jnp.int32, (Q_TILE, Q_TILE), 0) <= lax.broadcasted_iota(jnp.int32, (Q_TILE, Q_TILE), 1),
        0.0, 1.0))
    bias_pad = jnp.zeros((HEAD_DIM - blocks_per_chunk, NSA_REP * Q_TILE), F32)

    def scores(chunk):
        blk_bias = _tile_heads(selb_ref[pl.ds(pl.multiple_of(chunk * blocks_per_chunk, blocks_per_chunk),
                                              blocks_per_chunk), :])
        qa_ref[HEAD_DIM:, :] = jnp.concatenate([blk_bias, bias_pad], axis=0).astype(BF16)
        k0 = pl.multiple_of(chunk * SEL_CHUNK, SEL_CHUNK)
        return _dot(ks_ref[0, 0, 0, pl.ds(k0, SEL_CHUNK), :], qa_ref[...])

    def sel_step(c, s_cur_ref, s_next_ref):
        s_next_ref[...] = scores(jnp.minimum(c + 1, last_chunk))
        v0 = pl.multiple_of(jnp.maximum(c - 1, 0) * SEL_CHUNK, SEL_CHUNK)
        pv = _dot(vst_ref[0, 0, 0, :, pl.ds(v0, SEL_CHUNK)], p_ref[...])

        own_neg = jnp.where(c == n_chunks - 1, NEG, 0.0)
        s_cur_ref[pl.ds(own_row, Q_TILE), :] = s_cur_ref[pl.ds(own_row, Q_TILE), :] + own_mask * own_neg
        sm = s_cur_ref[...]
        m = m_ref[...]
        m_new = jnp.maximum(m, jnp.max(sm, axis=0, keepdims=True))
        p = jnp.exp2(sm - m_new)
        alpha = jnp.exp2(m - m_new)
        l_ref[...] = alpha * l_ref[...] + jnp.sum(p, axis=0, keepdims=True)
        acc_ref[...] = alpha * (acc_ref[...] + pv)
        m_ref[...] = m_new
        p_ref[...] = p.astype(BF16)

    qa_ref[0:HEAD_DIM, :] = q_t
    sa_ref[...] = scores(0)
    p_ref[...] = jnp.zeros_like(p_ref)

    dist = tq - (w0 + lax.broadcasted_iota(jnp.int32, (wkeys, 1), 0))
    sm3 = s3 + _tile_heads(jnp.where((dist >= 0) & (dist < WINDOW), 0.0, NEG))
    e3 = jnp.exp2(sm3 - jnp.max(sm3, axis=0, keepdims=True))
    den3_ref[...] = jnp.sum(e3, axis=0, keepdims=True)
    p3_ref[...] = e3.astype(BF16)
    oc_ref[...] = o_c

    m_ref[...] = jnp.full(m_ref.shape, NEG, F32)
    l_ref[...] = jnp.zeros_like(l_ref)
    acc_ref[...] = jnp.zeros_like(acc_ref)

    def sel_pair(i, carry):
        sel_step(2 * i, sa_ref, sb_ref)

        @pl.when(2 * i + 1 < n_chunks)
        def _():
            sel_step(2 * i + 1, sb_ref, sa_ref)
        return carry

    lax.fori_loop(0, (n_chunks + 1) // 2, sel_pair, 0)

    vl = pl.multiple_of((n_chunks - 1) * SEL_CHUNK, SEL_CHUNK)
    acc_w = _dot(vwt_ref[0, 0, 0, :, pl.ds(w0, wkeys)], p3_ref[...])
    acc_s = acc_ref[...] + _dot(vst_ref[0, 0, 0, :, pl.ds(vl, SEL_CHUNK)], p_ref[...])
    gates = gate_ref[0]
    g_c = jnp.concatenate([gates[3 * r:3 * r + 1, :] for r in range(NSA_REP)], axis=1)
    g_s = jnp.concatenate([gates[3 * r + 1:3 * r + 2, :] for r in range(NSA_REP)], axis=1)
    g_w = jnp.concatenate([gates[3 * r + 2:3 * r + 3, :] for r in range(NSA_REP)], axis=1)
    out_t = g_c * oc_ref[...] + (g_w / den3_ref[...]) * acc_w + (g_s / l_ref[...]) * acc_s
    out_t = jnp.concatenate([out_t[:, r * Q_TILE:(r + 1) * Q_TILE] for r in range(NSA_REP)], axis=0)
    o_ref[...] = out_t.T.astype(o_ref.dtype)


def _nsa(q_t, cc, cc_t, ksel, kwin, vt, gates_t, ovl, batch, seq):
    nqb = seq // Q_TILE
    ncb_pad = cc.shape[3]
    nsb = seq // SEL_BLOCK
    rows = NSA_REP * Q_TILE
    vspec = lambda which: pl.BlockSpec((1, 1, 1, HEAD_DIM, seq), lambda b, g, i: (which, b, g, 0, 0))
    return pl.pallas_call(
        _nsa_body,
        grid=(batch, NSA_GROUPS, nqb),
        in_specs=[
            pl.BlockSpec((1, NSA_REP, HEAD_DIM, Q_TILE), lambda b, g, i: (b, g, 0, i)),
            pl.BlockSpec((1, 1, 1, ncb_pad, HEAD_DIM), lambda b, g, i: (0, b, g, 0, 0)),
            pl.BlockSpec((1, 1, 1, HEAD_DIM, ncb_pad), lambda b, g, i: (1, b, g, 0, 0)),
            pl.BlockSpec((1, 1, 1, seq, LANES), lambda b, g, i: (0, b, g, 0, 0)),
            pl.BlockSpec((1, 1, 1, seq, HEAD_DIM), lambda b, g, i: (0, b, g, 0, 0)),
            vspec(0), vspec(1),
            pl.BlockSpec((1, GATE_ROWS, Q_TILE), lambda b, g, i: (g, 0, b * nqb + i)),
            pl.BlockSpec(ovl.shape, lambda b, g, i: (0, 0)),
        ],
        out_specs=pl.BlockSpec((Q_TILE, NSA_REP * HEAD_DIM), lambda b, g, i: (b * nqb + i, g)),
        out_shape=jax.ShapeDtypeStruct((batch * seq, NSA_HEADS * HEAD_DIM), BF16),
        scratch_shapes=[pltpu.VMEM((nsb, Q_TILE), F32), pltpu.VMEM((nsb, Q_TILE), F32),
                        pltpu.VMEM((2 * HEAD_DIM, rows), BF16),
                        pltpu.VMEM((SEL_CHUNK, rows), F32), pltpu.VMEM((SEL_CHUNK, rows), F32),
                        pltpu.VMEM((SEL_CHUNK, rows), BF16),
                        pltpu.VMEM((1, rows), F32), pltpu.VMEM((1, rows), F32),
                        pltpu.VMEM((HEAD_DIM, rows), F32), pltpu.VMEM((HEAD_DIM, rows), F32),
                        pltpu.VMEM((WINDOW + Q_TILE, rows), BF16), pltpu.VMEM((1, rows), F32)],
        compiler_params=pltpu.CompilerParams(
            dimension_semantics=("parallel", "parallel", "arbitrary"), vmem_limit_bytes=VMEM_LIMIT),
        name="nsa",
    )(q_t, cc, cc_t, ksel[None], kwin[None], vt, vt, gates_t, ovl)


def _hgrn_body(hq_ref, hf_ref, hi_ref, hg_ref, lbl_ref, gain_ref, o_ref,
               state_ref, b_ref, k_ref, v_ref, q_ref, *, layer):
    @pl.when(pl.program_id(1) == 0)
    def _():
        state_ref[...] = jnp.zeros_like(state_ref)

    logits = lbl_ref[...]
    ex = jnp.exp(logits - jnp.max(logits, axis=0, keepdims=True))
    sm = ex / jnp.sum(ex, axis=0, keepdims=True)
    lb_all = jnp.zeros((1, HG_HEADS * HG_DK), F32)
    for i in range(1, layer + 1):
        lb_all = lb_all + sm[i:i + 1, :]

    ch, blk, half = HG_CHUNK, HG_SUB, HG_SUB // 2
    nblk = ch // blk
    ri = lax.broadcasted_iota(jnp.int32, (ch, ch), 0)
    ci = lax.broadcasted_iota(jnp.int32, (ch, ch), 1)
    tril = jnp.where(ri >= ci, 1.0, 0.0).astype(BF16)
    blk_row = lax.broadcasted_iota(jnp.int32, (blk, HG_DK), 0)
    low_rows = blk_row >= half
    rr = lax.broadcasted_iota(jnp.int32, (blk, blk), 0)
    cc = lax.broadcasted_iota(jnp.int32, (blk, blk), 1)
    cross = (rr >= half) & (cc < half)
    half_row = lax.broadcasted_iota(jnp.int32, (half, HG_DK), 0)
    gain = gain_ref[...]
    heads = range(HG_HEADS)

    def chunk(j, carry):
        r0 = pl.multiple_of(j * ch, ch)
        for h in heads:
            cols = slice(h * HG_DK, (h + 1) * HG_DK)
            lb = lb_all[:, cols]
            hq = hq_ref[0, pl.ds(r0, ch), cols]
            hf = hf_ref[0, pl.ds(r0, ch), cols]
            f = lb + (1.0 - lb) * jax.nn.sigmoid(hf)
            lf = _split_bf16(jnp.log(jnp.maximum(f, 1e-30)), 3)
            b_ref[h] = _dot(tril, lf[0]) + _dot(tril, lf[1]) + _dot(tril, lf[2])
            k_ref[h] = 1.0 - f
            v_ref[h] = hi_ref[0, pl.ds(r0, ch), cols]
            q_ref[h] = hq * jax.nn.sigmoid(hq)

        o_blocks, tables = [], []
        for h in heads:
            inter = _dot_nt((q_ref[h] * jnp.exp(b_ref[h])).astype(BF16), state_ref[h].astype(BF16))
            o_blocks.append([inter[ib * blk:(ib + 1) * blk, :] for ib in range(nblk)])
            tables.append([])
            for ib in range(nblk):
                i0 = ib * blk
                qi = q_ref[h, i0:i0 + blk, :]
                bi = b_ref[h, i0:i0 + blk, :]
                a = None
                if ib > 0:
                    ref_b = b_ref[h, i0 - 1:i0, :]
                    qd = (qi * jnp.exp(bi - ref_b)).astype(BF16)
                    kd = (k_ref[h, 0:i0, :] * jnp.exp(ref_b - b_ref[h, 0:i0, :])).astype(BF16)
                    a = _dot_nt(qd, kd).astype(BF16)
                mid_b = b_ref[h, i0 + half - 1:i0 + half, :]
                qd2 = (qi * jnp.exp(jnp.where(low_rows, bi - mid_b, 0.0))).astype(BF16)
                kd2 = (k_ref[h, i0:i0 + blk, :] * jnp.exp(jnp.where(low_rows, 0.0, mid_b - bi))).astype(BF16)
                a2 = jnp.where(cross, _dot_nt(qd2, kd2), 0.0).astype(BF16)
                tables[h].append((a, a2))
        for h in heads:
            bl = b_ref[h, ch - 1:ch, :]
            kd_end = (k_ref[h] * jnp.exp(bl - b_ref[h])).astype(BF16)
            vt = v_ref[h].T.astype(BF16)
            state_ref[h] = state_ref[h] * jnp.exp(bl) + _dot(vt, kd_end)

        for h in heads:
            for ib in range(nblk):
                i0 = ib * blk
                a, a2 = tables[h][ib]
                o_i = o_blocks[h][ib] + _dot(a2, v_ref[h, i0:i0 + blk, :].astype(BF16))
                if a is not None:
                    o_i = o_i + _dot(a, v_ref[h, 0:i0, :].astype(BF16))
                o_blocks[h][ib] = o_i

        nhalf = ch // half
        diag = [[jnp.zeros((half, HG_DV), F32) for _ in range(nhalf)] for _ in heads]
        for s in range(half):
            for h in heads:
                for ih in range(nhalf):
                    g0 = ih * half
                    bs = b_ref[h, g0 + s:g0 + s + 1, :]
                    ks = k_ref[h, g0 + s:g0 + s + 1, :]
                    vs = v_ref[h, g0 + s:g0 + s + 1, :]
                    bi = b_ref[h, g0:g0 + half, :]
                    qi = q_ref[h, g0:g0 + half, :]
                    dec = jnp.exp(jnp.where(half_row >= s, bi - bs, NEG))
                    a_s = jnp.sum(qi * dec * ks, axis=-1, keepdims=True)
                    diag[h][ih] = diag[h][ih] + a_s * vs

        for h in heads:
            cols = slice(h * HG_DK, (h + 1) * HG_DK)
            o = jnp.concatenate(o_blocks[h], axis=0) + jnp.concatenate(diag[h], axis=0)
            o = o * lax.rsqrt(jnp.mean(o * o, axis=-1, keepdims=True) + EPS) * gain
            hg = hg_ref[0, pl.ds(r0, ch), cols]
            o_ref[pl.ds(r0, ch), cols] = (o * (hg * jax.nn.sigmoid(hg))).astype(o_ref.dtype)
        return carry

    lax.fori_loop(0, hq_ref.shape[1] // ch, chunk, 0)


def _hgrn(h_all, lb_logits, out_gain, layer, batch, seq):
    t = h_all.shape[1]
    nblk = seq // HG_ROWS
    nl = lb_logits.shape[0]
    hw = HG_HEADS * HG_DK
    hspec = lambda which: pl.BlockSpec((1, HG_ROWS, hw), lambda b, c: (which, b * nblk + c, 0))
    return pl.pallas_call(
        functools.partial(_hgrn_body, layer=layer),
        grid=(batch, nblk),
        in_specs=[hspec(0), hspec(1), hspec(2), hspec(3),
                  pl.BlockSpec((nl, hw), lambda b, c: (0, 0)),
                  pl.BlockSpec((1, HG_DV), lambda b, c: (0, 0))],
        out_specs=pl.BlockSpec((HG_ROWS, HG_HEADS * HG_DV), lambda b, c: (b * nblk + c, 0)),
        out_shape=jax.ShapeDtypeStruct((t, HG_HEADS * HG_DV), BF16),
        scratch_shapes=[pltpu.VMEM((HG_HEADS, HG_DV, HG_DK), F32)]
        + [pltpu.VMEM((HG_HEADS, HG_CHUNK, HG_DK), F32)] * 4,
        compiler_params=pltpu.CompilerParams(
            dimension_semantics=("parallel", "arbitrary"), vmem_limit_bytes=VMEM_LIMIT),
        name="hgrn",
    )(h_all, h_all, h_all, h_all, lb_logits, out_gain.reshape(1, HG_DV))


def _relayout_w_in(w):
    d = w.shape[0]
    nq = NSA_HEADS * HEAD_DIM
    nkv = 6 * NSA_GROUPS * HEAD_DIM
    ng = 3 * NSA_HEADS
    q_kv = w[:, :nq + nkv]
    gts = w[:, nq + nkv:nq + nkv + ng]
    hg = w[:, nq + nkv + ng:]
    per = 3 * NSA_REP
    gate_blocks = [jnp.pad(gts[:, gi * per:(gi + 1) * per], ((0, 0), (0, LANES - per)))
                   for gi in range(NSA_GROUPS)]
    out = jnp.concatenate([q_kv, hg] + gate_blocks, axis=1).astype(BF16)
    assert out.shape == (d, _C_END)
    return out


def _compress_weights(pos, w1, w2):
    eye = jnp.eye(NSA_GROUPS, dtype=F32)
    e = w1.shape[-1]

    def big(w1_half):
        y = jnp.einsum("sldh,gk->slgdkh", w1_half, eye)
        return y.reshape(2, CMP_STRIDE * NSA_GROUPS * HEAD_DIM, NSA_GROUPS * e).astype(BF16)

    def pos_rows(p_half):
        y = jnp.broadcast_to(p_half[:, :, None, :], (2, CMP_STRIDE, NSA_GROUPS, HEAD_DIM))
        return y.reshape(2, 1, CMP_STRIDE * NSA_GROUPS * HEAD_DIM)

    w2b = jnp.einsum("shd,gk->sghkd", w2, eye).reshape(2, NSA_GROUPS * e, NSA_GROUPS * HEAD_DIM)
    return (big(w1[:, :CMP_STRIDE]), big(w1[:, CMP_STRIDE:]),
            pos_rows(pos[:, :CMP_STRIDE]), pos_rows(pos[:, CMP_STRIDE:]), w2b.astype(BF16))


def kernel(x, ffn1_norm, ffn1_w_gate, ffn1_w_up, ffn1_w_down, mix_norm, w_in, q_norm, k_norm,
           cmp_pos, cmp_w1, cmp_w2, hgrn_lb_logits, hgrn_out_norm, w_out,
           ffn2_norm, ffn2_w_gate, ffn2_w_up, ffn2_w_down):
    batch, seq, d = x.shape
    depth = w_in.shape[0]
    assert CMP_BLOCK == 2 * CMP_STRIDE and seq % SEL_CHUNK == 0 and seq >= WINDOW + Q_TILE
    assert seq % HG_ROWS == 0 and SEL_CHUNK % SEL_BLOCK == 0 and SEL_CHUNK % Q_TILE == 0
    t = batch * seq
    tm_proj = 256

    inv = 1.0 / (ROPE_THETA ** (jnp.arange(0, HEAD_DIM, 2, dtype=F32) / HEAD_DIM))
    ang = jnp.arange(seq, dtype=F32)[:, None] * inv[None, :]
    ang = jnp.concatenate([ang, ang], axis=-1)
    cos1, sin1 = jnp.cos(ang), jnp.sin(ang)
    cos2, sin2 = jnp.tile(cos1, (1, 2)), jnp.tile(sin1, (1, 2))
    cost, sint = cos1.T, sin1.T

    nseg = seq // CMP_STRIDE
    nsb = seq // SEL_BLOCK
    ci = np.arange(nseg)[None, :]
    sj = np.arange(nsb)[:, None]
    ovl = ((ci * CMP_STRIDE <= sj * SEL_BLOCK + SEL_BLOCK - 1)
           & (ci * CMP_STRIDE + CMP_BLOCK - 1 >= sj * SEL_BLOCK)
           & (ci < (seq - CMP_BLOCK) // CMP_STRIDE + 1))
    ovl = jnp.asarray(ovl, dtype=BF16)

    xt = x.reshape(t, d)
    for l in range(depth):
        xt = _ffn(xt, ffn1_norm[l], ffn1_w_gate[l], ffn1_w_up[l], ffn1_w_down[l])
        qgt = jnp.broadcast_to(q_norm[l][:, None], (HEAD_DIM, tm_proj))
        kg2 = jnp.tile(k_norm[l], (1, 2))
        q_t, cmp_in, ksel, kwin, vt, gates_t, h_all = _inproj(
            xt, mix_norm[l], _relayout_w_in(w_in[l]), cos2, sin2, cost, sint, qgt, kg2,
            batch, seq, tm=tm_proj)
        w1a, w1b, pa, pb, w2b = _compress_weights(cmp_pos[l], cmp_w1[l], cmp_w2[l])
        cc, cc_t = _compress(cmp_in, w1a, w1b, pa, pb, w2b, batch, seq)
        o_nsa = _nsa(q_t, cc, cc_t, ksel, kwin, vt, gates_t, ovl, batch, seq)
        o_hg = _hgrn(h_all, hgrn_lb_logits, hgrn_out_norm[l], l, batch, seq)
        nq = NSA_HEADS * HEAD_DIM
        wo = w_out[l]
        xt = _ffn(xt, ffn2_norm[l], ffn2_w_gate[l], ffn2_w_up[l], ffn2_w_down[l],
                  mix=(o_nsa, o_hg, wo[:nq], wo[nq:]))
    return xt.reshape(batch, seq, d)
```

```python
import functools

import jax
import jax.numpy as jnp
import numpy as np
from jax import lax
from jax.experimental import pallas as pl
from jax.experimental.pallas import tpu as pltpu

NSA_HEADS = 8
NSA_GROUPS = 2
NSA_REP = NSA_HEADS // NSA_GROUPS
HEAD_DIM = 64
CMP_BLOCK = 32
CMP_STRIDE = 16
CMP_HIDDEN = 128
SEL_BLOCK = 64
SEL_TOPN = 16
WINDOW = 512
HG_HEADS = 4
HG_DK = 128
HG_DV = 128
ROPE_THETA = 10000.0
EPS = 1e-6
NEG = -1e30
LOG2E = 1.4426950408889634

LANES = 128
Q_TILE = 128
NSA_TILES = 2
SEL_CHUNK = 512
HG_CHUNK = 64
HG_SUB = 16
HG_ROWS = 512
VMEM_LIMIT = 56 * 1024 * 1024

BF16 = jnp.bfloat16
F32 = jnp.float32


def _dot(a, b):
    return jnp.dot(a, b, preferred_element_type=F32)


def _dot_nt(a, b):
    return lax.dot_general(a, b, (((1,), (1,)), ((), ())), preferred_element_type=F32)


def _split_bf16(x, parts):
    out = []
    rem = x
    for _ in range(parts):
        p = rem.astype(BF16)
        out.append(p)
        rem = rem - p.astype(F32)
    return out


def _ffn_body(*refs, fuse_mix):
    if fuse_mix:
        (x_ref, ma_ref, mb_ref, woa_ref, wob_ref, g_ref, wg_ref, wu_ref, wd_ref,
         o_ref, xs_ref, hn_ref, acc_ref) = refs
    else:
        x_ref, g_ref, wg_ref, wu_ref, wd_ref, o_ref, xs_ref, hn_ref, acc_ref = refs
    f = pl.program_id(1)

    @pl.when(f == 0)
    def _():
        x = x_ref[...]
        if fuse_mix:
            x = x + _dot(ma_ref[...], woa_ref[...].astype(BF16)) + _dot(mb_ref[...], wob_ref[...].astype(BF16))
        xs_ref[...] = x
        ms = jnp.mean(x * x, axis=-1, keepdims=True)
        hn_ref[...] = (x * lax.rsqrt(ms + EPS) * g_ref[...]).astype(BF16)
        acc_ref[...] = jnp.zeros_like(acc_ref)

    hn = hn_ref[...]
    gate = _dot(hn, wg_ref[...].astype(BF16))
    up = _dot(hn, wu_ref[...].astype(BF16))
    act = (gate * jax.nn.sigmoid(gate) * up).astype(BF16)
    acc_ref[...] += _dot(act, wd_ref[...].astype(BF16))

    @pl.when(f == pl.num_programs(1) - 1)
    def _():
        o_ref[...] = xs_ref[...] + 0.5 * acc_ref[...]


def _ffn(x, g, wg, wu, wd, mix=None, *, tm=1024, tf=256):
    t, d = x.shape
    ff = wg.shape[1]
    assert t % tm == 0 and ff % tf == 0
    fuse = mix is not None
    row = lambda i, f: (i, 0)
    in_specs = [pl.BlockSpec((tm, d), row)]
    args = [x]
    if fuse:
        ma, mb, woa, wob = mix
        in_specs += [pl.BlockSpec((tm, ma.shape[1]), row), pl.BlockSpec((tm, mb.shape[1]), row),
                     pl.BlockSpec(woa.shape, lambda i, f: (0, 0)),
                     pl.BlockSpec(wob.shape, lambda i, f: (0, 0))]
        args += [ma, mb, woa, wob]
    in_specs += [pl.BlockSpec((1, d), lambda i, f: (0, 0)),
                 pl.BlockSpec((d, tf), lambda i, f: (0, f)),
                 pl.BlockSpec((d, tf), lambda i, f: (0, f)),
                 pl.BlockSpec((tf, d), lambda i, f: (f, 0))]
    args += [g.reshape(1, d), wg, wu, wd]
    return pl.pallas_call(
        functools.partial(_ffn_body, fuse_mix=fuse),
        grid=(t // tm, ff // tf),
        in_specs=in_specs,
        out_specs=pl.BlockSpec((tm, d), row),
        out_shape=jax.ShapeDtypeStruct((t, d), F32),
        scratch_shapes=[pltpu.VMEM((tm, d), F32), pltpu.VMEM((tm, d), BF16), pltpu.VMEM((tm, d), F32)],
        compiler_params=pltpu.CompilerParams(
            dimension_semantics=("parallel", "arbitrary"), vmem_limit_bytes=VMEM_LIMIT),
        name="ffn_mix" if fuse else "ffn",
    )(*args)


_C_Q = 0
_C_KV = NSA_HEADS * HEAD_DIM
_C_HG = _C_KV + 6 * LANES
_C_GATE = _C_HG + 4 * HG_HEADS * HG_DK
_C_END = _C_GATE + NSA_GROUPS * LANES
GATE_ROWS = 16
Q_SCALE = HEAD_DIM ** -0.5 * LOG2E


def _inproj_body(x_ref, g_ref, w_ref, cos_ref, sin_ref, cost_ref, sint_ref, qg_ref, kg_ref,
                 q_ref, cmp_ref, ksel_ref, kwin_ref, vt_ref, gate_ref, h_ref, *, tiles_per_seq):
    x = x_ref[...]
    ms = jnp.mean(x * x, axis=-1, keepdims=True)
    hn = (x * lax.rsqrt(ms + EPS) * g_ref[...]).astype(BF16)
    cos = cos_ref[...]
    sin = sin_ref[...]
    tm = x.shape[0]
    half = HEAD_DIM // 2

    def proj(c0, width):
        return _dot(hn, w_ref[:, c0:c0 + width])

    ri = (lax.broadcasted_iota(jnp.int32, (2 * LANES, LANES), 0) % LANES) // HEAD_DIM
    ci = lax.broadcasted_iota(jnp.int32, (2 * LANES, LANES), 1) // HEAD_DIM
    head_mean = jnp.where(ri == ci, 1.0 / HEAD_DIM, 0.0).astype(BF16)
    lane = lax.broadcasted_iota(jnp.int32, (tm, LANES), 1)
    first_half = (lane % HEAD_DIM) < half

    def norm_rope(y, gain):
        msq = _dot(jnp.concatenate(_split_bf16(y * y, 2), axis=1), head_mean)
        yn = y * lax.rsqrt(msq + EPS) * gain
        rot = jnp.where(first_half, -pltpu.roll(yn, LANES - half, 1), pltpu.roll(yn, half, 1))
        return yn * cos + rot * sin

    def heads_t(y):
        return y.T.reshape(2, HEAD_DIM, tm)

    qgain = (qg_ref[...] * Q_SCALE)[None]
    cost = cost_ref[...][None]
    sint = sint_ref[...][None]
    yq = proj(_C_Q, NSA_HEADS * HEAD_DIM)
    ykv = proj(_C_KV, 6 * LANES)
    kv = lambda j: ykv[:, j * LANES:(j + 1) * LANES]
    for c in range(NSA_HEADS // 2):
        y3 = heads_t(yq[:, c * LANES:(c + 1) * LANES])
        yn = y3 * lax.rsqrt(jnp.mean(y3 * y3, axis=1, keepdims=True) + EPS) * qgain
        rot = jnp.concatenate([-yn[:, half:], yn[:, :half]], axis=1)
        q_ref[0, 2 * c:2 * c + 2] = (yn * cost + rot * sint).astype(BF16)

    kc = norm_rope(kv(0), kg_ref[0:1, :])
    cmp_ref[0] = kc.astype(BF16)
    cmp_ref[1] = kv(1).astype(BF16)
    ksel = norm_rope(kv(2), kg_ref[1:2, :])
    pos = (pl.program_id(0) % tiles_per_seq) * tm + lax.broadcasted_iota(jnp.int32, (tm, HEAD_DIM), 0)
    blk_in_chunk = (pos // SEL_BLOCK) % (SEL_CHUNK // SEL_BLOCK)
    onehot = jnp.where(lax.broadcasted_iota(jnp.int32, (tm, HEAD_DIM), 1) == blk_in_chunk, 1.0, 0.0)
    for gi in range(NSA_GROUPS):
        ksel_ref[0, gi] = jnp.concatenate(
            [ksel[:, gi * HEAD_DIM:(gi + 1) * HEAD_DIM], onehot], axis=1).astype(BF16)
    kwin = norm_rope(kv(4), kg_ref[2:3, :]).astype(BF16)
    kwin_ref[0, 0] = kwin[:, :HEAD_DIM]
    kwin_ref[0, 1] = kwin[:, HEAD_DIM:]
    for which, col in ((0, 3), (1, 5)):
        vt_ref[which, 0] = heads_t(kv(col)).astype(BF16)

    hw = HG_HEADS * HG_DK
    for i in range(4):
        h_ref[i] = proj(_C_HG + i * hw, hw)
    ygate = proj(_C_GATE, NSA_GROUPS * LANES)
    for gi in range(NSA_GROUPS):
        gt = jax.nn.sigmoid(ygate[:, gi * LANES:(gi + 1) * LANES]).T
        gate_ref[gi] = gt[:GATE_ROWS]


def _inproj(x, g, w2, cos2, sin2, cost, sint, qgt, kg2, batch, seq, *, tm=256):
    t, d = x.shape
    nst = seq // tm
    hw = HG_HEADS * HG_DK
    out_shape = (
        jax.ShapeDtypeStruct((batch, NSA_HEADS, HEAD_DIM, seq), BF16),
        jax.ShapeDtypeStruct((2, t, LANES), BF16),
        jax.ShapeDtypeStruct((batch, NSA_GROUPS, seq, LANES), BF16),
        jax.ShapeDtypeStruct((batch, NSA_GROUPS, seq, HEAD_DIM), BF16),
        jax.ShapeDtypeStruct((2, batch, NSA_GROUPS, HEAD_DIM, seq), BF16),
        jax.ShapeDtypeStruct((NSA_GROUPS, GATE_ROWS, t), F32),
        jax.ShapeDtypeStruct((4, t, hw), F32),
    )
    out_specs = (
        pl.BlockSpec((1, NSA_HEADS, HEAD_DIM, tm), lambda i: (i // nst, 0, 0, i % nst)),
        pl.BlockSpec((2, tm, LANES), lambda i: (0, i, 0)),
        pl.BlockSpec((1, NSA_GROUPS, tm, LANES), lambda i: (i // nst, 0, i % nst, 0)),
        pl.BlockSpec((1, NSA_GROUPS, tm, HEAD_DIM), lambda i: (i // nst, 0, i % nst, 0)),
        pl.BlockSpec((2, 1, NSA_GROUPS, HEAD_DIM, tm), lambda i: (0, i // nst, 0, 0, i % nst)),
        pl.BlockSpec((NSA_GROUPS, GATE_ROWS, tm), lambda i: (0, 0, i)),
        pl.BlockSpec((4, tm, hw), lambda i: (0, i, 0)),
    )
    in_specs = [
        pl.BlockSpec((tm, d), lambda i: (i, 0)),
        pl.BlockSpec((1, d), lambda i: (0, 0)),
        pl.BlockSpec(w2.shape, lambda i: (0, 0)),
        pl.BlockSpec((tm, LANES), lambda i: (i % nst, 0)),
        pl.BlockSpec((tm, LANES), lambda i: (i % nst, 0)),
        pl.BlockSpec((HEAD_DIM, tm), lambda i: (0, i % nst)),
        pl.BlockSpec((HEAD_DIM, tm), lambda i: (0, i % nst)),
        pl.BlockSpec((HEAD_DIM, tm), lambda i: (0, 0)),
        pl.BlockSpec((3, LANES), lambda i: (0, 0)),
    ]
    return pl.pallas_call(
        functools.partial(_inproj_body, tiles_per_seq=nst),
        grid=(t // tm,),
        in_specs=in_specs,
        out_specs=out_specs,
        out_shape=out_shape,
        compiler_params=pltpu.CompilerParams(
            dimension_semantics=("parallel",), vmem_limit_bytes=VMEM_LIMIT),
        name="inproj",
    )(x, g.reshape(1, d), w2, cos2, sin2, cost, sint, qgt, kg2)


def _compress_body(seg_ref, w1a_ref, w1b_ref, pa_ref, pb_ref, w2_ref, o_ref, ot_ref):
    segs = seg_ref[0, 0]
    w1a = w1a_ref[0]
    w1b = w1b_ref[0]
    first = _dot(segs, w1a)
    second = _dot(segs, w1b)
    pa_hi, pa_lo = _split_bf16(jnp.broadcast_to(pa_ref[0], (8, pa_ref.shape[2])), 2)
    pb_hi, pb_lo = _split_bf16(jnp.broadcast_to(pb_ref[0], (8, pb_ref.shape[2])), 2)
    cpos = _dot(pa_hi, w1a) + _dot(pa_lo, w1a) + _dot(pb_hi, w1b) + _dot(pb_lo, w1b)
    nseg = first.shape[0]
    pre = first + pltpu.roll(second, nseg - 1, 0) + cpos[0:1, :]
    hid = jax.nn.gelu(pre, approximate=True).astype(BF16)
    out = _dot(hid, w2_ref[0])
    outb = out.astype(BF16)
    o_ref[0, 0, 0] = outb[:, :HEAD_DIM]
    o_ref[0, 0, 1] = outb[:, HEAD_DIM:]
    ot_ref[0, 0] = out.T.reshape(NSA_GROUPS, HEAD_DIM, nseg).astype(BF16)


def _compress(cmp_in, w1a, w1b, pa, pb, w2, batch, seq):
    nseg = seq // CMP_STRIDE
    width = CMP_STRIDE * LANES
    segs = cmp_in.reshape(2, batch, nseg, width)
    return pl.pallas_call(
        _compress_body,
        grid=(2, batch),
        in_specs=[
            pl.BlockSpec((1, 1, nseg, width), lambda s, b: (s, b, 0, 0)),
            pl.BlockSpec((1,) + w1a.shape[1:], lambda s, b: (s, 0, 0)),
            pl.BlockSpec((1,) + w1b.shape[1:], lambda s, b: (s, 0, 0)),
            pl.BlockSpec((1, 1, width), lambda s, b: (s, 0, 0)),
            pl.BlockSpec((1, 1, width), lambda s, b: (s, 0, 0)),
            pl.BlockSpec((1,) + w2.shape[1:], lambda s, b: (s, 0, 0)),
        ],
        out_specs=(pl.BlockSpec((1, 1, NSA_GROUPS, nseg, HEAD_DIM), lambda s, b: (s, b, 0, 0, 0)),
                   pl.BlockSpec((1, 1, NSA_GROUPS, HEAD_DIM, nseg), lambda s, b: (s, b, 0, 0, 0))),
        out_shape=(jax.ShapeDtypeStruct((2, batch, NSA_GROUPS, nseg, HEAD_DIM), BF16),
                   jax.ShapeDtypeStruct((2, batch, NSA_GROUPS, HEAD_DIM, nseg), BF16)),
        compiler_params=pltpu.CompilerParams(
            dimension_semantics=("parallel", "parallel"), vmem_limit_bytes=VMEM_LIMIT),
        name="compress",
    )(segs, w1a, w1b, pa, pb, w2)


def _tile_heads(a):
    return jnp.concatenate([a] * NSA_REP, axis=1)


def _nsa_body(q_ref, kcc_ref, vcct_ref, ks_ref, kw_ref, vst_ref, vwt_ref, gate_ref, ovl_ref,
              o_ref, imp_ref, selb_ref, qa_ref, sa_ref, sb_ref, p_ref, m_ref, l_ref, acc_ref,
              oc_ref, p3_ref, den3_ref):
    tiles = range(NSA_TILES)
    first = pl.program_id(2) * (NSA_TILES * Q_TILE)
    starts = [first + t * Q_TILE for t in tiles]
    lanes = lambda t: slice(t * Q_TILE, (t + 1) * Q_TILE)
    q_t = [jnp.concatenate([q_ref[0, r, :, lanes(t)] for r in range(NSA_REP)], axis=1) for t in tiles]
    tq = [starts[t] + lax.broadcasted_iota(jnp.int32, (1, Q_TILE), 1) for t in tiles]

    kcc = kcc_ref[0, 0, 0]
    ncb = kcc.shape[0]
    wkeys = WINDOW + Q_TILE
    w0 = [pl.multiple_of(jnp.maximum(starts[t] - WINDOW, 0), Q_TILE) for t in tiles]
    s1 = [_dot(kcc, q_t[t]) for t in tiles]
    s3 = [_dot(kw_ref[0, 0, 0, pl.ds(w0[t], wkeys), :], q_t[t]) for t in tiles]

    cend = lax.broadcasted_iota(jnp.int32, (ncb, 1), 0) * CMP_STRIDE + (CMP_BLOCK - 1)
    p1 = []
    for t in tiles:
        sm1 = s1[t] + _tile_heads(jnp.where(cend <= tq[t], 0.0, NEG))
        e1 = jnp.exp2(sm1 - jnp.max(sm1, axis=0, keepdims=True))
        den1 = jnp.sum(e1, axis=0, keepdims=True)
        p1.append(e1 * jnp.where(_tile_heads(tq[t] >= CMP_BLOCK - 1), 1.0 / den1, 0.0))
    ovl = ovl_ref[...]
    nsb = ovl.shape[0]
    imp = []
    for t in tiles:
        oc_ref[t] = _dot(vcct_ref[0, 0, 0], p1[t].astype(BF16))
        p1sum = p1[t][:, 0:Q_TILE]
        for r in range(1, NSA_REP):
            p1sum = p1sum + p1[t][:, r * Q_TILE:(r + 1) * Q_TILE]
        ps_hi, ps_lo = _split_bf16(p1sum, 2)
        imp.append(_dot(ovl, ps_hi) + _dot(ovl, ps_lo))

    jb = lax.broadcasted_iota(jnp.int32, (nsb, Q_TILE), 0)
    valid, impm = [], []
    for t in tiles:
        cur = tq[t] // SEL_BLOCK
        forced = (jb == 0) | (jb == cur) | (jb == cur - 1)
        valid.append(jb * SEL_BLOCK <= tq[t])
        impm.append(jnp.where(valid[t], jnp.where(forced, jnp.inf, imp[t]), -jnp.inf))
        imp_ref[t] = impm[t]
    sub = lax.broadcasted_iota(jnp.int32, (8, Q_TILE), 0)
    ranks = [[jnp.zeros((8, Q_TILE), F32) for _ in range(nsb // 8)] for _ in tiles]
    for i in range(nsb):
        for t in tiles:
            xi = jnp.broadcast_to(imp_ref[t, i:i + 1, :], (8, Q_TILE))
            for v in range(nsb // 8):
                xv = impm[t][8 * v:8 * v + 8, :]
                ge = jnp.where(xi >= xv, 1.0, 0.0)
                gt = jnp.where(xi > xv, 1.0, 0.0)
                if 8 * v > i:
                    inc = ge
                elif 8 * v + 7 < i:
                    inc = gt
                else:
                    inc = jnp.where(sub + 8 * v > i, ge, gt)
                ranks[t][v] = ranks[t][v] + inc
    for t in tiles:
        rank = jnp.concatenate(ranks[t], axis=0)
        selb_ref[t] = jnp.where((rank < SEL_TOPN) & valid[t], 0.0, NEG)

    blocks_per_chunk = SEL_CHUNK // SEL_BLOCK
    last_chunk = ks_ref.shape[3] // SEL_CHUNK - 1
    n_chunks = (first + NSA_TILES * Q_TILE + SEL_CHUNK - 1) // SEL_CHUNK
    own_row = [pl.multiple_of(starts[t] - (n_chunks - 1) * SEL_CHUNK, Q_TILE) for t in tiles]
    own_mask = _tile_heads(jnp.where(
        lax.broadcasted_iota(jnp.int32, (Q_TILE, Q_TILE), 0) <= lax.broadcasted_iota(jnp.int32, (Q_TILE, Q_TILE), 1),
        0.0, 1.0))
    bias_pad = jnp.zeros((HEAD_DIM - blocks_per_chunk, NSA_REP * Q_TILE), F32)

    def scores(t, chunk):
        blk_bias = _tile_heads(selb_ref[t, pl.ds(pl.multiple_of(chunk * blocks_per_chunk, blocks_per_chunk),
                                                 blocks_per_chunk), :])
        qa_ref[t, HEAD_DIM:, :] = jnp.concatenate([blk_bias, bias_pad], axis=0).astype(BF16)
        k0 = pl.multiple_of(chunk * SEL_CHUNK, SEL_CHUNK)
        return _dot(ks_ref[0, 0, 0, pl.ds(k0, SEL_CHUNK), :], qa_ref[t])

    def sel_step(c, s_cur_ref, s_next_ref):
        own_neg = jnp.where(c == n_chunks - 1, NEG, 0.0)
        m_old, m_new = [], []
        for t in tiles:
            s_cur_ref[t, pl.ds(own_row[t], Q_TILE), :] = (
                s_cur_ref[t, pl.ds(own_row[t], Q_TILE), :] + own_mask * own_neg)
            m_old.append(m_ref[t])
            m_new.append(jnp.maximum(m_old[t], jnp.max(s_cur_ref[t], axis=0, keepdims=True)))
        nxt = jnp.minimum(c + 1, last_chunk)
        for t in tiles:
            s_next_ref[t] = scores(t, nxt)
        v0 = pl.multiple_of(jnp.maximum(c - 1, 0) * SEL_CHUNK, SEL_CHUNK)
        v_prev = vst_ref[0, 0, 0, :, pl.ds(v0, SEL_CHUNK)]
        pv = [_dot(v_prev, p_ref[t]) for t in tiles]
        for t in tiles:
            p = jnp.exp2(s_cur_ref[t] - m_new[t])
            alpha = jnp.exp2(m_old[t] - m_new[t])
            l_ref[t] = alpha * l_ref[t] + jnp.sum(p, axis=0, keepdims=True)
            acc_ref[t] = alpha * (acc_ref[t] + pv[t])
            m_ref[t] = m_new[t]
            p_ref[t] = p.astype(BF16)

    for t in tiles:
        qa_ref[t, 0:HEAD_DIM, :] = q_t[t]
        sa_ref[t] = scores(t, 0)
    p_ref[...] = jnp.zeros_like(p_ref)

    for t in tiles:
        dist = tq[t] - (w0[t] + lax.broadcasted_iota(jnp.int32, (wkeys, 1), 0))
        sm3 = s3[t] + _tile_heads(jnp.where((dist >= 0) & (dist < WINDOW), 0.0, NEG))
        e3 = jnp.exp2(sm3 - jnp.max(sm3, axis=0, keepdims=True))
        den3_ref[t] = jnp.sum(e3, axis=0, keepdims=True)
        p3_ref[t] = e3.astype(BF16)

    m_ref[...] = jnp.full(m_ref.shape, NEG, F32)
    l_ref[...] = jnp.zeros_like(l_ref)
    acc_ref[...] = jnp.zeros_like(acc_ref)

    def sel_pair(i, carry):
        sel_step(2 * i, sa_ref, sb_ref)

        @pl.when(2 * i + 1 < n_chunks)
        def _():
            sel_step(2 * i + 1, sb_ref, sa_ref)
        return carry

    lax.fori_loop(0, (n_chunks + 1) // 2, sel_pair, 0)

    vl = pl.multiple_of((n_chunks - 1) * SEL_CHUNK, SEL_CHUNK)
    v_last = vst_ref[0, 0, 0, :, pl.ds(vl, SEL_CHUNK)]
    acc_w = [_dot(vwt_ref[0, 0, 0, :, pl.ds(w0[t], wkeys)], p3_ref[t]) for t in tiles]
    acc_s = [acc_ref[t] + _dot(v_last, p_ref[t]) for t in tiles]
    for t in tiles:
        gates = gate_ref[0, :, lanes(t)]
        g_c = jnp.concatenate([gates[3 * r:3 * r + 1, :] for r in range(NSA_REP)], axis=1)
        g_s = jnp.concatenate([gates[3 * r + 1:3 * r + 2, :] for r in range(NSA_REP)], axis=1)
        g_w = jnp.concatenate([gates[3 * r + 2:3 * r + 3, :] for r in range(NSA_REP)], axis=1)
        out_t = g_c * oc_ref[t] + (g_w / den3_ref[t]) * acc_w[t] + (g_s / l_ref[t]) * acc_s[t]
        out_t = jnp.concatenate([out_t[:, r * Q_TILE:(r + 1) * Q_TILE] for r in range(NSA_REP)], axis=0)
        o_ref[lanes(t), :] = out_t.T.astype(o_ref.dtype)


def _nsa(q_t, cc, cc_t, ksel, kwin, vt, gates_t, ovl, batch, seq):
    span = NSA_TILES * Q_TILE
    nqb = seq // span
    ncb_pad = cc.shape[3]
    nsb = seq // SEL_BLOCK
    rows = NSA_REP * Q_TILE
    vspec = lambda which: pl.BlockSpec((1, 1, 1, HEAD_DIM, seq), lambda b, g, i: (which, b, g, 0, 0))
    per_tile = lambda shape, dtype: pltpu.VMEM((NSA_TILES,) + shape, dtype)
    return pl.pallas_call(
        _nsa_body,
        grid=(batch, NSA_GROUPS, nqb),
        in_specs=[
            pl.BlockSpec((1, NSA_REP, HEAD_DIM, span), lambda b, g, i: (b, g, 0, i)),
            pl.BlockSpec((1, 1, 1, ncb_pad, HEAD_DIM), lambda b, g, i: (0, b, g, 0, 0)),
            pl.BlockSpec((1, 1, 1, HEAD_DIM, ncb_pad), lambda b, g, i: (1, b, g, 0, 0)),
            pl.BlockSpec((1, 1, 1, seq, LANES), lambda b, g, i: (0, b, g, 0, 0)),
            pl.BlockSpec((1, 1, 1, seq, HEAD_DIM), lambda b, g, i: (0, b, g, 0, 0)),
            vspec(0), vspec(1),
            pl.BlockSpec((1, GATE_ROWS, span), lambda b, g, i: (g, 0, b * nqb + i)),
            pl.BlockSpec(ovl.shape, lambda b, g, i: (0, 0)),
        ],
        out_specs=pl.BlockSpec((span, NSA_REP * HEAD_DIM), lambda b, g, i: (b * nqb + i, g)),
        out_shape=jax.ShapeDtypeStruct((batch * seq, NSA_HEADS * HEAD_DIM), BF16),
        scratch_shapes=[per_tile((nsb, Q_TILE), F32), per_tile((nsb, Q_TILE), F32),
                        per_tile((2 * HEAD_DIM, rows), BF16),
                        per_tile((SEL_CHUNK, rows), F32), per_tile((SEL_CHUNK, rows), F32),
                        per_tile((SEL_CHUNK, rows), BF16),
                        per_tile((1, rows), F32), per_tile((1, rows), F32),
                        per_tile((HEAD_DIM, rows), F32), per_tile((HEAD_DIM, rows), F32),
                        per_tile((WINDOW + Q_TILE, rows), BF16), per_tile((1, rows), F32)],
        compiler_params=pltpu.CompilerParams(
            dimension_semantics=("parallel", "parallel", "arbitrary"), vmem_limit_bytes=VMEM_LIMIT),
        name="nsa",
    )(q_t, cc, cc_t, ksel[None], kwin[None], vt, vt, gates_t, ovl)


def _hgrn_body(hq_ref, hf_ref, hi_ref, hg_ref, lbl_ref, gain_ref, o_ref,
               state_ref, b_ref, k_ref, v_ref, q_ref, *, layer):
    @pl.when(pl.program_id(1) == 0)
    def _():
        state_ref[...] = jnp.zeros_like(state_ref)

    logits = lbl_ref[...]
    ex = jnp.exp(logits - jnp.max(logits, axis=0, keepdims=True))
    sm = ex / jnp.sum(ex, axis=0, keepdims=True)
    lb_all = jnp.zeros((1, HG_HEADS * HG_DK), F32)
    for i in range(1, layer + 1):
        lb_all = lb_all + sm[i:i + 1, :]

    ch, blk, half = HG_CHUNK, HG_SUB, HG_SUB // 2
    nblk = ch // blk
    ri = lax.broadcasted_iota(jnp.int32, (ch, ch), 0)
    ci = lax.broadcasted_iota(jnp.int32, (ch, ch), 1)
    tril = jnp.where(ri >= ci, 1.0, 0.0).astype(BF16)
    blk_row = lax.broadcasted_iota(jnp.int32, (blk, HG_DK), 0)
    low_rows = blk_row >= half
    rr = lax.broadcasted_iota(jnp.int32, (blk, blk), 0)
    cc = lax.broadcasted_iota(jnp.int32, (blk, blk), 1)
    cross = (rr >= half) & (cc < half)
    half_row = lax.broadcasted_iota(jnp.int32, (half, HG_DK), 0)
    gain = gain_ref[...]
    heads = range(HG_HEADS)

    def chunk(j, carry):
        r0 = pl.multiple_of(j * ch, ch)
        for h in heads:
            cols = slice(h * HG_DK, (h + 1) * HG_DK)
            lb = lb_all[:, cols]
            hq = hq_ref[0, pl.ds(r0, ch), cols]
            hf = hf_ref[0, pl.ds(r0, ch), cols]
            f = lb + (1.0 - lb) * jax.nn.sigmoid(hf)
            lf = _split_bf16(jnp.log(jnp.maximum(f, 1e-30)), 3)
            b_ref[h] = _dot(tril, lf[0]) + _dot(tril, lf[1]) + _dot(tril, lf[2])
            k_ref[h] = 1.0 - f
            v_ref[h] = hi_ref[0, pl.ds(r0, ch), cols]
            q_ref[h] = hq * jax.nn.sigmoid(hq)

        o_blocks, tables = [], []
        for h in heads:
            inter = _dot_nt((q_ref[h] * jnp.exp(b_ref[h])).astype(BF16), state_ref[h].astype(BF16))
            o_blocks.append([inter[ib * blk:(ib + 1) * blk, :] for ib in range(nblk)])
            tables.append([])
            for ib in range(nblk):
                i0 = ib * blk
                qi = q_ref[h, i0:i0 + blk, :]
                bi = b_ref[h, i0:i0 + blk, :]
                a = None
                if ib > 0:
                    ref_b = b_ref[h, i0 - 1:i0, :]
                    qd = (qi * jnp.exp(bi - ref_b)).astype(BF16)
                    kd = (k_ref[h, 0:i0, :] * jnp.exp(ref_b - b_ref[h, 0:i0, :])).astype(BF16)
                    a = _dot_nt(qd, kd).astype(BF16)
                mid_b = b_ref[h, i0 + half - 1:i0 + half, :]
                qd2 = (qi * jnp.exp(jnp.where(low_rows, bi - mid_b, 0.0))).astype(BF16)
                kd2 = (k_ref[h, i0:i0 + blk, :] * jnp.exp(jnp.where(low_rows, 0.0, mid_b - bi))).astype(BF16)
                a2 = jnp.where(cross, _dot_nt(qd2, kd2), 0.0).astype(BF16)
                tables[h].append((a, a2))
        for h in heads:
            bl = b_ref[h, ch - 1:ch, :]
            kd_end = (k_ref[h] * jnp.exp(bl - b_ref[h])).astype(BF16)
            vt = v_ref[h].T.astype(BF16)
            state_ref[h] = state_ref[h] * jnp.exp(bl) + _dot(vt, kd_end)

        for h in heads:
            for ib in range(nblk):
                i0 = ib * blk
                a, a2 = tables[h][ib]
                o_i = o_blocks[h][ib] + _dot(a2, v_ref[h, i0:i0 + blk, :].astype(BF16))
                if a is not None:
                    o_i = o_i + _dot(a, v_ref[h, 0:i0, :].astype(BF16))
                o_blocks[h][ib] = o_i

        nhalf = ch // half
        diag = [[jnp.zeros((half, HG_DV), F32) for _ in range(nhalf)] for _ in heads]
        for s in range(half):
            for h in heads:
                for ih in range(nhalf):
                    g0 = ih * half
                    bs = b_ref[h, g0 + s:g0 + s + 1, :]
                    ks = k_ref[h, g0 + s:g0 + s + 1, :]
                    vs = v_ref[h, g0 + s:g0 + s + 1, :]
                    bi = b_ref[h, g0:g0 + half, :]
                    qi = q_ref[h, g0:g0 + half, :]
                    dec = jnp.exp(jnp.where(half_row >= s, bi - bs, NEG))
                    a_s = jnp.sum(qi * dec * ks, axis=-1, keepdims=True)
                    diag[h][ih] = diag[h][ih] + a_s * vs

        for h in heads:
            cols = slice(h * HG_DK, (h + 1) * HG_DK)
            o = jnp.concatenate(o_blocks[h], axis=0) + jnp.concatenate(diag[h], axis=0)
            o = o * lax.rsqrt(jnp.mean(o * o, axis=-1, keepdims=True) + EPS) * gain
            hg = hg_ref[0, pl.ds(r0, ch), cols]
            o_ref[pl.ds(r0, ch), cols] = (o * (hg * jax.nn.sigmoid(hg))).astype(o_ref.dtype)
        return carry

    lax.fori_loop(0, hq_ref.shape[1] // ch, chunk, 0)


def _hgrn(h_all, lb_logits, out_gain, layer, batch, seq):
    t = h_all.shape[1]
    nblk = seq // HG_ROWS
    nl = lb_logits.shape[0]
    hw = HG_HEADS * HG_DK
    hspec = lambda which: pl.BlockSpec((1, HG_ROWS, hw), lambda b, c: (which, b * nblk + c, 0))
    return pl.pallas_call(
        functools.partial(_hgrn_body, layer=layer),
        grid=(batch, nblk),
        in_specs=[hspec(0), hspec(1), hspec(2), hspec(3),
                  pl.BlockSpec((nl, hw), lambda b, c: (0, 0)),
                  pl.BlockSpec((1, HG_DV), lambda b, c: (0, 0))],
        out_specs=pl.BlockSpec((HG_ROWS, HG_HEADS * HG_DV), lambda b, c: (b * nblk + c, 0)),
        out_shape=jax.ShapeDtypeStruct((t, HG_HEADS * HG_DV), BF16),
        scratch_shapes=[pltpu.VMEM((HG_HEADS, HG_DV, HG_DK), F32)]
        + [pltpu.VMEM((HG_HEADS, HG_CHUNK, HG_DK), F32)] * 4,
        compiler_params=pltpu.CompilerParams(
            dimension_semantics=("parallel", "arbitrary"), vmem_limit_bytes=VMEM_LIMIT),
        name="hgrn",
    )(h_all, h_all, h_all, h_all, lb_logits, out_gain.reshape(1, HG_DV))


def _relayout_w_in(w):
    d = w.shape[0]
    nq = NSA_HEADS * HEAD_DIM
    nkv = 6 * NSA_GROUPS * HEAD_DIM
    ng = 3 * NSA_HEADS
    q_kv = w[:, :nq + nkv]
    gts = w[:, nq + nkv:nq + nkv + ng]
    hg = w[:, nq + nkv + ng:]
    per = 3 * NSA_REP
    gate_blocks = [jnp.pad(gts[:, gi * per:(gi + 1) * per], ((0, 0), (0, LANES - per)))
                   for gi in range(NSA_GROUPS)]
    out = jnp.concatenate([q_kv, hg] + gate_blocks, axis=1).astype(BF16)
    assert out.shape == (d, _C_END)
    return out


def _compress_weights(pos, w1, w2):
    eye = jnp.eye(NSA_GROUPS, dtype=F32)
    e = w1.shape[-1]

    def big(w1_half):
        y = jnp.einsum("sldh,gk->slgdkh", w1_half, eye)
        return y.reshape(2, CMP_STRIDE * NSA_GROUPS * HEAD_DIM, NSA_GROUPS * e).astype(BF16)

    def pos_rows(p_half):
        y = jnp.broadcast_to(p_half[:, :, None, :], (2, CMP_STRIDE, NSA_GROUPS, HEAD_DIM))
        return y.reshape(2, 1, CMP_STRIDE * NSA_GROUPS * HEAD_DIM)

    w2b = jnp.einsum("shd,gk->sghkd", w2, eye).reshape(2, NSA_GROUPS * e, NSA_GROUPS * HEAD_DIM)
    return (big(w1[:, :CMP_STRIDE]), big(w1[:, CMP_STRIDE:]),
            pos_rows(pos[:, :CMP_STRIDE]), pos_rows(pos[:, CMP_STRIDE:]), w2b.astype(BF16))


def kernel(x, ffn1_norm, ffn1_w_gate, ffn1_w_up, ffn1_w_down, mix_norm, w_in, q_norm, k_norm,
           cmp_pos, cmp_w1, cmp_w2, hgrn_lb_logits, hgrn_out_norm, w_out,
           ffn2_norm, ffn2_w_gate, ffn2_w_up, ffn2_w_down):
    batch, seq, d = x.shape
    depth = w_in.shape[0]
    assert CMP_BLOCK == 2 * CMP_STRIDE and seq % SEL_CHUNK == 0 and seq >= WINDOW + Q_TILE
    assert seq % HG_ROWS == 0 and SEL_CHUNK % SEL_BLOCK == 0 and SEL_CHUNK % (NSA_TILES * Q_TILE) == 0
    t = batch * seq
    tm_proj = 256

    inv = 1.0 / (ROPE_THETA ** (jnp.arange(0, HEAD_DIM, 2, dtype=F32) / HEAD_DIM))
    ang = jnp.arange(seq, dtype=F32)[:, None] * inv[None, :]
    ang = jnp.concatenate([ang, ang], axis=-1)
    cos1, sin1 = jnp.cos(ang), jnp.sin(ang)
    cos2, sin2 = jnp.tile(cos1, (1, 2)), jnp.tile(sin1, (1, 2))
    cost, sint = cos1.T, sin1.T

    nseg = seq // CMP_STRIDE
    nsb = seq // SEL_BLOCK
    ci = np.arange(nseg)[None, :]
    sj = np.arange(nsb)[:, None]
    ovl = ((ci * CMP_STRIDE <= sj * SEL_BLOCK + SEL_BLOCK - 1)
           & (ci * CMP_STRIDE + CMP_BLOCK - 1 >= sj * SEL_BLOCK)
           & (ci < (seq - CMP_BLOCK) // CMP_STRIDE + 1))
    ovl = jnp.asarray(ovl, dtype=BF16)

    xt = x.reshape(t, d)
    for l in range(depth):
        xt = _ffn(xt, ffn1_norm[l], ffn1_w_gate[l], ffn1_w_up[l], ffn1_w_down[l])
        qgt = jnp.broadcast_to(q_norm[l][:, None], (HEAD_DIM, tm_proj))
        kg2 = jnp.tile(k_norm[l], (1, 2))
        q_t, cmp_in, ksel, kwin, vt, gates_t, h_all = _inproj(
            xt, mix_norm[l], _relayout_w_in(w_in[l]), cos2, sin2, cost, sint, qgt, kg2,
            batch, seq, tm=tm_proj)
        w1a, w1b, pa, pb, w2b = _compress_weights(cmp_pos[l], cmp_w1[l], cmp_w2[l])
        cc, cc_t = _compress(cmp_in, w1a, w1b, pa, pb, w2b, batch, seq)
        o_nsa = _nsa(q_t, cc, cc_t, ksel, kwin, vt, gates_t, ovl, batch, seq)
        o_hg = _hgrn(h_all, hgrn_lb_logits, hgrn_out_norm[l], l, batch, seq)
        nq = NSA_HEADS * HEAD_DIM
        wo = w_out[l]
        xt = _ffn(xt, ffn2_norm[l], ffn2_w_gate[l], ffn2_w_up[l], ffn2_w_down[l],
                  mix=(o_nsa, o_hg, wo[:nq], wo[nq:]))
    return xt.reshape(batch, seq, d)
```

```python
import functools

import jax
import jax.numpy as jnp
import numpy as np
from jax import lax
from jax.experimental import pallas as pl
from jax.experimental.pallas import tpu as pltpu

NSA_HEADS = 8
NSA_GROUPS = 2
NSA_REP = NSA_HEADS // NSA_GROUPS
HEAD_DIM = 64
CMP_BLOCK = 32
CMP_STRIDE = 16
CMP_HIDDEN = 128
SEL_BLOCK = 64
SEL_TOPN = 16
WINDOW = 512
HG_HEADS = 4
HG_DK = 128
HG_DV = 128
ROPE_THETA = 10000.0
EPS = 1e-6
NEG = -1e30
LOG2E = 1.4426950408889634

LANES = 128
Q_TILE = 128
NSA_TILES = 2
SEL_CHUNK = 512
HG_CHUNK = 64
HG_SUB = 16
HG_ROWS = 512
VMEM_LIMIT = 56 * 1024 * 1024

BF16 = jnp.bfloat16
F32 = jnp.float32


def _dot(a, b):
    return jnp.dot(a, b, preferred_element_type=F32)


def _dot_nt(a, b):
    return lax.dot_general(a, b, (((1,), (1,)), ((), ())), preferred_element_type=F32)


def _split_bf16(x, parts):
    out = []
    rem = x
    for _ in range(parts):
        p = rem.astype(BF16)
        out.append(p)
        rem = rem - p.astype(F32)
    return out


def _ffn_body(*refs, fuse_mix):
    if fuse_mix:
        (x_ref, ma_ref, mb_ref, woa_ref, wob_ref, g_ref, wg_ref, wu_ref, wd_ref,
         o_ref, xs_ref, hn_ref, acc_ref) = refs
    else:
        x_ref, g_ref, wg_ref, wu_ref, wd_ref, o_ref, xs_ref, hn_ref, acc_ref = refs
    f = pl.program_id(1)

    @pl.when(f == 0)
    def _():
        x = x_ref[...]
        if fuse_mix:
            x = x + _dot(ma_ref[...], woa_ref[0].astype(BF16)) + _dot(mb_ref[...], wob_ref[0].astype(BF16))
        xs_ref[...] = x
        ms = jnp.mean(x * x, axis=-1, keepdims=True)
        hn_ref[...] = (x * lax.rsqrt(ms + EPS) * g_ref[0]).astype(BF16)
        acc_ref[...] = jnp.zeros_like(acc_ref)

    hn = hn_ref[...]
    gate = _dot(hn, wg_ref[0].astype(BF16))
    up = _dot(hn, wu_ref[0].astype(BF16))
    act = (gate * jax.nn.sigmoid(gate) * up).astype(BF16)
    acc_ref[...] += _dot(act, wd_ref[0].astype(BF16))

    @pl.when(f == pl.num_programs(1) - 1)
    def _():
        o_ref[...] = xs_ref[...] + 0.5 * acc_ref[...]


def _ffn(x, g, wg, wu, wd, layer, mix=None, *, tm=1024, tf=256):
    t, d = x.shape
    ff = wg.shape[2]
    assert t % tm == 0 and ff % tf == 0
    fuse = mix is not None
    row = lambda i, f: (i, 0)
    in_specs = [pl.BlockSpec((tm, d), row)]
    args = [x]
    if fuse:
        ma, mb, wo = mix
        da, db = ma.shape[1], mb.shape[1]
        assert da == db and wo.shape[1] == da + db
        in_specs += [pl.BlockSpec((tm, da), row), pl.BlockSpec((tm, db), row),
                     pl.BlockSpec((1, da, d), lambda i, f: (layer, 0, 0)),
                     pl.BlockSpec((1, db, d), lambda i, f: (layer, 1, 0))]
        args += [ma, mb, wo, wo]
    in_specs += [pl.BlockSpec((1, 1, d), lambda i, f: (layer, 0, 0)),
                 pl.BlockSpec((1, d, tf), lambda i, f: (layer, 0, f)),
                 pl.BlockSpec((1, d, tf), lambda i, f: (layer, 0, f)),
                 pl.BlockSpec((1, tf, d), lambda i, f: (layer, f, 0))]
    args += [g.reshape(g.shape[0], 1, d), wg, wu, wd]
    return pl.pallas_call(
        functools.partial(_ffn_body, fuse_mix=fuse),
        grid=(t // tm, ff // tf),
        in_specs=in_specs,
        out_specs=pl.BlockSpec((tm, d), row),
        out_shape=jax.ShapeDtypeStruct((t, d), F32),
        scratch_shapes=[pltpu.VMEM((tm, d), F32), pltpu.VMEM((tm, d), BF16), pltpu.VMEM((tm, d), F32)],
        compiler_params=pltpu.CompilerParams(
            dimension_semantics=("parallel", "arbitrary"), vmem_limit_bytes=VMEM_LIMIT),
        name="ffn_mix" if fuse else "ffn",
    )(*args)


_C_Q = 0
_C_KV = NSA_HEADS * HEAD_DIM
_C_HG = _C_KV + 6 * LANES
_C_GATE = _C_HG + 4 * HG_HEADS * HG_DK
_C_END = _C_GATE + NSA_GROUPS * LANES
GATE_ROWS = 16
Q_SCALE = HEAD_DIM ** -0.5 * LOG2E


def _inproj_body(x_ref, g_ref, w_ref, cos_ref, sin_ref, cost_ref, sint_ref, qg_ref, kg_ref,
                 q_ref, cmp_ref, ksel_ref, kwin_ref, vt_ref, gate_ref, h_ref, *, tiles_per_seq):
    x = x_ref[...]
    ms = jnp.mean(x * x, axis=-1, keepdims=True)
    hn = (x * lax.rsqrt(ms + EPS) * g_ref[...]).astype(BF16)
    cos = cos_ref[...]
    sin = sin_ref[...]
    tm = x.shape[0]
    half = HEAD_DIM // 2

    def proj(c0, width):
        return _dot(hn, w_ref[:, c0:c0 + width])

    ri = (lax.broadcasted_iota(jnp.int32, (2 * LANES, LANES), 0) % LANES) // HEAD_DIM
    ci = lax.broadcasted_iota(jnp.int32, (2 * LANES, LANES), 1) // HEAD_DIM
    head_mean = jnp.where(ri == ci, 1.0 / HEAD_DIM, 0.0).astype(BF16)
    lane = lax.broadcasted_iota(jnp.int32, (tm, LANES), 1)
    first_half = (lane % HEAD_DIM) < half

    def norm_rope(y, gain):
        msq = _dot(jnp.concatenate(_split_bf16(y * y, 2), axis=1), head_mean)
        yn = y * lax.rsqrt(msq + EPS) * gain
        rot = jnp.where(first_half, -pltpu.roll(yn, LANES - half, 1), pltpu.roll(yn, half, 1))
        return yn * cos + rot * sin

    def heads_t(y):
        return y.T.reshape(2, HEAD_DIM, tm)

    qgain = (qg_ref[...] * Q_SCALE)[None]
    cost = cost_ref[...][None]
    sint = sint_ref[...][None]
    yq = proj(_C_Q, NSA_HEADS * HEAD_DIM)
    ykv = proj(_C_KV, 6 * LANES)
    kv = lambda j: ykv[:, j * LANES:(j + 1) * LANES]
    for c in range(NSA_HEADS // 2):
        y3 = heads_t(yq[:, c * LANES:(c + 1) * LANES])
        yn = y3 * lax.rsqrt(jnp.mean(y3 * y3, axis=1, keepdims=True) + EPS) * qgain
        rot = jnp.concatenate([-yn[:, half:], yn[:, :half]], axis=1)
        q_ref[0, 2 * c:2 * c + 2] = (yn * cost + rot * sint).astype(BF16)

    kc = norm_rope(kv(0), kg_ref[0:1, :])
    cmp_ref[0] = kc.astype(BF16)
    cmp_ref[1] = kv(1).astype(BF16)
    ksel = norm_rope(kv(2), kg_ref[1:2, :])
    pos = (pl.program_id(0) % tiles_per_seq) * tm + lax.broadcasted_iota(jnp.int32, (tm, HEAD_DIM), 0)
    blk_in_chunk = (pos // SEL_BLOCK) % (SEL_CHUNK // SEL_BLOCK)
    onehot = jnp.where(lax.broadcasted_iota(jnp.int32, (tm, HEAD_DIM), 1) == blk_in_chunk, 1.0, 0.0)
    for gi in range(NSA_GROUPS):
        ksel_ref[0, gi] = jnp.concatenate(
            [ksel[:, gi * HEAD_DIM:(gi + 1) * HEAD_DIM], onehot], axis=1).astype(BF16)
    kwin = norm_rope(kv(4), kg_ref[2:3, :]).astype(BF16)
    kwin_ref[0, 0] = kwin[:, :HEAD_DIM]
    kwin_ref[0, 1] = kwin[:, HEAD_DIM:]
    for which, col in ((0, 3), (1, 5)):
        vt_ref[which, 0] = heads_t(kv(col)).astype(BF16)

    hw = HG_HEADS * HG_DK
    for i in range(4):
        h_ref[i] = proj(_C_HG + i * hw, hw)
    ygate = proj(_C_GATE, NSA_GROUPS * LANES)
    for gi in range(NSA_GROUPS):
        gt = jax.nn.sigmoid(ygate[:, gi * LANES:(gi + 1) * LANES]).T
        gate_ref[gi] = gt[:GATE_ROWS]


def _inproj(x, g, w2, cos2, sin2, cost, sint, qgt, kg2, batch, seq, *, tm=256):
    t, d = x.shape
    nst = seq // tm
    hw = HG_HEADS * HG_DK
    out_shape = (
        jax.ShapeDtypeStruct((batch, NSA_HEADS, HEAD_DIM, seq), BF16),
        jax.ShapeDtypeStruct((2, t, LANES), BF16),
        jax.ShapeDtypeStruct((batch, NSA_GROUPS, seq, LANES), BF16),
        jax.ShapeDtypeStruct((batch, NSA_GROUPS, seq, HEAD_DIM), BF16),
        jax.ShapeDtypeStruct((2, batch, NSA_GROUPS, HEAD_DIM, seq), BF16),
        jax.ShapeDtypeStruct((NSA_GROUPS, GATE_ROWS, t), F32),
        jax.ShapeDtypeStruct((4, t, hw), F32),
    )
    out_specs = (
        pl.BlockSpec((1, NSA_HEADS, HEAD_DIM, tm), lambda i: (i // nst, 0, 0, i % nst)),
        pl.BlockSpec((2, tm, LANES), lambda i: (0, i, 0)),
        pl.BlockSpec((1, NSA_GROUPS, tm, LANES), lambda i: (i // nst, 0, i % nst, 0)),
        pl.BlockSpec((1, NSA_GROUPS, tm, HEAD_DIM), lambda i: (i // nst, 0, i % nst, 0)),
        pl.BlockSpec((2, 1, NSA_GROUPS, HEAD_DIM, tm), lambda i: (0, i // nst, 0, 0, i % nst)),
        pl.BlockSpec((NSA_GROUPS, GATE_ROWS, tm), lambda i: (0, 0, i)),
        pl.BlockSpec((4, tm, hw), lambda i: (0, i, 0)),
    )
    in_specs = [
        pl.BlockSpec((tm, d), lambda i: (i, 0)),
        pl.BlockSpec((1, d), lambda i: (0, 0)),
        pl.BlockSpec(w2.shape, lambda i: (0, 0)),
        pl.BlockSpec((tm, LANES), lambda i: (i % nst, 0)),
        pl.BlockSpec((tm, LANES), lambda i: (i % nst, 0)),
        pl.BlockSpec((HEAD_DIM, tm), lambda i: (0, i % nst)),
        pl.BlockSpec((HEAD_DIM, tm), lambda i: (0, i % nst)),
        pl.BlockSpec((HEAD_DIM, tm), lambda i: (0, 0)),
        pl.BlockSpec((3, LANES), lambda i: (0, 0)),
    ]
    return pl.pallas_call(
        functools.partial(_inproj_body, tiles_per_seq=nst),
        grid=(t // tm,),
        in_specs=in_specs,
        out_specs=out_specs,
        out_shape=out_shape,
        compiler_params=pltpu.CompilerParams(
            dimension_semantics=("parallel",), vmem_limit_bytes=VMEM_LIMIT),
        name="inproj",
    )(x, g.reshape(1, d), w2, cos2, sin2, cost, sint, qgt, kg2)


def _compress_body(seg_ref, w1a_ref, w1b_ref, pa_ref, pb_ref, w2_ref, o_ref, ot_ref):
    segs = seg_ref[0, 0]
    w1a = w1a_ref[0]
    w1b = w1b_ref[0]
    first = _dot(segs, w1a)
    second = _dot(segs, w1b)
    pa_hi, pa_lo = _split_bf16(jnp.broadcast_to(pa_ref[0], (8, pa_ref.shape[2])), 2)
    pb_hi, pb_lo = _split_bf16(jnp.broadcast_to(pb_ref[0], (8, pb_ref.shape[2])), 2)
    cpos = _dot(pa_hi, w1a) + _dot(pa_lo, w1a) + _dot(pb_hi, w1b) + _dot(pb_lo, w1b)
    nseg = first.shape[0]
    pre = first + pltpu.roll(second, nseg - 1, 0) + cpos[0:1, :]
    hid = jax.nn.gelu(pre, approximate=True).astype(BF16)
    out = _dot(hid, w2_ref[0])
    outb = out.astype(BF16)
    o_ref[0, 0, 0] = outb[:, :HEAD_DIM]
    o_ref[0, 0, 1] = outb[:, HEAD_DIM:]
    ot_ref[0, 0] = out.T.reshape(NSA_GROUPS, HEAD_DIM, nseg).astype(BF16)


def _compress(cmp_in, w1a, w1b, pa, pb, w2, batch, seq):
    nseg = seq // CMP_STRIDE
    width = CMP_STRIDE * LANES
    segs = cmp_in.reshape(2, batch, nseg, width)
    return pl.pallas_call(
        _compress_body,
        grid=(2, batch),
        in_specs=[
            pl.BlockSpec((1, 1, nseg, width), lambda s, b: (s, b, 0, 0)),
            pl.BlockSpec((1,) + w1a.shape[1:], lambda s, b: (s, 0, 0)),
            pl.BlockSpec((1,) + w1b.shape[1:], lambda s, b: (s, 0, 0)),
            pl.BlockSpec((1, 1, width), lambda s, b: (s, 0, 0)),
            pl.BlockSpec((1, 1, width), lambda s, b: (s, 0, 0)),
            pl.BlockSpec((1,) + w2.shape[1:], lambda s, b: (s, 0, 0)),
        ],
        out_specs=(pl.BlockSpec((1, 1, NSA_GROUPS, nseg, HEAD_DIM), lambda s, b: (s, b, 0, 0, 0)),
                   pl.BlockSpec((1, 1, NSA_GROUPS, HEAD_DIM, nseg), lambda s, b: (s, b, 0, 0, 0))),
        out_shape=(jax.ShapeDtypeStruct((2, batch, NSA_GROUPS, nseg, HEAD_DIM), BF16),
                   jax.ShapeDtypeStruct((2, batch, NSA_GROUPS, HEAD_DIM, nseg), BF16)),
        compiler_params=pltpu.CompilerParams(
            dimension_semantics=("parallel", "parallel"), vmem_limit_bytes=VMEM_LIMIT),
        name="compress",
    )(segs, w1a, w1b, pa, pb, w2)


def _tile_heads(a):
    return jnp.concatenate([a] * NSA_REP, axis=1)


def _nsa_body(q_ref, kcc_ref, vcct_ref, ks_ref, kw_ref, vst_ref, vwt_ref, gate_ref, ovl_ref,
              o_ref, imp_ref, selb_ref, qa_ref, sa_ref, sb_ref, p_ref, m_ref, l_ref, acc_ref,
              oc_ref, p3_ref, den3_ref):
    tiles = range(NSA_TILES)
    first = pl.program_id(2) * (NSA_TILES * Q_TILE)
    starts = [first + t * Q_TILE for t in tiles]
    lanes = lambda t: slice(t * Q_TILE, (t + 1) * Q_TILE)
    q_t = [jnp.concatenate([q_ref[0, r, :, lanes(t)] for r in range(NSA_REP)], axis=1) for t in tiles]
    tq = [starts[t] + lax.broadcasted_iota(jnp.int32, (1, Q_TILE), 1) for t in tiles]

    kcc = kcc_ref[0, 0, 0]
    ncb = kcc.shape[0]
    wkeys = WINDOW + Q_TILE
    w0 = [pl.multiple_of(jnp.maximum(starts[t] - WINDOW, 0), Q_TILE) for t in tiles]
    s1 = [_dot(kcc, q_t[t]) for t in tiles]
    s3 = [_dot(kw_ref[0, 0, 0, pl.ds(w0[t], wkeys), :], q_t[t]) for t in tiles]

    cend = lax.broadcasted_iota(jnp.int32, (ncb, 1), 0) * CMP_STRIDE + (CMP_BLOCK - 1)
    p1 = []
    for t in tiles:
        sm1 = s1[t] + _tile_heads(jnp.where(cend <= tq[t], 0.0, NEG))
        e1 = jnp.exp2(sm1 - jnp.max(sm1, axis=0, keepdims=True))
        den1 = jnp.sum(e1, axis=0, keepdims=True)
        p1.append(e1 * jnp.where(_tile_heads(tq[t] >= CMP_BLOCK - 1), 1.0 / den1, 0.0))
    ovl = ovl_ref[...]
    nsb = ovl.shape[0]
    imp = []
    for t in tiles:
        oc_ref[t] = _dot(vcct_ref[0, 0, 0], p1[t].astype(BF16))
        p1sum = p1[t][:, 0:Q_TILE]
        for r in range(1, NSA_REP):
            p1sum = p1sum + p1[t][:, r * Q_TILE:(r + 1) * Q_TILE]
        ps_hi, ps_lo = _split_bf16(p1sum, 2)
        imp.append(_dot(ovl, ps_hi) + _dot(ovl, ps_lo))

    jb = lax.broadcasted_iota(jnp.int32, (nsb, Q_TILE), 0)
    valid, impm = [], []
    for t in tiles:
        cur = tq[t] // SEL_BLOCK
        forced = (jb == 0) | (jb == cur) | (jb == cur - 1)
        valid.append(jb * SEL_BLOCK <= tq[t])
        impm.append(jnp.where(valid[t], jnp.where(forced, jnp.inf, imp[t]), -jnp.inf))
        imp_ref[t] = impm[t]
    sub = lax.broadcasted_iota(jnp.int32, (8, Q_TILE), 0)
    ranks = [[jnp.zeros((8, Q_TILE), F32) for _ in range(nsb // 8)] for _ in tiles]
    for i in range(nsb):
        for t in tiles:
            xi = jnp.broadcast_to(imp_ref[t, i:i + 1, :], (8, Q_TILE))
            for v in range(nsb // 8):
                xv = impm[t][8 * v:8 * v + 8, :]
                ge = jnp.where(xi >= xv, 1.0, 0.0)
                gt = jnp.where(xi > xv, 1.0, 0.0)
                if 8 * v > i:
                    inc = ge
                elif 8 * v + 7 < i:
                    inc = gt
                else:
                    inc = jnp.where(sub + 8 * v > i, ge, gt)
                ranks[t][v] = ranks[t][v] + inc
    for t in tiles:
        rank = jnp.concatenate(ranks[t], axis=0)
        selb_ref[t] = jnp.where((rank < SEL_TOPN) & valid[t], 0.0, NEG)

    blocks_per_chunk = SEL_CHUNK // SEL_BLOCK
    last_chunk = ks_ref.shape[3] // SEL_CHUNK - 1
    n_chunks = (first + NSA_TILES * Q_TILE + SEL_CHUNK - 1) // SEL_CHUNK
    own_row = [pl.multiple_of(starts[t] - (n_chunks - 1) * SEL_CHUNK, Q_TILE) for t in tiles]
    own_mask = _tile_heads(jnp.where(
        lax.broadcasted_iota(jnp.int32, (Q_TILE, Q_TILE), 0) <= lax.broadcasted_iota(jnp.int32, (Q_TILE, Q_TILE), 1),
        0.0, 1.0))
    bias_pad = jnp.zeros((HEAD_DIM - blocks_per_chunk, NSA_REP * Q_TILE), F32)

    def scores(t, chunk):
        blk_bias = _tile_heads(selb_ref[t, pl.ds(pl.multiple_of(chunk * blocks_per_chunk, blocks_per_chunk),
                                                 blocks_per_chunk), :])
        qa_ref[t, HEAD_DIM:, :] = jnp.concatenate([blk_bias, bias_pad], axis=0).astype(BF16)
        k0 = pl.multiple_of(chunk * SEL_CHUNK, SEL_CHUNK)
        return _dot(ks_ref[0, 0, 0, pl.ds(k0, SEL_CHUNK), :], qa_ref[t])

    def put_scores(t, chunk, s_ref):
        s_ref[t] = scores(t, chunk)
        own_neg = jnp.where(chunk == n_chunks - 1, NEG, 0.0)
        s_ref[t, pl.ds(own_row[t], Q_TILE), :] = s_ref[t, pl.ds(own_row[t], Q_TILE), :] + own_mask * own_neg

    def sel_step(c, s_cur_ref, s_next_ref):
        m_old, m_new = [], []
        for t in tiles:
            m_old.append(m_ref[t])
            m_new.append(jnp.maximum(m_old[t], jnp.max(s_cur_ref[t], axis=0, keepdims=True)))
        nxt = jnp.minimum(c + 1, last_chunk)
        for t in tiles:
            put_scores(t, nxt, s_next_ref)
        v0 = pl.multiple_of(jnp.maximum(c - 1, 0) * SEL_CHUNK, SEL_CHUNK)
        v_prev = vst_ref[0, 0, 0, :, pl.ds(v0, SEL_CHUNK)]
        pv = [_dot(v_prev, p_ref[t]) for t in tiles]
        for t in tiles:
            p = jnp.exp2(s_cur_ref[t] - m_new[t])
            alpha = jnp.exp2(m_old[t] - m_new[t])
            l_ref[t] = alpha * l_ref[t] + jnp.sum(p, axis=0, keepdims=True)
            acc_ref[t] = alpha * (acc_ref[t] + pv[t])
            m_ref[t] = m_new[t]
            p_ref[t] = p.astype(BF16)

    for t in tiles:
        qa_ref[t, 0:HEAD_DIM, :] = q_t[t]
        put_scores(t, 0, sa_ref)
    p_ref[...] = jnp.zeros_like(p_ref)

    for t in tiles:
        dist = tq[t] - (w0[t] + lax.broadcasted_iota(jnp.int32, (wkeys, 1), 0))
        sm3 = s3[t] + _tile_heads(jnp.where((dist >= 0) & (dist < WINDOW), 0.0, NEG))
        e3 = jnp.exp2(sm3 - jnp.max(sm3, axis=0, keepdims=True))
        den3_ref[t] = jnp.sum(e3, axis=0, keepdims=True)
        p3_ref[t] = e3.astype(BF16)

    m_ref[...] = jnp.full(m_ref.shape, NEG, F32)
    l_ref[...] = jnp.zeros_like(l_ref)
    acc_ref[...] = jnp.zeros_like(acc_ref)

    def sel_pair(i, carry):
        sel_step(2 * i, sa_ref, sb_ref)

        @pl.when(2 * i + 1 < n_chunks)
        def _():
            sel_step(2 * i + 1, sb_ref, sa_ref)
        return carry

    lax.fori_loop(0, (n_chunks + 1) // 2, sel_pair, 0)

    vl = pl.multiple_of((n_chunks - 1) * SEL_CHUNK, SEL_CHUNK)
    v_last = vst_ref[0, 0, 0, :, pl.ds(vl, SEL_CHUNK)]
    acc_w = [_dot(vwt_ref[0, 0, 0, :, pl.ds(w0[t], wkeys)], p3_ref[t]) for t in tiles]
    acc_s = [acc_ref[t] + _dot(v_last, p_ref[t]) for t in tiles]
    for t in tiles:
        gates = gate_ref[0, :, lanes(t)]
        g_c = jnp.concatenate([gates[3 * r:3 * r + 1, :] for r in range(NSA_REP)], axis=1)
        g_s = jnp.concatenate([gates[3 * r + 1:3 * r + 2, :] for r in range(NSA_REP)], axis=1)
        g_w = jnp.concatenate([gates[3 * r + 2:3 * r + 3, :] for r in range(NSA_REP)], axis=1)
        out_t = g_c * oc_ref[t] + (g_w / den3_ref[t]) * acc_w[t] + (g_s / l_ref[t]) * acc_s[t]
        out_t = jnp.concatenate([out_t[:, r * Q_TILE:(r + 1) * Q_TILE] for r in range(NSA_REP)], axis=0)
        o_ref[lanes(t), :] = out_t.T.astype(o_ref.dtype)


def _nsa(q_t, cc, cc_t, ksel, kwin, vt, gates_t, ovl, batch, seq):
    span = NSA_TILES * Q_TILE
    nqb = seq // span
    ncb_pad = cc.shape[3]
    nsb = seq // SEL_BLOCK
    rows = NSA_REP * Q_TILE
    vspec = lambda which: pl.BlockSpec((1, 1, 1, HEAD_DIM, seq), lambda b, g, i: (which, b, g, 0, 0))
    per_tile = lambda shape, dtype: pltpu.VMEM((NSA_TILES,) + shape, dtype)
    return pl.pallas_call(
        _nsa_body,
        grid=(batch, NSA_GROUPS, nqb),
        in_specs=[
            pl.BlockSpec((1, NSA_REP, HEAD_DIM, span), lambda b, g, i: (b, g, 0, i)),
            pl.BlockSpec((1, 1, 1, ncb_pad, HEAD_DIM), lambda b, g, i: (0, b, g, 0, 0)),
            pl.BlockSpec((1, 1, 1, HEAD_DIM, ncb_pad), lambda b, g, i: (1, b, g, 0, 0)),
            pl.BlockSpec((1, 1, 1, seq, LANES), lambda b, g, i: (0, b, g, 0, 0)),
            pl.BlockSpec((1, 1, 1, seq, HEAD_DIM), lambda b, g, i: (0, b, g, 0, 0)),
            vspec(0), vspec(1),
            pl.BlockSpec((1, GATE_ROWS, span), lambda b, g, i: (g, 0, b * nqb + i)),
            pl.BlockSpec(ovl.shape, lambda b, g, i: (0, 0)),
        ],
        out_specs=pl.BlockSpec((span, NSA_REP * HEAD_DIM), lambda b, g, i: (b * nqb + i, g)),
        out_shape=jax.ShapeDtypeStruct((batch * seq, NSA_HEADS * HEAD_DIM), BF16),
        scratch_shapes=[per_tile((nsb, Q_TILE), F32), per_tile((nsb, Q_TILE), F32),
                        per_tile((2 * HEAD_DIM, rows), BF16),
                        per_tile((SEL_CHUNK, rows), F32), per_tile((SEL_CHUNK, rows), F32),
                        per_tile((SEL_CHUNK, rows), BF16),
                        per_tile((1, rows), F32), per_tile((1, rows), F32),
                        per_tile((HEAD_DIM, rows), F32), per_tile((HEAD_DIM, rows), F32),
                        per_tile((WINDOW + Q_TILE, rows), BF16), per_tile((1, rows), F32)],
        compiler_params=pltpu.CompilerParams(
            dimension_semantics=("parallel", "parallel", "arbitrary"), vmem_limit_bytes=VMEM_LIMIT),
        name="nsa",
    )(q_t, cc, cc_t, ksel[None], kwin[None], vt, vt, gates_t, ovl)


def _hgrn_body(hq_ref, hf_ref, hi_ref, hg_ref, lbl_ref, gain_ref, o_ref,
               state_ref, b_ref, k_ref, v_ref, q_ref, *, layer):
    @pl.when(pl.program_id(1) == 0)
    def _():
        state_ref[...] = jnp.zeros_like(state_ref)

    logits = lbl_ref[...]
    ex = jnp.exp(logits - jnp.max(logits, axis=0, keepdims=True))
    sm = ex / jnp.sum(ex, axis=0, keepdims=True)
    lb_all = jnp.zeros((1, HG_HEADS * HG_DK), F32)
    for i in range(1, layer + 1):
        lb_all = lb_all + sm[i:i + 1, :]

    ch, blk, half = HG_CHUNK, HG_SUB, HG_SUB // 2
    nblk = ch // blk
    ri = lax.broadcasted_iota(jnp.int32, (ch, ch), 0)
    ci = lax.broadcasted_iota(jnp.int32, (ch, ch), 1)
    tril = jnp.where(ri >= ci, 1.0, 0.0).astype(BF16)
    blk_row = lax.broadcasted_iota(jnp.int32, (blk, HG_DK), 0)
    low_rows = blk_row >= half
    rr = lax.broadcasted_iota(jnp.int32, (blk, blk), 0)
    cc = lax.broadcasted_iota(jnp.int32, (blk, blk), 1)
    cross = (rr >= half) & (cc < half)
    half_row = lax.broadcasted_iota(jnp.int32, (half, HG_DK), 0)
    gain = gain_ref[...]
    heads = range(HG_HEADS)

    def chunk(j, carry):
        r0 = pl.multiple_of(j * ch, ch)
        for h in heads:
            cols = slice(h * HG_DK, (h + 1) * HG_DK)
            lb = lb_all[:, cols]
            hq = hq_ref[0, pl.ds(r0, ch), cols]
            hf = hf_ref[0, pl.ds(r0, ch), cols]
            f = lb + (1.0 - lb) * jax.nn.sigmoid(hf)
            lf = _split_bf16(jnp.log(jnp.maximum(f, 1e-30)), 3)
            b_ref[h] = _dot(tril, lf[0]) + _dot(tril, lf[1]) + _dot(tril, lf[2])
            k_ref[h] = 1.0 - f
            v_ref[h] = hi_ref[0, pl.ds(r0, ch), cols]
            q_ref[h] = hq * jax.nn.sigmoid(hq)

        o_blocks, tables = [], []
        for h in heads:
            inter = _dot_nt((q_ref[h] * jnp.exp(b_ref[h])).astype(BF16), state_ref[h].astype(BF16))
            o_blocks.append([inter[ib * blk:(ib + 1) * blk, :] for ib in range(nblk)])
            tables.append([])
            for ib in range(nblk):
                i0 = ib * blk
                qi = q_ref[h, i0:i0 + blk, :]
                bi = b_ref[h, i0:i0 + blk, :]
                a = None
                if ib > 0:
                    ref_b = b_ref[h, i0 - 1:i0, :]
                    qd = (qi * jnp.exp(bi - ref_b)).astype(BF16)
                    kd = (k_ref[h, 0:i0, :] * jnp.exp(ref_b - b_ref[h, 0:i0, :])).astype(BF16)
                    a = _dot_nt(qd, kd).astype(BF16)
                mid_b = b_ref[h, i0 + half - 1:i0 + half, :]
                qd2 = (qi * jnp.exp(jnp.where(low_rows, bi - mid_b, 0.0))).astype(BF16)
                kd2 = (k_ref[h, i0:i0 + blk, :] * jnp.exp(jnp.where(low_rows, 0.0, mid_b - bi))).astype(BF16)
                a2 = jnp.where(cross, _dot_nt(qd2, kd2), 0.0).astype(BF16)
                tables[h].append((a, a2))
        for h in heads:
            bl = b_ref[h, ch - 1:ch, :]
            kd_end = (k_ref[h] * jnp.exp(bl - b_ref[h])).astype(BF16)
            vt = v_ref[h].T.astype(BF16)
            state_ref[h] = state_ref[h] * jnp.exp(bl) + _dot(vt, kd_end)

        for h in heads:
            for ib in range(nblk):
                i0 = ib * blk
                a, a2 = tables[h][ib]
                o_i = o_blocks[h][ib] + _dot(a2, v_ref[h, i0:i0 + blk, :].astype(BF16))
                if a is not None:
                    o_i = o_i + _dot(a, v_ref[h, 0:i0, :].astype(BF16))
                o_blocks[h][ib] = o_i

        nhalf = ch // half
        diag = [[jnp.zeros((half, HG_DV), F32) for _ in range(nhalf)] for _ in heads]
        for s in range(half):
            for h in heads:
                for ih in range(nhalf):
                    g0 = ih * half
                    bs = b_ref[h, g0 + s:g0 + s + 1, :]
                    ks = k_ref[h, g0 + s:g0 + s + 1, :]
                    vs = v_ref[h, g0 + s:g0 + s + 1, :]
                    bi = b_ref[h, g0:g0 + half, :]
                    qi = q_ref[h, g0:g0 + half, :]
                    dec = jnp.exp(jnp.where(half_row >= s, bi - bs, NEG))
                    a_s = jnp.sum(qi * dec * ks, axis=-1, keepdims=True)
                    diag[h][ih] = diag[h][ih] + a_s * vs

        for h in heads:
            cols = slice(h * HG_DK, (h + 1) * HG_DK)
            o = jnp.concatenate(o_blocks[h], axis=0) + jnp.concatenate(diag[h], axis=0)
            o = o * lax.rsqrt(jnp.mean(o * o, axis=-1, keepdims=True) + EPS) * gain
            hg = hg_ref[0, pl.ds(r0, ch), cols]
            o_ref[pl.ds(r0, ch), cols] = (o * (hg * jax.nn.sigmoid(hg))).astype(o_ref.dtype)
        return carry

    lax.fori_loop(0, hq_ref.shape[1] // ch, chunk, 0)


def _hgrn(h_all, lb_logits, out_gain, layer, batch, seq):
    t = h_all.shape[1]
    nblk = seq // HG_ROWS
    nl = lb_logits.shape[0]
    hw = HG_HEADS * HG_DK
    hspec = lambda which: pl.BlockSpec((1, HG_ROWS, hw), lambda b, c: (which, b * nblk + c, 0))
    return pl.pallas_call(
        functools.partial(_hgrn_body, layer=layer),
        grid=(batch, nblk),
        in_specs=[hspec(0), hspec(1), hspec(2), hspec(3),
                  pl.BlockSpec((nl, hw), lambda b, c: (0, 0)),
                  pl.BlockSpec((1, HG_DV), lambda b, c: (0, 0))],
        out_specs=pl.BlockSpec((HG_ROWS, HG_HEADS * HG_DV), lambda b, c: (b * nblk + c, 0)),
        out_shape=jax.ShapeDtypeStruct((t, HG_HEADS * HG_DV), BF16),
        scratch_shapes=[pltpu.VMEM((HG_HEADS, HG_DV, HG_DK), F32)]
        + [pltpu.VMEM((HG_HEADS, HG_CHUNK, HG_DK), F32)] * 4,
        compiler_params=pltpu.CompilerParams(
            dimension_semantics=("parallel", "arbitrary"), vmem_limit_bytes=VMEM_LIMIT),
        name="hgrn",
    )(h_all, h_all, h_all, h_all, lb_logits, out_gain.reshape(1, HG_DV))


def _relayout_w_in(w):
    d = w.shape[0]
    nq = NSA_HEADS * HEAD_DIM
    nkv = 6 * NSA_GROUPS * HEAD_DIM
    ng = 3 * NSA_HEADS
    q_kv = w[:, :nq + nkv]
    gts = w[:, nq + nkv:nq + nkv + ng]
    hg = w[:, nq + nkv + ng:]
    per = 3 * NSA_REP
    gate_blocks = [jnp.pad(gts[:, gi * per:(gi + 1) * per], ((0, 0), (0, LANES - per)))
                   for gi in range(NSA_GROUPS)]
    out = jnp.concatenate([q_kv, hg] + gate_blocks, axis=1).astype(BF16)
    assert out.shape == (d, _C_END)
    return out


def _compress_weights(pos, w1, w2):
    eye = jnp.eye(NSA_GROUPS, dtype=F32)
    e = w1.shape[-1]

    def big(w1_half):
        y = jnp.einsum("sldh,gk->slgdkh", w1_half, eye)
        return y.reshape(2, CMP_STRIDE * NSA_GROUPS * HEAD_DIM, NSA_GROUPS * e).astype(BF16)

    def pos_rows(p_half):
        y = jnp.broadcast_to(p_half[:, :, None, :], (2, CMP_STRIDE, NSA_GROUPS, HEAD_DIM))
        return y.reshape(2, 1, CMP_STRIDE * NSA_GROUPS * HEAD_DIM)

    w2b = jnp.einsum("shd,gk->sghkd", w2, eye).reshape(2, NSA_GROUPS * e, NSA_GROUPS * HEAD_DIM)
    return (big(w1[:, :CMP_STRIDE]), big(w1[:, CMP_STRIDE:]),
            pos_rows(pos[:, :CMP_STRIDE]), pos_rows(pos[:, CMP_STRIDE:]), w2b.astype(BF16))


def kernel(x, ffn1_norm, ffn1_w_gate, ffn1_w_up, ffn1_w_down, mix_norm, w_in, q_norm, k_norm,
           cmp_pos, cmp_w1, cmp_w2, hgrn_lb_logits, hgrn_out_norm, w_out,
           ffn2_norm, ffn2_w_gate, ffn2_w_up, ffn2_w_down):
    batch, seq, d = x.shape
    depth = w_in.shape[0]
    assert CMP_BLOCK == 2 * CMP_STRIDE and seq % SEL_CHUNK == 0 and seq >= WINDOW + Q_TILE
    assert seq % HG_ROWS == 0 and SEL_CHUNK % SEL_BLOCK == 0 and SEL_CHUNK % (NSA_TILES * Q_TILE) == 0
    t = batch * seq
    tm_proj = 256

    inv = 1.0 / (ROPE_THETA ** (jnp.arange(0, HEAD_DIM, 2, dtype=F32) / HEAD_DIM))
    ang = jnp.arange(seq, dtype=F32)[:, None] * inv[None, :]
    ang = jnp.concatenate([ang, ang], axis=-1)
    cos1, sin1 = jnp.cos(ang), jnp.sin(ang)
    cos2, sin2 = jnp.tile(cos1, (1, 2)), jnp.tile(sin1, (1, 2))
    cost, sint = cos1.T, sin1.T

    nseg = seq // CMP_STRIDE
    nsb = seq // SEL_BLOCK
    ci = np.arange(nseg)[None, :]
    sj = np.arange(nsb)[:, None]
    ovl = ((ci * CMP_STRIDE <= sj * SEL_BLOCK + SEL_BLOCK - 1)
           & (ci * CMP_STRIDE + CMP_BLOCK - 1 >= sj * SEL_BLOCK)
           & (ci < (seq - CMP_BLOCK) // CMP_STRIDE + 1))
    ovl = jnp.asarray(ovl, dtype=BF16)

    xt = x.reshape(t, d)
    for l in range(depth):
        xt = _ffn(xt, ffn1_norm, ffn1_w_gate, ffn1_w_up, ffn1_w_down, l)
        qgt = jnp.broadcast_to(q_norm[l][:, None], (HEAD_DIM, tm_proj))
        kg2 = jnp.tile(k_norm[l], (1, 2))
        q_t, cmp_in, ksel, kwin, vt, gates_t, h_all = _inproj(
            xt, mix_norm[l], _relayout_w_in(w_in[l]), cos2, sin2, cost, sint, qgt, kg2,
            batch, seq, tm=tm_proj)
        w1a, w1b, pa, pb, w2b = _compress_weights(cmp_pos[l], cmp_w1[l], cmp_w2[l])
        cc, cc_t = _compress(cmp_in, w1a, w1b, pa, pb, w2b, batch, seq)
        o_nsa = _nsa(q_t, cc, cc_t, ksel, kwin, vt, gates_t, ovl, batch, seq)
        o_hg = _hgrn(h_all, hgrn_lb_logits, hgrn_out_norm[l], l, batch, seq)
        xt = _ffn(xt, ffn2_norm, ffn2_w_gate, ffn2_w_up, ffn2_w_down, l, mix=(o_nsa, o_hg, w_out))
    return xt.reshape(batch, seq, d)
```

```python
import functools

import jax
import jax.numpy as jnp
import numpy as np
from jax import lax
from jax.experimental import pallas as pl
from jax.experimental.pallas import tpu as pltpu

NSA_HEADS = 8
NSA_GROUPS = 2
NSA_REP = NSA_HEADS // NSA_GROUPS
HEAD_DIM = 64
CMP_BLOCK = 32
CMP_STRIDE = 16
CMP_HIDDEN = 128
SEL_BLOCK = 64
SEL_TOPN = 16
WINDOW = 512
HG_HEADS = 4
HG_DK = 128
HG_DV = 128
ROPE_THETA = 10000.0
EPS = 1e-6
NEG = -1e30
LOG2E = 1.4426950408889634

LANES = 128
Q_TILE = 128
NSA_TILES = 2
SEL_CHUNK = 512
HG_CHUNK = 64
HG_SUB = 16
HG_ROWS = 512
VMEM_LIMIT = 56 * 1024 * 1024

BF16 = jnp.bfloat16
F32 = jnp.float32


def _dot(a, b):
    return jnp.dot(a, b, preferred_element_type=F32)


def _dot_nt(a, b):
    return lax.dot_general(a, b, (((1,), (1,)), ((), ())), preferred_element_type=F32)


def _split_bf16(x, parts):
    out = []
    rem = x
    for _ in range(parts):
        p = rem.astype(BF16)
        out.append(p)
        rem = rem - p.astype(F32)
    return out


def _ffn_body(*refs, fuse_mix):
    if fuse_mix:
        (x_ref, ma_ref, mb_ref, woa_ref, wob_ref, g_ref, wg_ref, wu_ref, wd_ref,
         o_ref, xs_ref, hn_ref, acc_ref) = refs
    else:
        x_ref, g_ref, wg_ref, wu_ref, wd_ref, o_ref, xs_ref, hn_ref, acc_ref = refs
    f = pl.program_id(1)

    @pl.when(f == 0)
    def _():
        x = x_ref[...]
        if fuse_mix:
            x = x + _dot(ma_ref[...], woa_ref[0].astype(BF16)) + _dot(mb_ref[...], wob_ref[0].astype(BF16))
        xs_ref[...] = x
        ms = jnp.mean(x * x, axis=-1, keepdims=True)
        hn_ref[...] = (x * lax.rsqrt(ms + EPS) * g_ref[0]).astype(BF16)
        acc_ref[...] = jnp.zeros_like(acc_ref)

    hn = hn_ref[...]
    gate = _dot(hn, wg_ref[0].astype(BF16))
    up = _dot(hn, wu_ref[0].astype(BF16))
    act = (gate * jax.nn.sigmoid(gate) * up).astype(BF16)
    acc_ref[...] += _dot(act, wd_ref[0].astype(BF16))

    @pl.when(f == pl.num_programs(1) - 1)
    def _():
        o_ref[...] = xs_ref[...] + 0.5 * acc_ref[...]


def _ffn(x, g, wg, wu, wd, layer, mix=None, *, tm=1024, tf=256):
    t, d = x.shape
    ff = wg.shape[2]
    assert t % tm == 0 and ff % tf == 0
    fuse = mix is not None
    row = lambda i, f: (i, 0)
    in_specs = [pl.BlockSpec((tm, d), row)]
    args = [x]
    if fuse:
        ma, mb, wo = mix
        da, db = ma.shape[1], mb.shape[1]
        assert da == db and wo.shape[1] == da + db
        in_specs += [pl.BlockSpec((tm, da), row), pl.BlockSpec((tm, db), row),
                     pl.BlockSpec((1, da, d), lambda i, f: (layer, 0, 0)),
                     pl.BlockSpec((1, db, d), lambda i, f: (layer, 1, 0))]
        args += [ma, mb, wo, wo]
    in_specs += [pl.BlockSpec((1, 1, d), lambda i, f: (layer, 0, 0)),
                 pl.BlockSpec((1, d, tf), lambda i, f: (layer, 0, f)),
                 pl.BlockSpec((1, d, tf), lambda i, f: (layer, 0, f)),
                 pl.BlockSpec((1, tf, d), lambda i, f: (layer, f, 0))]
    args += [g.reshape(g.shape[0], 1, d), wg, wu, wd]
    return pl.pallas_call(
        functools.partial(_ffn_body, fuse_mix=fuse),
        grid=(t // tm, ff // tf),
        in_specs=in_specs,
        out_specs=pl.BlockSpec((tm, d), row),
        out_shape=jax.ShapeDtypeStruct((t, d), F32),
        scratch_shapes=[pltpu.VMEM((tm, d), F32), pltpu.VMEM((tm, d), BF16), pltpu.VMEM((tm, d), F32)],
        compiler_params=pltpu.CompilerParams(
            dimension_semantics=("parallel", "arbitrary"), vmem_limit_bytes=VMEM_LIMIT),
        name="ffn_mix" if fuse else "ffn",
    )(*args)


_C_Q = 0
_C_KV = NSA_HEADS * HEAD_DIM
_C_HG = _C_KV + 6 * LANES
_C_GATE = _C_HG + 4 * HG_HEADS * HG_DK
_C_END = _C_GATE + NSA_GROUPS * LANES
GATE_ROWS = 16
Q_SCALE = HEAD_DIM ** -0.5 * LOG2E


def _inproj_body(x_ref, g_ref, w_ref, cos_ref, sin_ref, cost_ref, sint_ref, qg_ref, kg_ref,
                 q_ref, cmp_ref, ksel_ref, kwin_ref, vt_ref, gate_ref, h_ref, *, tiles_per_seq):
    x = x_ref[...]
    ms = jnp.mean(x * x, axis=-1, keepdims=True)
    hn = (x * lax.rsqrt(ms + EPS) * g_ref[...]).astype(BF16)
    cos = cos_ref[...]
    sin = sin_ref[...]
    tm = x.shape[0]
    half = HEAD_DIM // 2

    def proj(c0, width):
        return _dot(hn, w_ref[:, c0:c0 + width])

    ri = (lax.broadcasted_iota(jnp.int32, (2 * LANES, LANES), 0) % LANES) // HEAD_DIM
    ci = lax.broadcasted_iota(jnp.int32, (2 * LANES, LANES), 1) // HEAD_DIM
    head_mean = jnp.where(ri == ci, 1.0 / HEAD_DIM, 0.0).astype(BF16)
    lane = lax.broadcasted_iota(jnp.int32, (tm, LANES), 1)
    first_half = (lane % HEAD_DIM) < half

    def norm_rope(y, gain):
        msq = _dot(jnp.concatenate(_split_bf16(y * y, 2), axis=1), head_mean)
        yn = y * lax.rsqrt(msq + EPS) * gain
        rot = jnp.where(first_half, -pltpu.roll(yn, LANES - half, 1), pltpu.roll(yn, half, 1))
        return yn * cos + rot * sin

    def heads_t(y):
        return y.T.reshape(2, HEAD_DIM, tm)

    qgain = (qg_ref[...] * Q_SCALE)[None]
    cost = cost_ref[...][None]
    sint = sint_ref[...][None]
    yq = proj(_C_Q, NSA_HEADS * HEAD_DIM)
    ykv = proj(_C_KV, 6 * LANES)
    kv = lambda j: ykv[:, j * LANES:(j + 1) * LANES]
    for c in range(NSA_HEADS // 2):
        y3 = heads_t(yq[:, c * LANES:(c + 1) * LANES])
        yn = y3 * lax.rsqrt(jnp.mean(y3 * y3, axis=1, keepdims=True) + EPS) * qgain
        rot = jnp.concatenate([-yn[:, half:], yn[:, :half]], axis=1)
        q_ref[0, 2 * c:2 * c + 2] = (yn * cost + rot * sint).astype(BF16)

    kc = norm_rope(kv(0), kg_ref[0:1, :])
    cmp_ref[0] = kc.astype(BF16)
    cmp_ref[1] = kv(1).astype(BF16)
    ksel = norm_rope(kv(2), kg_ref[1:2, :])
    pos = (pl.program_id(0) % tiles_per_seq) * tm + lax.broadcasted_iota(jnp.int32, (tm, HEAD_DIM), 0)
    blk_in_chunk = (pos // SEL_BLOCK) % (SEL_CHUNK // SEL_BLOCK)
    onehot = jnp.where(lax.broadcasted_iota(jnp.int32, (tm, HEAD_DIM), 1) == blk_in_chunk, 1.0, 0.0)
    for gi in range(NSA_GROUPS):
        ksel_ref[0, gi] = jnp.concatenate(
            [ksel[:, gi * HEAD_DIM:(gi + 1) * HEAD_DIM], onehot], axis=1).astype(BF16)
    kwin = norm_rope(kv(4), kg_ref[2:3, :]).astype(BF16)
    kwin_ref[0, 0] = kwin[:, :HEAD_DIM]
    kwin_ref[0, 1] = kwin[:, HEAD_DIM:]
    for which, col in ((0, 3), (1, 5)):
        vt_ref[which, 0] = heads_t(kv(col)).astype(BF16)

    hw = HG_HEADS * HG_DK
    for i in range(4):
        h_ref[i] = proj(_C_HG + i * hw, hw)
    ygate = proj(_C_GATE, NSA_GROUPS * LANES)
    for gi in range(NSA_GROUPS):
        gt = jax.nn.sigmoid(ygate[:, gi * LANES:(gi + 1) * LANES]).T
        gate_ref[gi] = gt[:GATE_ROWS]


def _inproj(x, g, w2, cos2, sin2, cost, sint, qgt, kg2, batch, seq, *, tm=256):
    t, d = x.shape
    nst = seq // tm
    hw = HG_HEADS * HG_DK
    out_shape = (
        jax.ShapeDtypeStruct((batch, NSA_HEADS, HEAD_DIM, seq), BF16),
        jax.ShapeDtypeStruct((2, t, LANES), BF16),
        jax.ShapeDtypeStruct((batch, NSA_GROUPS, seq, LANES), BF16),
        jax.ShapeDtypeStruct((batch, NSA_GROUPS, seq, HEAD_DIM), BF16),
        jax.ShapeDtypeStruct((2, batch, NSA_GROUPS, HEAD_DIM, seq), BF16),
        jax.ShapeDtypeStruct((NSA_GROUPS, GATE_ROWS, t), F32),
        jax.ShapeDtypeStruct((4, t, hw), F32),
    )
    out_specs = (
        pl.BlockSpec((1, NSA_HEADS, HEAD_DIM, tm), lambda i: (i // nst, 0, 0, i % nst)),
        pl.BlockSpec((2, tm, LANES), lambda i: (0, i, 0)),
        pl.BlockSpec((1, NSA_GROUPS, tm, LANES), lambda i: (i // nst, 0, i % nst, 0)),
        pl.BlockSpec((1, NSA_GROUPS, tm, HEAD_DIM), lambda i: (i // nst, 0, i % nst, 0)),
        pl.BlockSpec((2, 1, NSA_GROUPS, HEAD_DIM, tm), lambda i: (0, i // nst, 0, 0, i % nst)),
        pl.BlockSpec((NSA_GROUPS, GATE_ROWS, tm), lambda i: (0, 0, i)),
        pl.BlockSpec((4, tm, hw), lambda i: (0, i, 0)),
    )
    in_specs = [
        pl.BlockSpec((tm, d), lambda i: (i, 0)),
        pl.BlockSpec((1, d), lambda i: (0, 0)),
        pl.BlockSpec(w2.shape, lambda i: (0, 0)),
        pl.BlockSpec((tm, LANES), lambda i: (i % nst, 0)),
        pl.BlockSpec((tm, LANES), lambda i: (i % nst, 0)),
        pl.BlockSpec((HEAD_DIM, tm), lambda i: (0, i % nst)),
        pl.BlockSpec((HEAD_DIM, tm), lambda i: (0, i % nst)),
        pl.BlockSpec((HEAD_DIM, tm), lambda i: (0, 0)),
        pl.BlockSpec((3, LANES), lambda i: (0, 0)),
    ]
    return pl.pallas_call(
        functools.partial(_inproj_body, tiles_per_seq=nst),
        grid=(t // tm,),
        in_specs=in_specs,
        out_specs=out_specs,
        out_shape=out_shape,
        compiler_params=pltpu.CompilerParams(
            dimension_semantics=("parallel",), vmem_limit_bytes=VMEM_LIMIT),
        name="inproj",
    )(x, g.reshape(1, d), w2, cos2, sin2, cost, sint, qgt, kg2)


def _compress_body(seg_ref, w1a_ref, w1b_ref, pa_ref, pb_ref, w2_ref, o_ref, ot_ref):
    segs = seg_ref[0, 0]
    w1a = w1a_ref[0]
    w1b = w1b_ref[0]
    first = _dot(segs, w1a)
    second = _dot(segs, w1b)
    pa_hi, pa_lo = _split_bf16(jnp.broadcast_to(pa_ref[0], (8, pa_ref.shape[2])), 2)
    pb_hi, pb_lo = _split_bf16(jnp.broadcast_to(pb_ref[0], (8, pb_ref.shape[2])), 2)
    cpos = _dot(pa_hi, w1a) + _dot(pa_lo, w1a) + _dot(pb_hi, w1b) + _dot(pb_lo, w1b)
    nseg = first.shape[0]
    pre = first + pltpu.roll(second, nseg - 1, 0) + cpos[0:1, :]
    hid = jax.nn.gelu(pre, approximate=True).astype(BF16)
    out = _dot(hid, w2_ref[0])
    outb = out.astype(BF16)
    o_ref[0, 0, 0] = outb[:, :HEAD_DIM]
    o_ref[0, 0, 1] = outb[:, HEAD_DIM:]
    ot_ref[0, 0] = out.T.reshape(NSA_GROUPS, HEAD_DIM, nseg).astype(BF16)


def _compress(cmp_in, w1a, w1b, pa, pb, w2, batch, seq):
    nseg = seq // CMP_STRIDE
    width = CMP_STRIDE * LANES
    segs = cmp_in.reshape(2, batch, nseg, width)
    return pl.pallas_call(
        _compress_body,
        grid=(2, batch),
        in_specs=[
            pl.BlockSpec((1, 1, nseg, width), lambda s, b: (s, b, 0, 0)),
            pl.BlockSpec((1,) + w1a.shape[1:], lambda s, b: (s, 0, 0)),
            pl.BlockSpec((1,) + w1b.shape[1:], lambda s, b: (s, 0, 0)),
            pl.BlockSpec((1, 1, width), lambda s, b: (s, 0, 0)),
            pl.BlockSpec((1, 1, width), lambda s, b: (s, 0, 0)),
            pl.BlockSpec((1,) + w2.shape[1:], lambda s, b: (s, 0, 0)),
        ],
        out_specs=(pl.BlockSpec((1, 1, NSA_GROUPS, nseg, HEAD_DIM), lambda s, b: (s, b, 0, 0, 0)),
                   pl.BlockSpec((1, 1, NSA_GROUPS, HEAD_DIM, nseg), lambda s, b: (s, b, 0, 0, 0))),
        out_shape=(jax.ShapeDtypeStruct((2, batch, NSA_GROUPS, nseg, HEAD_DIM), BF16),
                   jax.ShapeDtypeStruct((2, batch, NSA_GROUPS, HEAD_DIM, nseg), BF16)),
        compiler_params=pltpu.CompilerParams(
            dimension_semantics=("parallel", "parallel"), vmem_limit_bytes=VMEM_LIMIT),
        name="compress",
    )(segs, w1a, w1b, pa, pb, w2)


def _tile_heads(a):
    return jnp.concatenate([a] * NSA_REP, axis=1)


def _nsa_body(q_ref, kcc_ref, vcct_ref, ks_ref, kw_ref, vst_ref, vwt_ref, gate_ref, ovl_ref,
              o_ref, imp_ref, selb_ref, qa_ref, sa_ref, sb_ref, p_ref, m_ref, l_ref, acc_ref,
              oc_ref, p3_ref, den3_ref, *, q_lo, q_hi):
    tiles = range(NSA_TILES)
    first = q_lo + pl.program_id(2) * (NSA_TILES * Q_TILE)
    starts = [first + t * Q_TILE for t in tiles]
    lanes = lambda t: slice(t * Q_TILE, (t + 1) * Q_TILE)
    q_t = [jnp.concatenate([q_ref[0, r, :, lanes(t)] for r in range(NSA_REP)], axis=1) for t in tiles]
    tq = [starts[t] + lax.broadcasted_iota(jnp.int32, (1, Q_TILE), 1) for t in tiles]

    kcc = kcc_ref[0, 0, 0]
    ncb = kcc.shape[0]
    wkeys = WINDOW + Q_TILE
    w0 = [pl.multiple_of(jnp.maximum(starts[t] - WINDOW, 0), Q_TILE) for t in tiles]
    s1 = [_dot(kcc, q_t[t]) for t in tiles]
    s3 = [_dot(kw_ref[0, 0, 0, pl.ds(w0[t], wkeys), :], q_t[t]) for t in tiles]

    cend = lax.broadcasted_iota(jnp.int32, (ncb, 1), 0) * CMP_STRIDE + (CMP_BLOCK - 1)
    p1 = []
    for t in tiles:
        sm1 = s1[t] + _tile_heads(jnp.where(cend <= tq[t], 0.0, NEG))
        e1 = jnp.exp2(sm1 - jnp.max(sm1, axis=0, keepdims=True))
        den1 = jnp.sum(e1, axis=0, keepdims=True)
        p1.append(e1 * jnp.where(_tile_heads(tq[t] >= CMP_BLOCK - 1), 1.0 / den1, 0.0))
    ovl = ovl_ref[...]
    nsb = ovl.shape[0]
    imp = []
    for t in tiles:
        oc_ref[t] = _dot(vcct_ref[0, 0, 0], p1[t].astype(BF16))
        p1sum = p1[t][:, 0:Q_TILE]
        for r in range(1, NSA_REP):
            p1sum = p1sum + p1[t][:, r * Q_TILE:(r + 1) * Q_TILE]
        ps_hi, ps_lo = _split_bf16(p1sum, 2)
        imp.append(_dot(ovl, ps_hi) + _dot(ovl, ps_lo))

    jb = lax.broadcasted_iota(jnp.int32, (nsb, Q_TILE), 0)
    valid, impm = [], []
    for t in tiles:
        cur = tq[t] // SEL_BLOCK
        forced = (jb == 0) | (jb == cur) | (jb == cur - 1)
        valid.append(jb * SEL_BLOCK <= tq[t])
        impm.append(jnp.where(valid[t], jnp.where(forced, jnp.inf, imp[t]), -jnp.inf))
        imp_ref[t] = impm[t]
    sub = lax.broadcasted_iota(jnp.int32, (8, Q_TILE), 0)
    ranks = [[jnp.zeros((8, Q_TILE), F32) for _ in range(nsb // 8)] for _ in tiles]
    seen = q_hi // SEL_BLOCK
    for i in range(seen if seen > SEL_TOPN else 0):
        for t in tiles:
            xi = jnp.broadcast_to(imp_ref[t, i:i + 1, :], (8, Q_TILE))
            for v in range(seen // 8):
                xv = impm[t][8 * v:8 * v + 8, :]
                ge = jnp.where(xi >= xv, 1.0, 0.0)
                gt = jnp.where(xi > xv, 1.0, 0.0)
                if 8 * v > i:
                    inc = ge
                elif 8 * v + 7 < i:
                    inc = gt
                else:
                    inc = jnp.where(sub + 8 * v > i, ge, gt)
                ranks[t][v] = ranks[t][v] + inc
    for t in tiles:
        rank = jnp.concatenate(ranks[t], axis=0)
        selb_ref[t] = jnp.where((rank < SEL_TOPN) & valid[t], 0.0, NEG)

    blocks_per_chunk = SEL_CHUNK // SEL_BLOCK
    last_chunk = ks_ref.shape[3] // SEL_CHUNK - 1
    n_chunks = (first + NSA_TILES * Q_TILE + SEL_CHUNK - 1) // SEL_CHUNK
    own_row = [pl.multiple_of(starts[t] - (n_chunks - 1) * SEL_CHUNK, Q_TILE) for t in tiles]
    own_mask = _tile_heads(jnp.where(
        lax.broadcasted_iota(jnp.int32, (Q_TILE, Q_TILE), 0) <= lax.broadcasted_iota(jnp.int32, (Q_TILE, Q_TILE), 1),
        0.0, 1.0))
    bias_pad = jnp.zeros((HEAD_DIM - blocks_per_chunk, NSA_REP * Q_TILE), F32)

    def scores(t, chunk):
        blk_bias = _tile_heads(selb_ref[t, pl.ds(pl.multiple_of(chunk * blocks_per_chunk, blocks_per_chunk),
                                                 blocks_per_chunk), :])
        qa_ref[t, HEAD_DIM:, :] = jnp.concatenate([blk_bias, bias_pad], axis=0).astype(BF16)
        k0 = pl.multiple_of(chunk * SEL_CHUNK, SEL_CHUNK)
        return _dot(ks_ref[0, 0, 0, pl.ds(k0, SEL_CHUNK), :], qa_ref[t])

    def put_scores(t, chunk, s_ref):
        s_ref[t] = scores(t, chunk)
        own_neg = jnp.where(chunk == n_chunks - 1, NEG, 0.0)
        s_ref[t, pl.ds(own_row[t], Q_TILE), :] = s_ref[t, pl.ds(own_row[t], Q_TILE), :] + own_mask * own_neg

    def sel_step(c, s_cur_ref, s_next_ref):
        m_old, m_new = [], []
        for t in tiles:
            m_old.append(m_ref[t])
            m_new.append(jnp.maximum(m_old[t], jnp.max(s_cur_ref[t], axis=0, keepdims=True)))
        nxt = jnp.minimum(c + 1, last_chunk)
        for t in tiles:
            put_scores(t, nxt, s_next_ref)
        v0 = pl.multiple_of(jnp.maximum(c - 1, 0) * SEL_CHUNK, SEL_CHUNK)
        v_prev = vst_ref[0, 0, 0, :, pl.ds(v0, SEL_CHUNK)]
        pv = [_dot(v_prev, p_ref[t]) for t in tiles]
        for t in tiles:
            p = jnp.exp2(s_cur_ref[t] - m_new[t])
            alpha = jnp.exp2(m_old[t] - m_new[t])
            l_ref[t] = alpha * l_ref[t] + jnp.sum(p, axis=0, keepdims=True)
            acc_ref[t] = alpha * (acc_ref[t] + pv[t])
            m_ref[t] = m_new[t]
            p_ref[t] = p.astype(BF16)

    for t in tiles:
        qa_ref[t, 0:HEAD_DIM, :] = q_t[t]
        put_scores(t, 0, sa_ref)
    p_ref[...] = jnp.zeros_like(p_ref)

    for t in tiles:
        dist = tq[t] - (w0[t] + lax.broadcasted_iota(jnp.int32, (wkeys, 1), 0))
        sm3 = s3[t] + _tile_heads(jnp.where((dist >= 0) & (dist < WINDOW), 0.0, NEG))
        e3 = jnp.exp2(sm3 - jnp.max(sm3, axis=0, keepdims=True))
        den3_ref[t] = jnp.sum(e3, axis=0, keepdims=True)
        p3_ref[t] = e3.astype(BF16)

    m_ref[...] = jnp.full(m_ref.shape, NEG, F32)
    l_ref[...] = jnp.zeros_like(l_ref)
    acc_ref[...] = jnp.zeros_like(acc_ref)

    def sel_pair(i, carry):
        sel_step(2 * i, sa_ref, sb_ref)

        @pl.when(2 * i + 1 < n_chunks)
        def _():
            sel_step(2 * i + 1, sb_ref, sa_ref)
        return carry

    lax.fori_loop(0, (n_chunks + 1) // 2, sel_pair, 0)

    vl = pl.multiple_of((n_chunks - 1) * SEL_CHUNK, SEL_CHUNK)
    v_last = vst_ref[0, 0, 0, :, pl.ds(vl, SEL_CHUNK)]
    acc_w = [_dot(vwt_ref[0, 0, 0, :, pl.ds(w0[t], wkeys)], p3_ref[t]) for t in tiles]
    acc_s = [acc_ref[t] + _dot(v_last, p_ref[t]) for t in tiles]
    for t in tiles:
        gates = gate_ref[0, :, lanes(t)]
        g_c = jnp.concatenate([gates[3 * r:3 * r + 1, :] for r in range(NSA_REP)], axis=1)
        g_s = jnp.concatenate([gates[3 * r + 1:3 * r + 2, :] for r in range(NSA_REP)], axis=1)
        g_w = jnp.concatenate([gates[3 * r + 2:3 * r + 3, :] for r in range(NSA_REP)], axis=1)
        out_t = g_c * oc_ref[t] + (g_w / den3_ref[t]) * acc_w[t] + (g_s / l_ref[t]) * acc_s[t]
        out_t = jnp.concatenate([out_t[:, r * Q_TILE:(r + 1) * Q_TILE] for r in range(NSA_REP)], axis=0)
        o_ref[0, lanes(t), :] = out_t.T.astype(o_ref.dtype)


def _nsa(q_t, cc, cc_t, ksel, kwin, vt, gates_t, ovl, batch, seq, q_lo, q_hi):
    span = NSA_TILES * Q_TILE
    nqb = (q_hi - q_lo) // span
    off = q_lo // span
    nq_all = seq // span
    ncb_pad = cc.shape[3]
    nsb = seq // SEL_BLOCK
    rows = NSA_REP * Q_TILE
    vspec = lambda which: pl.BlockSpec((1, 1, 1, HEAD_DIM, seq), lambda b, g, i: (which, b, g, 0, 0))
    per_tile = lambda shape, dtype: pltpu.VMEM((NSA_TILES,) + shape, dtype)
    return pl.pallas_call(
        functools.partial(_nsa_body, q_lo=q_lo, q_hi=q_hi),
        grid=(batch, NSA_GROUPS, nqb),
        in_specs=[
            pl.BlockSpec((1, NSA_REP, HEAD_DIM, span), lambda b, g, i: (b, g, 0, off + i)),
            pl.BlockSpec((1, 1, 1, ncb_pad, HEAD_DIM), lambda b, g, i: (0, b, g, 0, 0)),
            pl.BlockSpec((1, 1, 1, HEAD_DIM, ncb_pad), lambda b, g, i: (1, b, g, 0, 0)),
            pl.BlockSpec((1, 1, 1, seq, LANES), lambda b, g, i: (0, b, g, 0, 0)),
            pl.BlockSpec((1, 1, 1, seq, HEAD_DIM), lambda b, g, i: (0, b, g, 0, 0)),
            vspec(0), vspec(1),
            pl.BlockSpec((1, GATE_ROWS, span), lambda b, g, i: (g, 0, b * nq_all + off + i)),
            pl.BlockSpec(ovl.shape, lambda b, g, i: (0, 0)),
        ],
        out_specs=pl.BlockSpec((1, span, NSA_REP * HEAD_DIM), lambda b, g, i: (b, i, g)),
        out_shape=jax.ShapeDtypeStruct((batch, q_hi - q_lo, NSA_HEADS * HEAD_DIM), BF16),
        scratch_shapes=[per_tile((nsb, Q_TILE), F32), per_tile((nsb, Q_TILE), F32),
                        per_tile((2 * HEAD_DIM, rows), BF16),
                        per_tile((SEL_CHUNK, rows), F32), per_tile((SEL_CHUNK, rows), F32),
                        per_tile((SEL_CHUNK, rows), BF16),
                        per_tile((1, rows), F32), per_tile((1, rows), F32),
                        per_tile((HEAD_DIM, rows), F32), per_tile((HEAD_DIM, rows), F32),
                        per_tile((WINDOW + Q_TILE, rows), BF16), per_tile((1, rows), F32)],
        compiler_params=pltpu.CompilerParams(
            dimension_semantics=("parallel", "parallel", "arbitrary"), vmem_limit_bytes=VMEM_LIMIT),
        name="nsa",
    )(q_t, cc, cc_t, ksel[None], kwin[None], vt, vt, gates_t, ovl)


def _hgrn_body(hq_ref, hf_ref, hi_ref, hg_ref, lbl_ref, gain_ref, o_ref,
               state_ref, b_ref, k_ref, v_ref, q_ref, *, layer):
    @pl.when(pl.program_id(1) == 0)
    def _():
        state_ref[...] = jnp.zeros_like(state_ref)

    logits = lbl_ref[...]
    ex = jnp.exp(logits - jnp.max(logits, axis=0, keepdims=True))
    sm = ex / jnp.sum(ex, axis=0, keepdims=True)
    lb_all = jnp.zeros((1, HG_HEADS * HG_DK), F32)
    for i in range(1, layer + 1):
        lb_all = lb_all + sm[i:i + 1, :]

    ch, blk, half = HG_CHUNK, HG_SUB, HG_SUB // 2
    nblk = ch // blk
    ri = lax.broadcasted_iota(jnp.int32, (ch, ch), 0)
    ci = lax.broadcasted_iota(jnp.int32, (ch, ch), 1)
    tril = jnp.where(ri >= ci, 1.0, 0.0).astype(BF16)
    blk_row = lax.broadcasted_iota(jnp.int32, (blk, HG_DK), 0)
    low_rows = blk_row >= half
    rr = lax.broadcasted_iota(jnp.int32, (blk, blk), 0)
    cc = lax.broadcasted_iota(jnp.int32, (blk, blk), 1)
    cross = (rr >= half) & (cc < half)
    half_row = lax.broadcasted_iota(jnp.int32, (half, HG_DK), 0)
    gain = gain_ref[...]
    heads = range(HG_HEADS)

    def chunk(j, carry):
        r0 = pl.multiple_of(j * ch, ch)
        for h in heads:
            cols = slice(h * HG_DK, (h + 1) * HG_DK)
            lb = lb_all[:, cols]
            hq = hq_ref[0, pl.ds(r0, ch), cols]
            hf = hf_ref[0, pl.ds(r0, ch), cols]
            f = lb + (1.0 - lb) * jax.nn.sigmoid(hf)
            lf = _split_bf16(jnp.log(jnp.maximum(f, 1e-30)), 3)
            b_ref[h] = _dot(tril, lf[0]) + _dot(tril, lf[1]) + _dot(tril, lf[2])
            k_ref[h] = 1.0 - f
            v_ref[h] = hi_ref[0, pl.ds(r0, ch), cols]
            q_ref[h] = hq * jax.nn.sigmoid(hq)

        o_blocks, tables = [], []
        for h in heads:
            inter = _dot_nt((q_ref[h] * jnp.exp(b_ref[h])).astype(BF16), state_ref[h].astype(BF16))
            o_blocks.append([inter[ib * blk:(ib + 1) * blk, :] for ib in range(nblk)])
            tables.append([])
            for ib in range(nblk):
                i0 = ib * blk
                qi = q_ref[h, i0:i0 + blk, :]
                bi = b_ref[h, i0:i0 + blk, :]
                a = None
                if ib > 0:
                    ref_b = b_ref[h, i0 - 1:i0, :]
                    qd = (qi * jnp.exp(bi - ref_b)).astype(BF16)
                    kd = (k_ref[h, 0:i0, :] * jnp.exp(ref_b - b_ref[h, 0:i0, :])).astype(BF16)
                    a = _dot_nt(qd, kd).astype(BF16)
                mid_b = b_ref[h, i0 + half - 1:i0 + half, :]
                qd2 = (qi * jnp.exp(jnp.where(low_rows, bi - mid_b, 0.0))).astype(BF16)
                kd2 = (k_ref[h, i0:i0 + blk, :] * jnp.exp(jnp.where(low_rows, 0.0, mid_b - bi))).astype(BF16)
                a2 = jnp.where(cross, _dot_nt(qd2, kd2), 0.0).astype(BF16)
                tables[h].append((a, a2))
        for h in heads:
            bl = b_ref[h, ch - 1:ch, :]
            kd_end = (k_ref[h] * jnp.exp(bl - b_ref[h])).astype(BF16)
            vt = v_ref[h].T.astype(BF16)
            state_ref[h] = state_ref[h] * jnp.exp(bl) + _dot(vt, kd_end)

        for h in heads:
            for ib in range(nblk):
                i0 = ib * blk
                a, a2 = tables[h][ib]
                o_i = o_blocks[h][ib] + _dot(a2, v_ref[h, i0:i0 + blk, :].astype(BF16))
                if a is not None:
                    o_i = o_i + _dot(a, v_ref[h, 0:i0, :].astype(BF16))
                o_blocks[h][ib] = o_i

        nhalf = ch // half
        diag = [[jnp.zeros((half, HG_DV), F32) for _ in range(nhalf)] for _ in heads]
        for s in range(half):
            for h in heads:
                for ih in range(nhalf):
                    g0 = ih * half
                    bs = b_ref[h, g0 + s:g0 + s + 1, :]
                    ks = k_ref[h, g0 + s:g0 + s + 1, :]
                    vs = v_ref[h, g0 + s:g0 + s + 1, :]
                    bi = b_ref[h, g0:g0 + half, :]
                    qi = q_ref[h, g0:g0 + half, :]
                    dec = jnp.exp(jnp.where(half_row >= s, bi - bs, NEG))
                    a_s = jnp.sum(qi * dec * ks, axis=-1, keepdims=True)
                    diag[h][ih] = diag[h][ih] + a_s * vs

        for h in heads:
            cols = slice(h * HG_DK, (h + 1) * HG_DK)
            o = jnp.concatenate(o_blocks[h], axis=0) + jnp.concatenate(diag[h], axis=0)
            o = o * lax.rsqrt(jnp.mean(o * o, axis=-1, keepdims=True) + EPS) * gain
            hg = hg_ref[0, pl.ds(r0, ch), cols]
            o_ref[pl.ds(r0, ch), cols] = (o * (hg * jax.nn.sigmoid(hg))).astype(o_ref.dtype)
        return carry

    lax.fori_loop(0, hq_ref.shape[1] // ch, chunk, 0)


def _hgrn(h_all, lb_logits, out_gain, layer, batch, seq):
    t = h_all.shape[1]
    nblk = seq // HG_ROWS
    nl = lb_logits.shape[0]
    hw = HG_HEADS * HG_DK
    hspec = lambda which: pl.BlockSpec((1, HG_ROWS, hw), lambda b, c: (which, b * nblk + c, 0))
    return pl.pallas_call(
        functools.partial(_hgrn_body, layer=layer),
        grid=(batch, nblk),
        in_specs=[hspec(0), hspec(1), hspec(2), hspec(3),
                  pl.BlockSpec((nl, hw), lambda b, c: (0, 0)),
                  pl.BlockSpec((1, HG_DV), lambda b, c: (0, 0))],
        out_specs=pl.BlockSpec((HG_ROWS, HG_HEADS * HG_DV), lambda b, c: (b * nblk + c, 0)),
        out_shape=jax.ShapeDtypeStruct((t, HG_HEADS * HG_DV), BF16),
        scratch_shapes=[pltpu.VMEM((HG_HEADS, HG_DV, HG_DK), F32)]
        + [pltpu.VMEM((HG_HEADS, HG_CHUNK, HG_DK), F32)] * 4,
        compiler_params=pltpu.CompilerParams(
            dimension_semantics=("parallel", "arbitrary"), vmem_limit_bytes=VMEM_LIMIT),
        name="hgrn",
    )(h_all, h_all, h_all, h_all, lb_logits, out_gain.reshape(1, HG_DV))


def _relayout_w_in(w):
    d = w.shape[0]
    nq = NSA_HEADS * HEAD_DIM
    nkv = 6 * NSA_GROUPS * HEAD_DIM
    ng = 3 * NSA_HEADS
    q_kv = w[:, :nq + nkv]
    gts = w[:, nq + nkv:nq + nkv + ng]
    hg = w[:, nq + nkv + ng:]
    per = 3 * NSA_REP
    gate_blocks = [jnp.pad(gts[:, gi * per:(gi + 1) * per], ((0, 0), (0, LANES - per)))
                   for gi in range(NSA_GROUPS)]
    out = jnp.concatenate([q_kv, hg] + gate_blocks, axis=1).astype(BF16)
    assert out.shape == (d, _C_END)
    return out


def _compress_weights(pos, w1, w2):
    eye = jnp.eye(NSA_GROUPS, dtype=F32)
    e = w1.shape[-1]

    def big(w1_half):
        y = jnp.einsum("sldh,gk->slgdkh", w1_half, eye)
        return y.reshape(2, CMP_STRIDE * NSA_GROUPS * HEAD_DIM, NSA_GROUPS * e).astype(BF16)

    def pos_rows(p_half):
        y = jnp.broadcast_to(p_half[:, :, None, :], (2, CMP_STRIDE, NSA_GROUPS, HEAD_DIM))
        return y.reshape(2, 1, CMP_STRIDE * NSA_GROUPS * HEAD_DIM)

    w2b = jnp.einsum("shd,gk->sghkd", w2, eye).reshape(2, NSA_GROUPS * e, NSA_GROUPS * HEAD_DIM)
    return (big(w1[:, :CMP_STRIDE]), big(w1[:, CMP_STRIDE:]),
            pos_rows(pos[:, :CMP_STRIDE]), pos_rows(pos[:, CMP_STRIDE:]), w2b.astype(BF16))


def kernel(x, ffn1_norm, ffn1_w_gate, ffn1_w_up, ffn1_w_down, mix_norm, w_in, q_norm, k_norm,
           cmp_pos, cmp_w1, cmp_w2, hgrn_lb_logits, hgrn_out_norm, w_out,
           ffn2_norm, ffn2_w_gate, ffn2_w_up, ffn2_w_down):
    batch, seq, d = x.shape
    depth = w_in.shape[0]
    assert CMP_BLOCK == 2 * CMP_STRIDE and seq % SEL_CHUNK == 0 and seq >= WINDOW + Q_TILE
    assert seq % HG_ROWS == 0 and SEL_CHUNK % SEL_BLOCK == 0 and SEL_CHUNK % (NSA_TILES * Q_TILE) == 0
    assert seq % (2 * SEL_CHUNK) == 0
    t = batch * seq
    tm_proj = 256

    inv = 1.0 / (ROPE_THETA ** (jnp.arange(0, HEAD_DIM, 2, dtype=F32) / HEAD_DIM))
    ang = jnp.arange(seq, dtype=F32)[:, None] * inv[None, :]
    ang = jnp.concatenate([ang, ang], axis=-1)
    cos1, sin1 = jnp.cos(ang), jnp.sin(ang)
    cos2, sin2 = jnp.tile(cos1, (1, 2)), jnp.tile(sin1, (1, 2))
    cost, sint = cos1.T, sin1.T

    nseg = seq // CMP_STRIDE
    nsb = seq // SEL_BLOCK
    ci = np.arange(nseg)[None, :]
    sj = np.arange(nsb)[:, None]
    ovl = ((ci * CMP_STRIDE <= sj * SEL_BLOCK + SEL_BLOCK - 1)
           & (ci * CMP_STRIDE + CMP_BLOCK - 1 >= sj * SEL_BLOCK)
           & (ci < (seq - CMP_BLOCK) // CMP_STRIDE + 1))
    ovl = jnp.asarray(ovl, dtype=BF16)

    xt = x.reshape(t, d)
    for l in range(depth):
        xt = _ffn(xt, ffn1_norm, ffn1_w_gate, ffn1_w_up, ffn1_w_down, l)
        qgt = jnp.broadcast_to(q_norm[l][:, None], (HEAD_DIM, tm_proj))
        kg2 = jnp.tile(k_norm[l], (1, 2))
        q_t, cmp_in, ksel, kwin, vt, gates_t, h_all = _inproj(
            xt, mix_norm[l], _relayout_w_in(w_in[l]), cos2, sin2, cost, sint, qgt, kg2,
            batch, seq, tm=tm_proj)
        w1a, w1b, pa, pb, w2b = _compress_weights(cmp_pos[l], cmp_w1[l], cmp_w2[l])
        cc, cc_t = _compress(cmp_in, w1a, w1b, pa, pb, w2b, batch, seq)
        qr = 2 * SEL_CHUNK
        o_nsa = jnp.concatenate([_nsa(q_t, cc, cc_t, ksel, kwin, vt, gates_t, ovl, batch, seq, lo, lo + qr)
                                 for lo in range(0, seq, qr)], axis=1).reshape(t, NSA_HEADS * HEAD_DIM)
        o_hg = _hgrn(h_all, hgrn_lb_logits, hgrn_out_norm[l], l, batch, seq)
        xt = _ffn(xt, ffn2_norm, ffn2_w_gate, ffn2_w_up, ffn2_w_down, l, mix=(o_nsa, o_hg, w_out))
    return xt.reshape(batch, seq, d)
```

```python
import functools

import jax
import jax.numpy as jnp
import numpy as np
from jax import lax
from jax.experimental import pallas as pl
from jax.experimental.pallas import tpu as pltpu

NSA_HEADS = 8
NSA_GROUPS = 2
NSA_REP = NSA_HEADS // NSA_GROUPS
HEAD_DIM = 64
CMP_BLOCK = 32
CMP_STRIDE = 16
CMP_HIDDEN = 128
SEL_BLOCK = 64
SEL_TOPN = 16
WINDOW = 512
HG_HEADS = 4
HG_DK = 128
HG_DV = 128
ROPE_THETA = 10000.0
EPS = 1e-6
NEG = -1e30
LOG2E = 1.4426950408889634

LANES = 128
Q_TILE = 128
NSA_TILES = 2
SEL_CHUNK = 512
HG_CHUNK = 64
HG_SUB = 16
HG_ROWS = 512
VMEM_LIMIT = 56 * 1024 * 1024

BF16 = jnp.bfloat16
F32 = jnp.float32


def _dot(a, b):
    return jnp.dot(a, b, preferred_element_type=F32)


def _dot_nt(a, b):
    return lax.dot_general(a, b, (((1,), (1,)), ((), ())), preferred_element_type=F32)


def _split_bf16(x, parts):
    out = []
    rem = x
    for _ in range(parts):
        p = rem.astype(BF16)
        out.append(p)
        rem = rem - p.astype(F32)
    return out


def _ffn_body(*refs, fuse_mix):
    if fuse_mix:
        (x_ref, ma_ref, mb_ref, woa_ref, wob_ref, g_ref, wg_ref, wu_ref, wd_ref,
         o_ref, xs_ref, hn_ref, acc_ref) = refs
    else:
        x_ref, g_ref, wg_ref, wu_ref, wd_ref, o_ref, xs_ref, hn_ref, acc_ref = refs
    f = pl.program_id(1)

    @pl.when(f == 0)
    def _():
        x = x_ref[...]
        if fuse_mix:
            x = x + _dot(ma_ref[...], woa_ref[0].astype(BF16)) + _dot(mb_ref[...], wob_ref[0].astype(BF16))
        xs_ref[...] = x
        ms = jnp.mean(x * x, axis=-1, keepdims=True)
        hn_ref[...] = (x * lax.rsqrt(ms + EPS) * g_ref[0]).astype(BF16)
        acc_ref[...] = jnp.zeros_like(acc_ref)

    hn = hn_ref[...]
    gate = _dot(hn, wg_ref[0].astype(BF16))
    up = _dot(hn, wu_ref[0].astype(BF16))
    act = (gate * jax.nn.sigmoid(gate) * up).astype(BF16)
    acc_ref[...] += _dot(act, wd_ref[0].astype(BF16))

    @pl.when(f == pl.num_programs(1) - 1)
    def _():
        o_ref[...] = xs_ref[...] + 0.5 * acc_ref[...]


def _ffn(x, g, wg, wu, wd, layer, mix=None, *, tm=1024, tf=256):
    t, d = x.shape
    ff = wg.shape[2]
    assert t % tm == 0 and ff % tf == 0
    fuse = mix is not None
    row = lambda i, f: (i, 0)
    in_specs = [pl.BlockSpec((tm, d), row)]
    args = [x]
    if fuse:
        ma, mb, wo = mix
        da, db = ma.shape[1], mb.shape[1]
        assert da == db and wo.shape[1] == da + db
        in_specs += [pl.BlockSpec((tm, da), row), pl.BlockSpec((tm, db), row),
                     pl.BlockSpec((1, da, d), lambda i, f: (layer, 0, 0)),
                     pl.BlockSpec((1, db, d), lambda i, f: (layer, 1, 0))]
        args += [ma, mb, wo, wo]
    in_specs += [pl.BlockSpec((1, 1, d), lambda i, f: (layer, 0, 0)),
                 pl.BlockSpec((1, d, tf), lambda i, f: (layer, 0, f)),
                 pl.BlockSpec((1, d, tf), lambda i, f: (layer, 0, f)),
                 pl.BlockSpec((1, tf, d), lambda i, f: (layer, f, 0))]
    args += [g.reshape(g.shape[0], 1, d), wg, wu, wd]
    return pl.pallas_call(
        functools.partial(_ffn_body, fuse_mix=fuse),
        grid=(t // tm, ff // tf),
        in_specs=in_specs,
        out_specs=pl.BlockSpec((tm, d), row),
        out_shape=jax.ShapeDtypeStruct((t, d), F32),
        scratch_shapes=[pltpu.VMEM((tm, d), F32), pltpu.VMEM((tm, d), BF16), pltpu.VMEM((tm, d), F32)],
        compiler_params=pltpu.CompilerParams(
            dimension_semantics=("parallel", "arbitrary"), vmem_limit_bytes=VMEM_LIMIT),
        name="ffn_mix" if fuse else "ffn",
    )(*args)


_C_Q = 0
_C_KV = NSA_HEADS * HEAD_DIM
_C_HG = _C_KV + 6 * LANES
_C_GATE = _C_HG + 4 * HG_HEADS * HG_DK
_C_END = _C_GATE + NSA_GROUPS * LANES
GATE_ROWS = 16
Q_SCALE = HEAD_DIM ** -0.5 * LOG2E


def _inproj_body(x_ref, g_ref, w_ref, cos_ref, sin_ref, cost_ref, sint_ref, qg_ref, kg_ref,
                 q_ref, cmp_ref, ksel_ref, kwin_ref, vt_ref, gate_ref, h_ref, *, tiles_per_seq):
    x = x_ref[...]
    ms = jnp.mean(x * x, axis=-1, keepdims=True)
    hn = (x * lax.rsqrt(ms + EPS) * g_ref[...]).astype(BF16)
    cos = cos_ref[...]
    sin = sin_ref[...]
    tm = x.shape[0]
    half = HEAD_DIM // 2

    def proj(c0, width):
        return _dot(hn, w_ref[:, c0:c0 + width])

    ri = (lax.broadcasted_iota(jnp.int32, (2 * LANES, LANES), 0) % LANES) // HEAD_DIM
    ci = lax.broadcasted_iota(jnp.int32, (2 * LANES, LANES), 1) // HEAD_DIM
    head_mean = jnp.where(ri == ci, 1.0 / HEAD_DIM, 0.0).astype(BF16)
    lane = lax.broadcasted_iota(jnp.int32, (tm, LANES), 1)
    first_half = (lane % HEAD_DIM) < half

    def norm_rope(y, gain):
        msq = _dot(jnp.concatenate(_split_bf16(y * y, 2), axis=1), head_mean)
        yn = y * lax.rsqrt(msq + EPS) * gain
        rot = jnp.where(first_half, -pltpu.roll(yn, LANES - half, 1), pltpu.roll(yn, half, 1))
        return yn * cos + rot * sin

    def heads_t(y):
        return y.T.reshape(2, HEAD_DIM, tm)

    qgain = (qg_ref[...] * Q_SCALE)[None]
    cost = cost_ref[...][None]
    sint = sint_ref[...][None]
    yq = proj(_C_Q, NSA_HEADS * HEAD_DIM)
    ykv = proj(_C_KV, 6 * LANES)
    kv = lambda j: ykv[:, j * LANES:(j + 1) * LANES]
    for c in range(NSA_HEADS // 2):
        y3 = heads_t(yq[:, c * LANES:(c + 1) * LANES])
        yn = y3 * lax.rsqrt(jnp.mean(y3 * y3, axis=1, keepdims=True) + EPS) * qgain
        rot = jnp.concatenate([-yn[:, half:], yn[:, :half]], axis=1)
        q_ref[0, 2 * c:2 * c + 2] = (yn * cost + rot * sint).astype(BF16)

    kc = norm_rope(kv(0), kg_ref[0:1, :])
    cmp_ref[0] = kc.astype(BF16)
    cmp_ref[1] = kv(1).astype(BF16)
    ksel = norm_rope(kv(2), kg_ref[1:2, :])
    pos = (pl.program_id(0) % tiles_per_seq) * tm + lax.broadcasted_iota(jnp.int32, (tm, HEAD_DIM), 0)
    blk_in_chunk = (pos // SEL_BLOCK) % (SEL_CHUNK // SEL_BLOCK)
    onehot = jnp.where(lax.broadcasted_iota(jnp.int32, (tm, HEAD_DIM), 1) == blk_in_chunk, 1.0, 0.0)
    for gi in range(NSA_GROUPS):
        ksel_ref[0, gi] = jnp.concatenate(
            [ksel[:, gi * HEAD_DIM:(gi + 1) * HEAD_DIM], onehot], axis=1).astype(BF16)
    kwin = norm_rope(kv(4), kg_ref[2:3, :]).astype(BF16)
    kwin_ref[0, 0] = kwin[:, :HEAD_DIM]
    kwin_ref[0, 1] = kwin[:, HEAD_DIM:]
    for which, col in ((0, 3), (1, 5)):
        vt_ref[which, 0] = heads_t(kv(col)).astype(BF16)

    hw = HG_HEADS * HG_DK
    for i in range(4):
        h_ref[i] = proj(_C_HG + i * hw, hw)
    ygate = proj(_C_GATE, NSA_GROUPS * LANES)
    for gi in range(NSA_GROUPS):
        gt = jax.nn.sigmoid(ygate[:, gi * LANES:(gi + 1) * LANES]).T
        gate_ref[gi] = gt[:GATE_ROWS]


def _inproj(x, g, w2, cos2, sin2, cost, sint, qgt, kg2, batch, seq, *, tm=256):
    t, d = x.shape
    nst = seq // tm
    hw = HG_HEADS * HG_DK
    out_shape = (
        jax.ShapeDtypeStruct((batch, NSA_HEADS, HEAD_DIM, seq), BF16),
        jax.ShapeDtypeStruct((2, t, LANES), BF16),
        jax.ShapeDtypeStruct((batch, NSA_GROUPS, seq, LANES), BF16),
        jax.ShapeDtypeStruct((batch, NSA_GROUPS, seq, HEAD_DIM), BF16),
        jax.ShapeDtypeStruct((2, batch, NSA_GROUPS, HEAD_DIM, seq), BF16),
        jax.ShapeDtypeStruct((NSA_GROUPS, GATE_ROWS, t), F32),
        jax.ShapeDtypeStruct((4, t, hw), F32),
    )
    out_specs = (
        pl.BlockSpec((1, NSA_HEADS, HEAD_DIM, tm), lambda i: (i // nst, 0, 0, i % nst)),
        pl.BlockSpec((2, tm, LANES), lambda i: (0, i, 0)),
        pl.BlockSpec((1, NSA_GROUPS, tm, LANES), lambda i: (i // nst, 0, i % nst, 0)),
        pl.BlockSpec((1, NSA_GROUPS, tm, HEAD_DIM), lambda i: (i // nst, 0, i % nst, 0)),
        pl.BlockSpec((2, 1, NSA_GROUPS, HEAD_DIM, tm), lambda i: (0, i // nst, 0, 0, i % nst)),
        pl.BlockSpec((NSA_GROUPS, GATE_ROWS, tm), lambda i: (0, 0, i)),
        pl.BlockSpec((4, tm, hw), lambda i: (0, i, 0)),
    )
    in_specs = [
        pl.BlockSpec((tm, d), lambda i: (i, 0)),
        pl.BlockSpec((1, d), lambda i: (0, 0)),
        pl.BlockSpec(w2.shape, lambda i: (0, 0)),
        pl.BlockSpec((tm, LANES), lambda i: (i % nst, 0)),
        pl.BlockSpec((tm, LANES), lambda i: (i % nst, 0)),
        pl.BlockSpec((HEAD_DIM, tm), lambda i: (0, i % nst)),
        pl.BlockSpec((HEAD_DIM, tm), lambda i: (0, i % nst)),
        pl.BlockSpec((HEAD_DIM, tm), lambda i: (0, 0)),
        pl.BlockSpec((3, LANES), lambda i: (0, 0)),
    ]
    return pl.pallas_call(
        functools.partial(_inproj_body, tiles_per_seq=nst),
        grid=(t // tm,),
        in_specs=in_specs,
        out_specs=out_specs,
        out_shape=out_shape,
        compiler_params=pltpu.CompilerParams(
            dimension_semantics=("parallel",), vmem_limit_bytes=VMEM_LIMIT),
        name="inproj",
    )(x, g.reshape(1, d), w2, cos2, sin2, cost, sint, qgt, kg2)


def _compress_body(seg_ref, w1a_ref, w1b_ref, pa_ref, pb_ref, w2_ref, o_ref, ot_ref):
    segs = seg_ref[0, 0]
    w1a = w1a_ref[0]
    w1b = w1b_ref[0]
    first = _dot(segs, w1a)
    second = _dot(segs, w1b)
    pa_hi, pa_lo = _split_bf16(jnp.broadcast_to(pa_ref[0], (8, pa_ref.shape[2])), 2)
    pb_hi, pb_lo = _split_bf16(jnp.broadcast_to(pb_ref[0], (8, pb_ref.shape[2])), 2)
    cpos = _dot(pa_hi, w1a) + _dot(pa_lo, w1a) + _dot(pb_hi, w1b) + _dot(pb_lo, w1b)
    nseg = first.shape[0]
    pre = first + pltpu.roll(second, nseg - 1, 0) + cpos[0:1, :]
    hid = jax.nn.gelu(pre, approximate=True).astype(BF16)
    out = _dot(hid, w2_ref[0])
    outb = out.astype(BF16)
    o_ref[0, 0, 0] = outb[:, :HEAD_DIM]
    o_ref[0, 0, 1] = outb[:, HEAD_DIM:]
    ot_ref[0, 0] = out.T.reshape(NSA_GROUPS, HEAD_DIM, nseg).astype(BF16)


def _compress(cmp_in, w1a, w1b, pa, pb, w2, batch, seq):
    nseg = seq // CMP_STRIDE
    width = CMP_STRIDE * LANES
    segs = cmp_in.reshape(2, batch, nseg, width)
    return pl.pallas_call(
        _compress_body,
        grid=(2, batch),
        in_specs=[
            pl.BlockSpec((1, 1, nseg, width), lambda s, b: (s, b, 0, 0)),
            pl.BlockSpec((1,) + w1a.shape[1:], lambda s, b: (s, 0, 0)),
            pl.BlockSpec((1,) + w1b.shape[1:], lambda s, b: (s, 0, 0)),
            pl.BlockSpec((1, 1, width), lambda s, b: (s, 0, 0)),
            pl.BlockSpec((1, 1, width), lambda s, b: (s, 0, 0)),
            pl.BlockSpec((1,) + w2.shape[1:], lambda s, b: (s, 0, 0)),
        ],
        out_specs=(pl.BlockSpec((1, 1, NSA_GROUPS, nseg, HEAD_DIM), lambda s, b: (s, b, 0, 0, 0)),
                   pl.BlockSpec((1, 1, NSA_GROUPS, HEAD_DIM, nseg), lambda s, b: (s, b, 0, 0, 0))),
        out_shape=(jax.ShapeDtypeStruct((2, batch, NSA_GROUPS, nseg, HEAD_DIM), BF16),
                   jax.ShapeDtypeStruct((2, batch, NSA_GROUPS, HEAD_DIM, nseg), BF16)),
        compiler_params=pltpu.CompilerParams(
            dimension_semantics=("parallel", "parallel"), vmem_limit_bytes=VMEM_LIMIT),
        name="compress",
    )(segs, w1a, w1b, pa, pb, w2)


def _tile_heads(a):
    return jnp.concatenate([a] * NSA_REP, axis=1)


def _nsa_body(q_ref, kcc_ref, vcct_ref, ks_ref, kw_ref, vst_ref, vwt_ref, gate_ref, ovl_ref, prev_ref,
              o_ref, imp_ref, selb_ref, qa_ref, sa_ref, sb_ref, p_ref, m_ref, l_ref, acc_ref,
              oc_ref, p3_ref, den3_ref, *, q_lo, q_hi):
    del prev_ref
    tiles = range(NSA_TILES)
    first = q_lo + pl.program_id(2) * (NSA_TILES * Q_TILE)
    starts = [first + t * Q_TILE for t in tiles]
    lanes = lambda t: slice(t * Q_TILE, (t + 1) * Q_TILE)
    q_t = [jnp.concatenate([q_ref[0, r, :, lanes(t)] for r in range(NSA_REP)], axis=1) for t in tiles]
    tq = [starts[t] + lax.broadcasted_iota(jnp.int32, (1, Q_TILE), 1) for t in tiles]

    kcc = kcc_ref[0, 0, 0]
    ncb = kcc.shape[0]
    wkeys = WINDOW + Q_TILE
    w0 = [pl.multiple_of(jnp.maximum(starts[t] - WINDOW, 0), Q_TILE) for t in tiles]
    s1 = [_dot(kcc, q_t[t]) for t in tiles]
    s3 = [_dot(kw_ref[0, 0, 0, pl.ds(w0[t], wkeys), :], q_t[t]) for t in tiles]

    cend = lax.broadcasted_iota(jnp.int32, (ncb, 1), 0) * CMP_STRIDE + (CMP_BLOCK - 1)
    p1 = []
    for t in tiles:
        sm1 = s1[t] + _tile_heads(jnp.where(cend <= tq[t], 0.0, NEG))
        e1 = jnp.exp2(sm1 - jnp.max(sm1, axis=0, keepdims=True))
        den1 = jnp.sum(e1, axis=0, keepdims=True)
        p1.append(e1 * jnp.where(_tile_heads(tq[t] >= CMP_BLOCK - 1), 1.0 / den1, 0.0))
    ovl = ovl_ref[...]
    nsb = ovl.shape[0]
    imp = []
    for t in tiles:
        oc_ref[t] = _dot(vcct_ref[0, 0, 0], p1[t].astype(BF16))
        p1sum = p1[t][:, 0:Q_TILE]
        for r in range(1, NSA_REP):
            p1sum = p1sum + p1[t][:, r * Q_TILE:(r + 1) * Q_TILE]
        ps_hi, ps_lo = _split_bf16(p1sum, 2)
        imp.append(_dot(ovl, ps_hi) + _dot(ovl, ps_lo))

    jb = lax.broadcasted_iota(jnp.int32, (nsb, Q_TILE), 0)
    valid, impm = [], []
    for t in tiles:
        cur = tq[t] // SEL_BLOCK
        forced = (jb == 0) | (jb == cur) | (jb == cur - 1)
        valid.append(jb * SEL_BLOCK <= tq[t])
        impm.append(jnp.where(valid[t], jnp.where(forced, jnp.inf, imp[t]), -jnp.inf))
        imp_ref[t] = impm[t]
    sub = lax.broadcasted_iota(jnp.int32, (8, Q_TILE), 0)
    ranks = [[jnp.zeros((8, Q_TILE), F32) for _ in range(nsb // 8)] for _ in tiles]
    seen = q_hi // SEL_BLOCK
    for i in range(seen if seen > SEL_TOPN else 0):
        for t in tiles:
            xi = jnp.broadcast_to(imp_ref[t, i:i + 1, :], (8, Q_TILE))
            for v in range(seen // 8):
                xv = impm[t][8 * v:8 * v + 8, :]
                ge = jnp.where(xi >= xv, 1.0, 0.0)
                gt = jnp.where(xi > xv, 1.0, 0.0)
                if 8 * v > i:
                    inc = ge
                elif 8 * v + 7 < i:
                    inc = gt
                else:
                    inc = jnp.where(sub + 8 * v > i, ge, gt)
                ranks[t][v] = ranks[t][v] + inc
    for t in tiles:
        rank = jnp.concatenate(ranks[t], axis=0)
        selb_ref[t] = jnp.where((rank < SEL_TOPN) & valid[t], 0.0, NEG)

    blocks_per_chunk = SEL_CHUNK // SEL_BLOCK
    last_chunk = ks_ref.shape[3] // SEL_CHUNK - 1
    n_chunks = (first + NSA_TILES * Q_TILE + SEL_CHUNK - 1) // SEL_CHUNK
    own_row = [pl.multiple_of(starts[t] - (n_chunks - 1) * SEL_CHUNK, Q_TILE) for t in tiles]
    own_mask = _tile_heads(jnp.where(
        lax.broadcasted_iota(jnp.int32, (Q_TILE, Q_TILE), 0) <= lax.broadcasted_iota(jnp.int32, (Q_TILE, Q_TILE), 1),
        0.0, 1.0))
    bias_pad = jnp.zeros((HEAD_DIM - blocks_per_chunk, NSA_REP * Q_TILE), F32)

    def scores(t, chunk):
        blk_bias = _tile_heads(selb_ref[t, pl.ds(pl.multiple_of(chunk * blocks_per_chunk, blocks_per_chunk),
                                                 blocks_per_chunk), :])
        qa_ref[t, HEAD_DIM:, :] = jnp.concatenate([blk_bias, bias_pad], axis=0).astype(BF16)
        k0 = pl.multiple_of(chunk * SEL_CHUNK, SEL_CHUNK)
        return _dot(ks_ref[0, 0, 0, pl.ds(k0, SEL_CHUNK), :], qa_ref[t])

    def put_scores(t, chunk, s_ref):
        s_ref[t] = scores(t, chunk)
        own_neg = jnp.where(chunk == n_chunks - 1, NEG, 0.0)
        s_ref[t, pl.ds(own_row[t], Q_TILE), :] = s_ref[t, pl.ds(own_row[t], Q_TILE), :] + own_mask * own_neg

    def sel_step(c, s_cur_ref, s_next_ref):
        m_old, m_new = [], []
        for t in tiles:
            m_old.append(m_ref[t])
            m_new.append(jnp.maximum(m_old[t], jnp.max(s_cur_ref[t], axis=0, keepdims=True)))
        nxt = jnp.minimum(c + 1, last_chunk)
        for t in tiles:
            put_scores(t, nxt, s_next_ref)
        v0 = pl.multiple_of(jnp.maximum(c - 1, 0) * SEL_CHUNK, SEL_CHUNK)
        v_prev = vst_ref[0, 0, 0, :, pl.ds(v0, SEL_CHUNK)]
        pv = [_dot(v_prev, p_ref[t]) for t in tiles]
        for t in tiles:
            p = jnp.exp2(s_cur_ref[t] - m_new[t])
            alpha = jnp.exp2(m_old[t] - m_new[t])
            l_ref[t] = alpha * l_ref[t] + jnp.sum(p, axis=0, keepdims=True)
            acc_ref[t] = alpha * (acc_ref[t] + pv[t])
            m_ref[t] = m_new[t]
            p_ref[t] = p.astype(BF16)

    for t in tiles:
        qa_ref[t, 0:HEAD_DIM, :] = q_t[t]
        put_scores(t, 0, sa_ref)
    p_ref[...] = jnp.zeros_like(p_ref)

    for t in tiles:
        dist = tq[t] - (w0[t] + lax.broadcasted_iota(jnp.int32, (wkeys, 1), 0))
        sm3 = s3[t] + _tile_heads(jnp.where((dist >= 0) & (dist < WINDOW), 0.0, NEG))
        e3 = jnp.exp2(sm3 - jnp.max(sm3, axis=0, keepdims=True))
        den3_ref[t] = jnp.sum(e3, axis=0, keepdims=True)
        p3_ref[t] = e3.astype(BF16)

    m_ref[...] = jnp.full(m_ref.shape, NEG, F32)
    l_ref[...] = jnp.zeros_like(l_ref)
    acc_ref[...] = jnp.zeros_like(acc_ref)

    def sel_pair(i, carry):
        sel_step(2 * i, sa_ref, sb_ref)

        @pl.when(2 * i + 1 < n_chunks)
        def _():
            sel_step(2 * i + 1, sb_ref, sa_ref)
        return carry

    lax.fori_loop(0, (n_chunks + 1) // 2, sel_pair, 0)

    vl = pl.multiple_of((n_chunks - 1) * SEL_CHUNK, SEL_CHUNK)
    v_last = vst_ref[0, 0, 0, :, pl.ds(vl, SEL_CHUNK)]
    acc_w = [_dot(vwt_ref[0, 0, 0, :, pl.ds(w0[t], wkeys)], p3_ref[t]) for t in tiles]
    acc_s = [acc_ref[t] + _dot(v_last, p_ref[t]) for t in tiles]
    for t in tiles:
        gates = gate_ref[0, :, lanes(t)]
        g_c = jnp.concatenate([gates[3 * r:3 * r + 1, :] for r in range(NSA_REP)], axis=1)
        g_s = jnp.concatenate([gates[3 * r + 1:3 * r + 2, :] for r in range(NSA_REP)], axis=1)
        g_w = jnp.concatenate([gates[3 * r + 2:3 * r + 3, :] for r in range(NSA_REP)], axis=1)
        out_t = g_c * oc_ref[t] + (g_w / den3_ref[t]) * acc_w[t] + (g_s / l_ref[t]) * acc_s[t]
        out_t = jnp.concatenate([out_t[:, r * Q_TILE:(r + 1) * Q_TILE] for r in range(NSA_REP)], axis=0)
        o_ref[lanes(t), :] = out_t.T.astype(o_ref.dtype)


def _nsa(q_t, cc, cc_t, ksel, kwin, vt, gates_t, ovl, prev, batch, seq, q_lo, q_hi):
    span = NSA_TILES * Q_TILE
    nqb = (q_hi - q_lo) // span
    off = q_lo // span
    nq_all = seq // span
    ncb_pad = cc.shape[3]
    nsb = seq // SEL_BLOCK
    rows = NSA_REP * Q_TILE
    vspec = lambda which: pl.BlockSpec((1, 1, 1, HEAD_DIM, seq), lambda b, g, i: (which, b, g, 0, 0))
    per_tile = lambda shape, dtype: pltpu.VMEM((NSA_TILES,) + shape, dtype)
    return pl.pallas_call(
        functools.partial(_nsa_body, q_lo=q_lo, q_hi=q_hi),
        grid=(batch, NSA_GROUPS, nqb),
        in_specs=[
            pl.BlockSpec((1, NSA_REP, HEAD_DIM, span), lambda b, g, i: (b, g, 0, off + i)),
            pl.BlockSpec((1, 1, 1, ncb_pad, HEAD_DIM), lambda b, g, i: (0, b, g, 0, 0)),
            pl.BlockSpec((1, 1, 1, HEAD_DIM, ncb_pad), lambda b, g, i: (1, b, g, 0, 0)),
            pl.BlockSpec((1, 1, 1, seq, LANES), lambda b, g, i: (0, b, g, 0, 0)),
            pl.BlockSpec((1, 1, 1, seq, HEAD_DIM), lambda b, g, i: (0, b, g, 0, 0)),
            vspec(0), vspec(1),
            pl.BlockSpec((1, GATE_ROWS, span), lambda b, g, i: (g, 0, b * nq_all + off + i)),
            pl.BlockSpec(ovl.shape, lambda b, g, i: (0, 0)),
            pl.BlockSpec(memory_space=pl.ANY),
        ],
        out_specs=pl.BlockSpec((span, NSA_REP * HEAD_DIM), lambda b, g, i: (b * nq_all + off + i, g)),
        out_shape=jax.ShapeDtypeStruct(prev.shape, prev.dtype),
        input_output_aliases={9: 0},
        scratch_shapes=[per_tile((nsb, Q_TILE), F32), per_tile((nsb, Q_TILE), F32),
                        per_tile((2 * HEAD_DIM, rows), BF16),
                        per_tile((SEL_CHUNK, rows), F32), per_tile((SEL_CHUNK, rows), F32),
                        per_tile((SEL_CHUNK, rows), BF16),
                        per_tile((1, rows), F32), per_tile((1, rows), F32),
                        per_tile((HEAD_DIM, rows), F32), per_tile((HEAD_DIM, rows), F32),
                        per_tile((WINDOW + Q_TILE, rows), BF16), per_tile((1, rows), F32)],
        compiler_params=pltpu.CompilerParams(
            dimension_semantics=("parallel", "parallel", "arbitrary"), vmem_limit_bytes=VMEM_LIMIT),
        name="nsa",
    )(q_t, cc, cc_t, ksel[None], kwin[None], vt, vt, gates_t, ovl, prev)


def _hgrn_body(hq_ref, hf_ref, hi_ref, hg_ref, lbl_ref, gain_ref, o_ref,
               state_ref, b_ref, k_ref, v_ref, q_ref, *, layer):
    @pl.when(pl.program_id(1) == 0)
    def _():
        state_ref[...] = jnp.zeros_like(state_ref)

    logits = lbl_ref[...]
    ex = jnp.exp(logits - jnp.max(logits, axis=0, keepdims=True))
    sm = ex / jnp.sum(ex, axis=0, keepdims=True)
    lb_all = jnp.zeros((1, HG_HEADS * HG_DK), F32)
    for i in range(1, layer + 1):
        lb_all = lb_all + sm[i:i + 1, :]

    ch, blk, half = HG_CHUNK, HG_SUB, HG_SUB // 2
    nblk = ch // blk
    ri = lax.broadcasted_iota(jnp.int32, (ch, ch), 0)
    ci = lax.broadcasted_iota(jnp.int32, (ch, ch), 1)
    tril = jnp.where(ri >= ci, 1.0, 0.0).astype(BF16)
    blk_row = lax.broadcasted_iota(jnp.int32, (blk, HG_DK), 0)
    low_rows = blk_row >= half
    rr = lax.broadcasted_iota(jnp.int32, (blk, blk), 0)
    cc = lax.broadcasted_iota(jnp.int32, (blk, blk), 1)
    cross = (rr >= half) & (cc < half)
    half_row = lax.broadcasted_iota(jnp.int32, (half, HG_DK), 0)
    gain = gain_ref[...]
    heads = range(HG_HEADS)

    def chunk(j, carry):
        r0 = pl.multiple_of(j * ch, ch)
        for h in heads:
            cols = slice(h * HG_DK, (h + 1) * HG_DK)
            lb = lb_all[:, cols]
            hq = hq_ref[0, pl.ds(r0, ch), cols]
            hf = hf_ref[0, pl.ds(r0, ch), cols]
            f = lb + (1.0 - lb) * jax.nn.sigmoid(hf)
            lf = _split_bf16(jnp.log(jnp.maximum(f, 1e-30)), 3)
            b_ref[h] = _dot(tril, lf[0]) + _dot(tril, lf[1]) + _dot(tril, lf[2])
            k_ref[h] = 1.0 - f
            v_ref[h] = hi_ref[0, pl.ds(r0, ch), cols]
            q_ref[h] = hq * jax.nn.sigmoid(hq)

        o_blocks, tables = [], []
        for h in heads:
            inter = _dot_nt((q_ref[h] * jnp.exp(b_ref[h])).astype(BF16), state_ref[h].astype(BF16))
            o_blocks.append([inter[ib * blk:(ib + 1) * blk, :] for ib in range(nblk)])
            tables.append([])
            for ib in range(nblk):
                i0 = ib * blk
                qi = q_ref[h, i0:i0 + blk, :]
                bi = b_ref[h, i0:i0 + blk, :]
                a = None
                if ib > 0:
                    ref_b = b_ref[h, i0 - 1:i0, :]
                    qd = (qi * jnp.exp(bi - ref_b)).astype(BF16)
                    kd = (k_ref[h, 0:i0, :] * jnp.exp(ref_b - b_ref[h, 0:i0, :])).astype(BF16)
                    a = _dot_nt(qd, kd).astype(BF16)
                mid_b = b_ref[h, i0 + half - 1:i0 + half, :]
                qd2 = (qi * jnp.exp(jnp.where(low_rows, bi - mid_b, 0.0))).astype(BF16)
                kd2 = (k_ref[h, i0:i0 + blk, :] * jnp.exp(jnp.where(low_rows, 0.0, mid_b - bi))).astype(BF16)
                a2 = jnp.where(cross, _dot_nt(qd2, kd2), 0.0).astype(BF16)
                tables[h].append((a, a2))
        for h in heads:
            bl = b_ref[h, ch - 1:ch, :]
            kd_end = (k_ref[h] * jnp.exp(bl - b_ref[h])).astype(BF16)
            vt = v_ref[h].T.astype(BF16)
            state_ref[h] = state_ref[h] * jnp.exp(bl) + _dot(vt, kd_end)

        for h in heads:
            for ib in range(nblk):
                i0 = ib * blk
                a, a2 = tables[h][ib]
                o_i = o_blocks[h][ib] + _dot(a2, v_ref[h, i0:i0 + blk, :].astype(BF16))
                if a is not None:
                    o_i = o_i + _dot(a, v_ref[h, 0:i0, :].astype(BF16))
                o_blocks[h][ib] = o_i

        nhalf = ch // half
        diag = [[jnp.zeros((half, HG_DV), F32) for _ in range(nhalf)] for _ in heads]
        for s in range(half):
            for h in heads:
                for ih in range(nhalf):
                    g0 = ih * half
                    bs = b_ref[h, g0 + s:g0 + s + 1, :]
                    ks = k_ref[h, g0 + s:g0 + s + 1, :]
                    vs = v_ref[h, g0 + s:g0 + s + 1, :]
                    bi = b_ref[h, g0:g0 + half, :]
                    qi = q_ref[h, g0:g0 + half, :]
                    dec = jnp.exp(jnp.where(half_row >= s, bi - bs, NEG))
                    a_s = jnp.sum(qi * dec * ks, axis=-1, keepdims=True)
                    diag[h][ih] = diag[h][ih] + a_s * vs

        for h in heads:
            cols = slice(h * HG_DK, (h + 1) * HG_DK)
            o = jnp.concatenate(o_blocks[h], axis=0) + jnp.concatenate(diag[h], axis=0)
            o = o * lax.rsqrt(jnp.mean(o * o, axis=-1, keepdims=True) + EPS) * gain
            hg = hg_ref[0, pl.ds(r0, ch), cols]
            o_ref[pl.ds(r0, ch), cols] = (o * (hg * jax.nn.sigmoid(hg))).astype(o_ref.dtype)
        return carry

    lax.fori_loop(0, hq_ref.shape[1] // ch, chunk, 0)


def _hgrn(h_all, lb_logits, out_gain, layer, batch, seq):
    t = h_all.shape[1]
    nblk = seq // HG_ROWS
    nl = lb_logits.shape[0]
    hw = HG_HEADS * HG_DK
    hspec = lambda which: pl.BlockSpec((1, HG_ROWS, hw), lambda b, c: (which, b * nblk + c, 0))
    return pl.pallas_call(
        functools.partial(_hgrn_body, layer=layer),
        grid=(batch, nblk),
        in_specs=[hspec(0), hspec(1), hspec(2), hspec(3),
                  pl.BlockSpec((nl, hw), lambda b, c: (0, 0)),
                  pl.BlockSpec((1, HG_DV), lambda b, c: (0, 0))],
        out_specs=pl.BlockSpec((HG_ROWS, HG_HEADS * HG_DV), lambda b, c: (b * nblk + c, 0)),
        out_shape=jax.ShapeDtypeStruct((t, HG_HEADS * HG_DV), BF16),
        scratch_shapes=[pltpu.VMEM((HG_HEADS, HG_DV, HG_DK), F32)]
        + [pltpu.VMEM((HG_HEADS, HG_CHUNK, HG_DK), F32)] * 4,
        compiler_params=pltpu.CompilerParams(
            dimension_semantics=("parallel", "arbitrary"), vmem_limit_bytes=VMEM_LIMIT),
        name="hgrn",
    )(h_all, h_all, h_all, h_all, lb_logits, out_gain.reshape(1, HG_DV))


def _relayout_w_in(w):
    d = w.shape[0]
    nq = NSA_HEADS * HEAD_DIM
    nkv = 6 * NSA_GROUPS * HEAD_DIM
    ng = 3 * NSA_HEADS
    q_kv = w[:, :nq + nkv]
    gts = w[:, nq + nkv:nq + nkv + ng]
    hg = w[:, nq + nkv + ng:]
    per = 3 * NSA_REP
    gate_blocks = [jnp.pad(gts[:, gi * per:(gi + 1) * per], ((0, 0), (0, LANES - per)))
                   for gi in range(NSA_GROUPS)]
    out = jnp.concatenate([q_kv, hg] + gate_blocks, axis=1).astype(BF16)
    assert out.shape == (d, _C_END)
    return out


def _compress_weights(pos, w1, w2):
    eye = jnp.eye(NSA_GROUPS, dtype=F32)
    e = w1.shape[-1]

    def big(w1_half):
        y = jnp.einsum("sldh,gk->slgdkh", w1_half, eye)
        return y.reshape(2, CMP_STRIDE * NSA_GROUPS * HEAD_DIM, NSA_GROUPS * e).astype(BF16)

    def pos_rows(p_half):
        y = jnp.broadcast_to(p_half[:, :, None, :], (2, CMP_STRIDE, NSA_GROUPS, HEAD_DIM))
        return y.reshape(2, 1, CMP_STRIDE * NSA_GROUPS * HEAD_DIM)

    w2b = jnp.einsum("shd,gk->sghkd", w2, eye).reshape(2, NSA_GROUPS * e, NSA_GROUPS * HEAD_DIM)
    return (big(w1[:, :CMP_STRIDE]), big(w1[:, CMP_STRIDE:]),
            pos_rows(pos[:, :CMP_STRIDE]), pos_rows(pos[:, CMP_STRIDE:]), w2b.astype(BF16))


def kernel(x, ffn1_norm, ffn1_w_gate, ffn1_w_up, ffn1_w_down, mix_norm, w_in, q_norm, k_norm,
           cmp_pos, cmp_w1, cmp_w2, hgrn_lb_logits, hgrn_out_norm, w_out,
           ffn2_norm, ffn2_w_gate, ffn2_w_up, ffn2_w_down):
    batch, seq, d = x.shape
    depth = w_in.shape[0]
    assert CMP_BLOCK == 2 * CMP_STRIDE and seq % SEL_CHUNK == 0 and seq >= WINDOW + Q_TILE
    assert seq % HG_ROWS == 0 and SEL_CHUNK % SEL_BLOCK == 0 and SEL_CHUNK % (NSA_TILES * Q_TILE) == 0
    assert seq % (2 * SEL_CHUNK) == 0
    t = batch * seq
    tm_proj = 512

    inv = 1.0 / (ROPE_THETA ** (jnp.arange(0, HEAD_DIM, 2, dtype=F32) / HEAD_DIM))
    ang = jnp.arange(seq, dtype=F32)[:, None] * inv[None, :]
    ang = jnp.concatenate([ang, ang], axis=-1)
    cos1, sin1 = jnp.cos(ang), jnp.sin(ang)
    cos2, sin2 = jnp.tile(cos1, (1, 2)), jnp.tile(sin1, (1, 2))
    cost, sint = cos1.T, sin1.T

    nseg = seq // CMP_STRIDE
    nsb = seq // SEL_BLOCK
    ci = np.arange(nseg)[None, :]
    sj = np.arange(nsb)[:, None]
    ovl = ((ci * CMP_STRIDE <= sj * SEL_BLOCK + SEL_BLOCK - 1)
           & (ci * CMP_STRIDE + CMP_BLOCK - 1 >= sj * SEL_BLOCK)
           & (ci < (seq - CMP_BLOCK) // CMP_STRIDE + 1))
    ovl = jnp.asarray(ovl, dtype=BF16)

    xt = x.reshape(t, d)
    for l in range(depth):
        xt = _ffn(xt, ffn1_norm, ffn1_w_gate, ffn1_w_up, ffn1_w_down, l)
        qgt = jnp.broadcast_to(q_norm[l][:, None], (HEAD_DIM, tm_proj))
        kg2 = jnp.tile(k_norm[l], (1, 2))
        q_t, cmp_in, ksel, kwin, vt, gates_t, h_all = _inproj(
            xt, mix_norm[l], _relayout_w_in(w_in[l]), cos2, sin2, cost, sint, qgt, kg2,
            batch, seq, tm=tm_proj)
        w1a, w1b, pa, pb, w2b = _compress_weights(cmp_pos[l], cmp_w1[l], cmp_w2[l])
        cc, cc_t = _compress(cmp_in, w1a, w1b, pa, pb, w2b, batch, seq)
        qr = 2 * SEL_CHUNK
        o_nsa = jnp.zeros((t, NSA_HEADS * HEAD_DIM), BF16)
        for lo in range(0, seq, qr):
            o_nsa = _nsa(q_t, cc, cc_t, ksel, kwin, vt, gates_t, ovl, o_nsa, batch, seq, lo, lo + qr)
        o_hg = _hgrn(h_all, hgrn_lb_logits, hgrn_out_norm[l], l, batch, seq)
        xt = _ffn(xt, ffn2_norm, ffn2_w_gate, ffn2_w_up, ffn2_w_down, l, mix=(o_nsa, o_hg, w_out))
    return xt.reshape(batch, seq, d)
```

```python
import functools

import jax
import jax.numpy as jnp
import numpy as np
from jax import lax
from jax.experimental import pallas as pl
from jax.experimental.pallas import tpu as pltpu

NSA_HEADS = 8
NSA_GROUPS = 2
NSA_REP = NSA_HEADS // NSA_GROUPS
HEAD_DIM = 64
CMP_BLOCK = 32
CMP_STRIDE = 16
CMP_HIDDEN = 128
SEL_BLOCK = 64
SEL_TOPN = 16
WINDOW = 512
HG_HEADS = 4
HG_DK = 128
HG_DV = 128
ROPE_THETA = 10000.0
EPS = 1e-6
NEG = -1e30
LOG2E = 1.4426950408889634

LANES = 128
Q_TILE = 128
NSA_TILES = 2
SEL_CHUNK = 512
HG_CHUNK = 64
HG_SUB = 16
HG_ROWS = 512
VMEM_LIMIT = 56 * 1024 * 1024

BF16 = jnp.bfloat16
F32 = jnp.float32


def _dot(a, b):
    return jnp.dot(a, b, preferred_element_type=F32)


def _dot_nt(a, b):
    return lax.dot_general(a, b, (((1,), (1,)), ((), ())), preferred_element_type=F32)


def _split_bf16(x, parts):
    out = []
    rem = x
    for _ in range(parts):
        p = rem.astype(BF16)
        out.append(p)
        rem = rem - p.astype(F32)
    return out


def _ffn_body(*refs, fuse_mix):
    if fuse_mix:
        (x_ref, ma_ref, mb_ref, woa_ref, wob_ref, g_ref, wg_ref, wu_ref, wd_ref,
         o_ref, xs_ref, hn_ref, acc_ref) = refs
    else:
        x_ref, g_ref, wg_ref, wu_ref, wd_ref, o_ref, xs_ref, hn_ref, acc_ref = refs
    f = pl.program_id(1)

    @pl.when(f == 0)
    def _():
        x = x_ref[...]
        if fuse_mix:
            x = x + _dot(ma_ref[...], woa_ref[0].astype(BF16)) + _dot(mb_ref[...], wob_ref[0].astype(BF16))
        xs_ref[...] = x
        ms = jnp.mean(x * x, axis=-1, keepdims=True)
        hn_ref[...] = (x * lax.rsqrt(ms + EPS) * g_ref[0]).astype(BF16)
        acc_ref[...] = jnp.zeros_like(acc_ref)

    hn = hn_ref[...]
    gate = _dot(hn, wg_ref[0].astype(BF16))
    up = _dot(hn, wu_ref[0].astype(BF16))
    act = (gate * jax.nn.sigmoid(gate) * up).astype(BF16)
    acc_ref[...] += _dot(act, wd_ref[0].astype(BF16))

    @pl.when(f == pl.num_programs(1) - 1)
    def _():
        o_ref[...] = xs_ref[...] + 0.5 * acc_ref[...]


def _ffn(x, g, wg, wu, wd, layer, mix=None, *, tm=1024, tf=256):
    t, d = x.shape
    ff = wg.shape[2]
    assert t % tm == 0 and ff % tf == 0
    fuse = mix is not None
    row = lambda i, f: (i, 0)
    in_specs = [pl.BlockSpec((tm, d), row)]
    args = [x]
    if fuse:
        ma, mb, wo = mix
        da, db = ma.shape[1], mb.shape[1]
        assert da == db and wo.shape[1] == da + db
        in_specs += [pl.BlockSpec((tm, da), row), pl.BlockSpec((tm, db), row),
                     pl.BlockSpec((1, da, d), lambda i, f: (layer, 0, 0)),
                     pl.BlockSpec((1, db, d), lambda i, f: (layer, 1, 0))]
        args += [ma, mb, wo, wo]
    in_specs += [pl.BlockSpec((1, 1, d), lambda i, f: (layer, 0, 0)),
                 pl.BlockSpec((1, d, tf), lambda i, f: (layer, 0, f)),
                 pl.BlockSpec((1, d, tf), lambda i, f: (layer, 0, f)),
                 pl.BlockSpec((1, tf, d), lambda i, f: (layer, f, 0))]
    args += [g.reshape(g.shape[0], 1, d), wg, wu, wd]
    return pl.pallas_call(
        functools.partial(_ffn_body, fuse_mix=fuse),
        grid=(t // tm, ff // tf),
        in_specs=in_specs,
        out_specs=pl.BlockSpec((tm, d), row),
        out_shape=jax.ShapeDtypeStruct((t, d), F32),
        scratch_shapes=[pltpu.VMEM((tm, d), F32), pltpu.VMEM((tm, d), BF16), pltpu.VMEM((tm, d), F32)],
        compiler_params=pltpu.CompilerParams(
            dimension_semantics=("parallel", "arbitrary"), vmem_limit_bytes=VMEM_LIMIT),
        name="ffn_mix" if fuse else "ffn",
    )(*args)


_C_Q = 0
_C_KV = NSA_HEADS * HEAD_DIM
_C_HG = _C_KV + 6 * LANES
_C_GATE = _C_HG + 4 * HG_HEADS * HG_DK
_C_END = _C_GATE + NSA_GROUPS * LANES
GATE_ROWS = 16
Q_SCALE = HEAD_DIM ** -0.5 * LOG2E


def _inproj_body(x_ref, g_ref, w_ref, cos_ref, sin_ref, cost_ref, sint_ref, qg_ref, kg_ref,
                 q_ref, cmp_ref, ksel_ref, kwin_ref, vt_ref, gate_ref, h_ref, seg_ref, *, tiles_per_seq):
    x = x_ref[...]
    ms = jnp.mean(x * x, axis=-1, keepdims=True)
    hn = (x * lax.rsqrt(ms + EPS) * g_ref[...]).astype(BF16)
    cos = cos_ref[...]
    sin = sin_ref[...]
    tm = x.shape[0]
    half = HEAD_DIM // 2

    def proj(c0, width):
        return _dot(hn, w_ref[:, c0:c0 + width])

    ri = (lax.broadcasted_iota(jnp.int32, (2 * LANES, LANES), 0) % LANES) // HEAD_DIM
    ci = lax.broadcasted_iota(jnp.int32, (2 * LANES, LANES), 1) // HEAD_DIM
    head_mean = jnp.where(ri == ci, 1.0 / HEAD_DIM, 0.0).astype(BF16)
    lane = lax.broadcasted_iota(jnp.int32, (tm, LANES), 1)
    first_half = (lane % HEAD_DIM) < half

    def norm_rope(y, gain):
        msq = _dot(jnp.concatenate(_split_bf16(y * y, 2), axis=1), head_mean)
        yn = y * lax.rsqrt(msq + EPS) * gain
        rot = jnp.where(first_half, -pltpu.roll(yn, LANES - half, 1), pltpu.roll(yn, half, 1))
        return yn * cos + rot * sin

    def heads_t(y):
        return y.T.reshape(2, HEAD_DIM, tm)

    qgain = (qg_ref[...] * Q_SCALE)[None]
    cost = cost_ref[...][None]
    sint = sint_ref[...][None]
    yq = proj(_C_Q, NSA_HEADS * HEAD_DIM)
    ykv = proj(_C_KV, 6 * LANES)
    kv = lambda j: ykv[:, j * LANES:(j + 1) * LANES]
    for c in range(NSA_HEADS // 2):
        y3 = heads_t(yq[:, c * LANES:(c + 1) * LANES])
        yn = y3 * lax.rsqrt(jnp.mean(y3 * y3, axis=1, keepdims=True) + EPS) * qgain
        rot = jnp.concatenate([-yn[:, half:], yn[:, :half]], axis=1)
        q_ref[0, 2 * c:2 * c + 2] = (yn * cost + rot * sint).astype(BF16)

    seg_ref[0] = norm_rope(kv(0), kg_ref[0:1, :])
    seg_ref[1] = kv(1)
    nseg_t = tm // CMP_STRIDE
    for s in range(2):
        for l in range(CMP_STRIDE):
            cmp_ref[s, :, l * LANES:(l + 1) * LANES] = seg_ref[s, pl.ds(l, nseg_t, stride=CMP_STRIDE), :].astype(BF16)
    ksel = norm_rope(kv(2), kg_ref[1:2, :])
    pos = (pl.program_id(0) % tiles_per_seq) * tm + lax.broadcasted_iota(jnp.int32, (tm, HEAD_DIM), 0)
    blk_in_chunk = (pos // SEL_BLOCK) % (SEL_CHUNK // SEL_BLOCK)
    onehot = jnp.where(lax.broadcasted_iota(jnp.int32, (tm, HEAD_DIM), 1) == blk_in_chunk, 1.0, 0.0)
    for gi in range(NSA_GROUPS):
        ksel_ref[0, gi] = jnp.concatenate(
            [ksel[:, gi * HEAD_DIM:(gi + 1) * HEAD_DIM], onehot], axis=1).astype(BF16)
    kwin = norm_rope(kv(4), kg_ref[2:3, :]).astype(BF16)
    kwin_ref[0, 0] = kwin[:, :HEAD_DIM]
    kwin_ref[0, 1] = kwin[:, HEAD_DIM:]
    for which, col in ((0, 3), (1, 5)):
        vt_ref[which, 0] = heads_t(kv(col)).astype(BF16)

    hw = HG_HEADS * HG_DK
    for i in range(4):
        h_ref[i] = proj(_C_HG + i * hw, hw)
    ygate = proj(_C_GATE, NSA_GROUPS * LANES)
    for gi in range(NSA_GROUPS):
        gt = jax.nn.sigmoid(ygate[:, gi * LANES:(gi + 1) * LANES]).T
        gate_ref[gi] = gt[:GATE_ROWS]


def _inproj(x, g, w2, cos2, sin2, cost, sint, qgt, kg2, batch, seq, *, tm=256):
    t, d = x.shape
    nst = seq // tm
    hw = HG_HEADS * HG_DK
    out_shape = (
        jax.ShapeDtypeStruct((batch, NSA_HEADS, HEAD_DIM, seq), BF16),
        jax.ShapeDtypeStruct((2, t // CMP_STRIDE, CMP_STRIDE * LANES), BF16),
        jax.ShapeDtypeStruct((batch, NSA_GROUPS, seq, LANES), BF16),
        jax.ShapeDtypeStruct((batch, NSA_GROUPS, seq, HEAD_DIM), BF16),
        jax.ShapeDtypeStruct((2, batch, NSA_GROUPS, HEAD_DIM, seq), BF16),
        jax.ShapeDtypeStruct((NSA_GROUPS, GATE_ROWS, t), F32),
        jax.ShapeDtypeStruct((4, t, hw), F32),
    )
    out_specs = (
        pl.BlockSpec((1, NSA_HEADS, HEAD_DIM, tm), lambda i: (i // nst, 0, 0, i % nst)),
        pl.BlockSpec((2, tm // CMP_STRIDE, CMP_STRIDE * LANES), lambda i: (0, i, 0)),
        pl.BlockSpec((1, NSA_GROUPS, tm, LANES), lambda i: (i // nst, 0, i % nst, 0)),
        pl.BlockSpec((1, NSA_GROUPS, tm, HEAD_DIM), lambda i: (i // nst, 0, i % nst, 0)),
        pl.BlockSpec((2, 1, NSA_GROUPS, HEAD_DIM, tm), lambda i: (0, i // nst, 0, 0, i % nst)),
        pl.BlockSpec((NSA_GROUPS, GATE_ROWS, tm), lambda i: (0, 0, i)),
        pl.BlockSpec((4, tm, hw), lambda i: (0, i, 0)),
    )
    in_specs = [
        pl.BlockSpec((tm, d), lambda i: (i, 0)),
        pl.BlockSpec((1, d), lambda i: (0, 0)),
        pl.BlockSpec(w2.shape, lambda i: (0, 0)),
        pl.BlockSpec((tm, LANES), lambda i: (i % nst, 0)),
        pl.BlockSpec((tm, LANES), lambda i: (i % nst, 0)),
        pl.BlockSpec((HEAD_DIM, tm), lambda i: (0, i % nst)),
        pl.BlockSpec((HEAD_DIM, tm), lambda i: (0, i % nst)),
        pl.BlockSpec((HEAD_DIM, tm), lambda i: (0, 0)),
        pl.BlockSpec((3, LANES), lambda i: (0, 0)),
    ]
    return pl.pallas_call(
        functools.partial(_inproj_body, tiles_per_seq=nst),
        grid=(t // tm,),
        in_specs=in_specs,
        out_specs=out_specs,
        out_shape=out_shape,
        scratch_shapes=[pltpu.VMEM((2, tm, LANES), F32)],
        compiler_params=pltpu.CompilerParams(
            dimension_semantics=("parallel",), vmem_limit_bytes=VMEM_LIMIT),
        name="inproj",
    )(x, g.reshape(1, d), w2, cos2, sin2, cost, sint, qgt, kg2)


def _compress_body(seg_ref, w1a_ref, w1b_ref, pa_ref, pb_ref, w2_ref, o_ref, ot_ref):
    segs = seg_ref[0, 0]
    w1a = w1a_ref[0]
    w1b = w1b_ref[0]
    first = _dot(segs, w1a)
    second = _dot(segs, w1b)
    pa_hi, pa_lo = _split_bf16(jnp.broadcast_to(pa_ref[0], (8, pa_ref.shape[2])), 2)
    pb_hi, pb_lo = _split_bf16(jnp.broadcast_to(pb_ref[0], (8, pb_ref.shape[2])), 2)
    cpos = _dot(pa_hi, w1a) + _dot(pa_lo, w1a) + _dot(pb_hi, w1b) + _dot(pb_lo, w1b)
    nseg = first.shape[0]
    pre = first + pltpu.roll(second, nseg - 1, 0) + cpos[0:1, :]
    hid = jax.nn.gelu(pre, approximate=True).astype(BF16)
    out = _dot(hid, w2_ref[0])
    outb = out.astype(BF16)
    o_ref[0, 0, 0] = outb[:, :HEAD_DIM]
    o_ref[0, 0, 1] = outb[:, HEAD_DIM:]
    ot_ref[0, 0] = out.T.reshape(NSA_GROUPS, HEAD_DIM, nseg).astype(BF16)


def _compress(cmp_in, w1a, w1b, pa, pb, w2, batch, seq):
    nseg = seq // CMP_STRIDE
    width = CMP_STRIDE * LANES
    segs = cmp_in.reshape(2, batch, nseg, width)
    return pl.pallas_call(
        _compress_body,
        grid=(2, batch),
        in_specs=[
            pl.BlockSpec((1, 1, nseg, width), lambda s, b: (s, b, 0, 0)),
            pl.BlockSpec((1,) + w1a.shape[1:], lambda s, b: (s, 0, 0)),
            pl.BlockSpec((1,) + w1b.shape[1:], lambda s, b: (s, 0, 0)),
            pl.BlockSpec((1, 1, width), lambda s, b: (s, 0, 0)),
            pl.BlockSpec((1, 1, width), lambda s, b: (s, 0, 0)),
            pl.BlockSpec((1,) + w2.shape[1:], lambda s, b: (s, 0, 0)),
        ],
        out_specs=(pl.BlockSpec((1, 1, NSA_GROUPS, nseg, HEAD_DIM), lambda s, b: (s, b, 0, 0, 0)),
                   pl.BlockSpec((1, 1, NSA_GROUPS, HEAD_DIM, nseg), lambda s, b: (s, b, 0, 0, 0))),
        out_shape=(jax.ShapeDtypeStruct((2, batch, NSA_GROUPS, nseg, HEAD_DIM), BF16),
                   jax.ShapeDtypeStruct((2, batch, NSA_GROUPS, HEAD_DIM, nseg), BF16)),
        compiler_params=pltpu.CompilerParams(
            dimension_semantics=("parallel", "parallel"), vmem_limit_bytes=VMEM_LIMIT),
        name="compress",
    )(segs, w1a, w1b, pa, pb, w2)


def _tile_heads(a):
    return jnp.concatenate([a] * NSA_REP, axis=1)


def _nsa_body(q_ref, kcc_ref, vcct_ref, ks_ref, kw_ref, vst_ref, vwt_ref, gate_ref, ovl_ref, prev_ref,
              o_ref, imp_ref, selb_ref, qa_ref, sa_ref, sb_ref, p_ref, m_ref, l_ref, acc_ref,
              oc_ref, p3_ref, den3_ref, *, q_lo, q_hi):
    del prev_ref
    tiles = range(NSA_TILES)
    first = q_lo + pl.program_id(2) * (NSA_TILES * Q_TILE)
    starts = [first + t * Q_TILE for t in tiles]
    lanes = lambda t: slice(t * Q_TILE, (t + 1) * Q_TILE)
    q_t = [jnp.concatenate([q_ref[0, r, :, lanes(t)] for r in range(NSA_REP)], axis=1) for t in tiles]
    tq = [starts[t] + lax.broadcasted_iota(jnp.int32, (1, Q_TILE), 1) for t in tiles]

    kcc = kcc_ref[0, 0, 0]
    ncb = kcc.shape[0]
    wkeys = WINDOW + Q_TILE
    w0 = [pl.multiple_of(jnp.maximum(starts[t] - WINDOW, 0), Q_TILE) for t in tiles]
    s1 = [_dot(kcc, q_t[t]) for t in tiles]
    s3 = [_dot(kw_ref[0, 0, 0, pl.ds(w0[t], wkeys), :], q_t[t]) for t in tiles]

    cend = lax.broadcasted_iota(jnp.int32, (ncb, 1), 0) * CMP_STRIDE + (CMP_BLOCK - 1)
    p1 = []
    for t in tiles:
        sm1 = s1[t] + _tile_heads(jnp.where(cend <= tq[t], 0.0, NEG))
        e1 = jnp.exp2(sm1 - jnp.max(sm1, axis=0, keepdims=True))
        den1 = jnp.sum(e1, axis=0, keepdims=True)
        p1.append(e1 * jnp.where(_tile_heads(tq[t] >= CMP_BLOCK - 1), 1.0 / den1, 0.0))
    ovl = ovl_ref[...]
    nsb = ovl.shape[0]
    imp = []
    for t in tiles:
        oc_ref[t] = _dot(vcct_ref[0, 0, 0], p1[t].astype(BF16))
        p1sum = p1[t][:, 0:Q_TILE]
        for r in range(1, NSA_REP):
            p1sum = p1sum + p1[t][:, r * Q_TILE:(r + 1) * Q_TILE]
        ps_hi, ps_lo = _split_bf16(p1sum, 2)
        imp.append(_dot(ovl, ps_hi) + _dot(ovl, ps_lo))

    jb = lax.broadcasted_iota(jnp.int32, (nsb, Q_TILE), 0)
    valid, impm = [], []
    for t in tiles:
        cur = tq[t] // SEL_BLOCK
        forced = (jb == 0) | (jb == cur) | (jb == cur - 1)
        valid.append(jb * SEL_BLOCK <= tq[t])
        impm.append(jnp.where(valid[t], jnp.where(forced, jnp.inf, imp[t]), -jnp.inf))
        imp_ref[t] = impm[t]
    sub = lax.broadcasted_iota(jnp.int32, (8, Q_TILE), 0)
    ranks = [[jnp.zeros((8, Q_TILE), F32) for _ in range(nsb // 8)] for _ in tiles]
    seen = q_hi // SEL_BLOCK
    for i in range(seen if seen > SEL_TOPN else 0):
        for t in tiles:
            xi = jnp.broadcast_to(imp_ref[t, i:i + 1, :], (8, Q_TILE))
            for v in range(seen // 8):
                xv = impm[t][8 * v:8 * v + 8, :]
                ge = jnp.where(xi >= xv, 1.0, 0.0)
                gt = jnp.where(xi > xv, 1.0, 0.0)
                if 8 * v > i:
                    inc = ge
                elif 8 * v + 7 < i:
                    inc = gt
                else:
                    inc = jnp.where(sub + 8 * v > i, ge, gt)
                ranks[t][v] = ranks[t][v] + inc
    for t in tiles:
        rank = jnp.concatenate(ranks[t], axis=0)
        selb_ref[t] = jnp.where((rank < SEL_TOPN) & valid[t], 0.0, NEG)

    blocks_per_chunk = SEL_CHUNK // SEL_BLOCK
    last_chunk = ks_ref.shape[3] // SEL_CHUNK - 1
    n_chunks = (first + NSA_TILES * Q_TILE + SEL_CHUNK - 1) // SEL_CHUNK
    own_row = [pl.multiple_of(starts[t] - (n_chunks - 1) * SEL_CHUNK, Q_TILE) for t in tiles]
    own_mask = _tile_heads(jnp.where(
        lax.broadcasted_iota(jnp.int32, (Q_TILE, Q_TILE), 0) <= lax.broadcasted_iota(jnp.int32, (Q_TILE, Q_TILE), 1),
        0.0, 1.0))
    bias_pad = jnp.zeros((HEAD_DIM - blocks_per_chunk, NSA_REP * Q_TILE), F32)

    def scores(t, chunk):
        blk_bias = _tile_heads(selb_ref[t, pl.ds(pl.multiple_of(chunk * blocks_per_chunk, blocks_per_chunk),
                                                 blocks_per_chunk), :])
        qa_ref[t, HEAD_DIM:, :] = jnp.concatenate([blk_bias, bias_pad], axis=0).astype(BF16)
        k0 = pl.multiple_of(chunk * SEL_CHUNK, SEL_CHUNK)
        return _dot(ks_ref[0, 0, 0, pl.ds(k0, SEL_CHUNK), :], qa_ref[t])

    def put_scores(t, chunk, s_ref):
        s_ref[t] = scores(t, chunk)
        own_neg = jnp.where(chunk == n_chunks - 1, NEG, 0.0)
        s_ref[t, pl.ds(own_row[t], Q_TILE), :] = s_ref[t, pl.ds(own_row[t], Q_TILE), :] + own_mask * own_neg

    def sel_step(c, s_cur_ref, s_next_ref):
        m_old, m_new = [], []
        for t in tiles:
            m_old.append(m_ref[t])
            m_new.append(jnp.maximum(m_old[t], jnp.max(s_cur_ref[t], axis=0, keepdims=True)))
        nxt = jnp.minimum(c + 1, last_chunk)
        for t in tiles:
            put_scores(t, nxt, s_next_ref)
        v0 = pl.multiple_of(jnp.maximum(c - 1, 0) * SEL_CHUNK, SEL_CHUNK)
        v_prev = vst_ref[0, 0, 0, :, pl.ds(v0, SEL_CHUNK)]
        pv = [_dot(v_prev, p_ref[t]) for t in tiles]
        for t in tiles:
            p = jnp.exp2(s_cur_ref[t] - m_new[t])
            alpha = jnp.exp2(m_old[t] - m_new[t])
            l_ref[t] = alpha * l_ref[t] + jnp.sum(p, axis=0, keepdims=True)
            acc_ref[t] = alpha * (acc_ref[t] + pv[t])
            m_ref[t] = m_new[t]
            p_ref[t] = p.astype(BF16)

    for t in tiles:
        qa_ref[t, 0:HEAD_DIM, :] = q_t[t]
        put_scores(t, 0, sa_ref)
    p_ref[...] = jnp.zeros_like(p_ref)

    for t in tiles:
        dist = tq[t] - (w0[t] + lax.broadcasted_iota(jnp.int32, (wkeys, 1), 0))
        sm3 = s3[t] + _tile_heads(jnp.where((dist >= 0) & (dist < WINDOW), 0.0, NEG))
        e3 = jnp.exp2(sm3 - jnp.max(sm3, axis=0, keepdims=True))
        den3_ref[t] = jnp.sum(e3, axis=0, keepdims=True)
        p3_ref[t] = e3.astype(BF16)

    m_ref[...] = jnp.full(m_ref.shape, NEG, F32)
    l_ref[...] = jnp.zeros_like(l_ref)
    acc_ref[...] = jnp.zeros_like(acc_ref)

    def sel_pair(i, carry):
        sel_step(2 * i, sa_ref, sb_ref)

        @pl.when(2 * i + 1 < n_chunks)
        def _():
            sel_step(2 * i + 1, sb_ref, sa_ref)
        return carry

    lax.fori_loop(0, (n_chunks + 1) // 2, sel_pair, 0)

    vl = pl.multiple_of((n_chunks - 1) * SEL_CHUNK, SEL_CHUNK)
    v_last = vst_ref[0, 0, 0, :, pl.ds(vl, SEL_CHUNK)]
    acc_w = [_dot(vwt_ref[0, 0, 0, :, pl.ds(w0[t], wkeys)], p3_ref[t]) for t in tiles]
    acc_s = [acc_ref[t] + _dot(v_last, p_ref[t]) for t in tiles]
    for t in tiles:
        gates = gate_ref[0, :, lanes(t)]
        g_c = jnp.concatenate([gates[3 * r:3 * r + 1, :] for r in range(NSA_REP)], axis=1)
        g_s = jnp.concatenate([gates[3 * r + 1:3 * r + 2, :] for r in range(NSA_REP)], axis=1)
        g_w = jnp.concatenate([gates[3 * r + 2:3 * r + 3, :] for r in range(NSA_REP)], axis=1)
        out_t = g_c * oc_ref[t] + (g_w / den3_ref[t]) * acc_w[t] + (g_s / l_ref[t]) * acc_s[t]
        out_t = jnp.concatenate([out_t[:, r * Q_TILE:(r + 1) * Q_TILE] for r in range(NSA_REP)], axis=0)
        o_ref[lanes(t), :] = out_t.T.astype(o_ref.dtype)


def _nsa(q_t, cc, cc_t, ksel, kwin, vt, gates_t, ovl, prev, batch, seq, q_lo, q_hi):
    span = NSA_TILES * Q_TILE
    nqb = (q_hi - q_lo) // span
    off = q_lo // span
    nq_all = seq // span
    ncb_pad = cc.shape[3]
    nsb = seq // SEL_BLOCK
    rows = NSA_REP * Q_TILE
    vspec = lambda which: pl.BlockSpec((1, 1, 1, HEAD_DIM, seq), lambda b, g, i: (which, b, g, 0, 0))
    per_tile = lambda shape, dtype: pltpu.VMEM((NSA_TILES,) + shape, dtype)
    return pl.pallas_call(
        functools.partial(_nsa_body, q_lo=q_lo, q_hi=q_hi),
        grid=(batch, NSA_GROUPS, nqb),
        in_specs=[
            pl.BlockSpec((1, NSA_REP, HEAD_DIM, span), lambda b, g, i: (b, g, 0, off + i)),
            pl.BlockSpec((1, 1, 1, ncb_pad, HEAD_DIM), lambda b, g, i: (0, b, g, 0, 0)),
            pl.BlockSpec((1, 1, 1, HEAD_DIM, ncb_pad), lambda b, g, i: (1, b, g, 0, 0)),
            pl.BlockSpec((1, 1, 1, seq, LANES), lambda b, g, i: (0, b, g, 0, 0)),
            pl.BlockSpec((1, 1, 1, seq, HEAD_DIM), lambda b, g, i: (0, b, g, 0, 0)),
            vspec(0), vspec(1),
            pl.BlockSpec((1, GATE_ROWS, span), lambda b, g, i: (g, 0, b * nq_all + off + i)),
            pl.BlockSpec(ovl.shape, lambda b, g, i: (0, 0)),
            pl.BlockSpec(memory_space=pl.ANY),
        ],
        out_specs=pl.BlockSpec((span, NSA_REP * HEAD_DIM), lambda b, g, i: (b * nq_all + off + i, g)),
        out_shape=jax.ShapeDtypeStruct(prev.shape, prev.dtype),
        input_output_aliases={9: 0},
        scratch_shapes=[per_tile((nsb, Q_TILE), F32), per_tile((nsb, Q_TILE), F32),
                        per_tile((2 * HEAD_DIM, rows), BF16),
                        per_tile((SEL_CHUNK, rows), F32), per_tile((SEL_CHUNK, rows), F32),
                        per_tile((SEL_CHUNK, rows), BF16),
                        per_tile((1, rows), F32), per_tile((1, rows), F32),
                        per_tile((HEAD_DIM, rows), F32), per_tile((HEAD_DIM, rows), F32),
                        per_tile((WINDOW + Q_TILE, rows), BF16), per_tile((1, rows), F32)],
        compiler_params=pltpu.CompilerParams(
            dimension_semantics=("parallel", "parallel", "arbitrary"), vmem_limit_bytes=VMEM_LIMIT),
        name="nsa",
    )(q_t, cc, cc_t, ksel[None], kwin[None], vt, vt, gates_t, ovl, prev)


def _hgrn_body(hq_ref, hf_ref, hi_ref, hg_ref, lbl_ref, gain_ref, o_ref,
               state_ref, b_ref, k_ref, v_ref, q_ref, *, layer):
    @pl.when(pl.program_id(1) == 0)
    def _():
        state_ref[...] = jnp.zeros_like(state_ref)

    logits = lbl_ref[...]
    ex = jnp.exp(logits - jnp.max(logits, axis=0, keepdims=True))
    sm = ex / jnp.sum(ex, axis=0, keepdims=True)
    lb_all = jnp.zeros((1, HG_HEADS * HG_DK), F32)
    for i in range(1, layer + 1):
        lb_all = lb_all + sm[i:i + 1, :]

    ch, blk, half = HG_CHUNK, HG_SUB, HG_SUB // 2
    nblk = ch // blk
    ri = lax.broadcasted_iota(jnp.int32, (ch, ch), 0)
    ci = lax.broadcasted_iota(jnp.int32, (ch, ch), 1)
    tril = jnp.where(ri >= ci, 1.0, 0.0).astype(BF16)
    blk_row = lax.broadcasted_iota(jnp.int32, (blk, HG_DK), 0)
    low_rows = blk_row >= half
    rr = lax.broadcasted_iota(jnp.int32, (blk, blk), 0)
    cc = lax.broadcasted_iota(jnp.int32, (blk, blk), 1)
    cross = (rr >= half) & (cc < half)
    half_row = lax.broadcasted_iota(jnp.int32, (half, HG_DK), 0)
    gain = gain_ref[...]
    heads = range(HG_HEADS)

    def chunk(j, carry):
        r0 = pl.multiple_of(j * ch, ch)
        for h in heads:
            cols = slice(h * HG_DK, (h + 1) * HG_DK)
            lb = lb_all[:, cols]
            hq = hq_ref[0, pl.ds(r0, ch), cols]
            hf = hf_ref[0, pl.ds(r0, ch), cols]
            f = lb + (1.0 - lb) * jax.nn.sigmoid(hf)
            lf = _split_bf16(jnp.log(jnp.maximum(f, 1e-30)), 3)
            b_ref[h] = _dot(tril, lf[0]) + _dot(tril, lf[1]) + _dot(tril, lf[2])
            k_ref[h] = 1.0 - f
            v_ref[h] = hi_ref[0, pl.ds(r0, ch), cols]
            q_ref[h] = hq * jax.nn.sigmoid(hq)

        o_blocks, tables = [], []
        for h in heads:
            inter = _dot_nt((q_ref[h] * jnp.exp(b_ref[h])).astype(BF16), state_ref[h].astype(BF16))
            o_blocks.append([inter[ib * blk:(ib + 1) * blk, :] for ib in range(nblk)])
            tables.append([])
            for ib in range(nblk):
                i0 = ib * blk
                qi = q_ref[h, i0:i0 + blk, :]
                bi = b_ref[h, i0:i0 + blk, :]
                a = None
                if ib > 0:
                    ref_b = b_ref[h, i0 - 1:i0, :]
                    qd = (qi * jnp.exp(bi - ref_b)).astype(BF16)
                    kd = (k_ref[h, 0:i0, :] * jnp.exp(ref_b - b_ref[h, 0:i0, :])).astype(BF16)
                    a = _dot_nt(qd, kd).astype(BF16)
                mid_b = b_ref[h, i0 + half - 1:i0 + half, :]
                qd2 = (qi * jnp.exp(jnp.where(low_rows, bi - mid_b, 0.0))).astype(BF16)
                kd2 = (k_ref[h, i0:i0 + blk, :] * jnp.exp(jnp.where(low_rows, 0.0, mid_b - bi))).astype(BF16)
                a2 = jnp.where(cross, _dot_nt(qd2, kd2), 0.0).astype(BF16)
                tables[h].append((a, a2))
        for h in heads:
            bl = b_ref[h, ch - 1:ch, :]
            kd_end = (k_ref[h] * jnp.exp(bl - b_ref[h])).astype(BF16)
            vt = v_ref[h].T.astype(BF16)
            state_ref[h] = state_ref[h] * jnp.exp(bl) + _dot(vt, kd_end)

        for h in heads:
            for ib in range(nblk):
                i0 = ib * blk
                a, a2 = tables[h][ib]
                o_i = o_blocks[h][ib] + _dot(a2, v_ref[h, i0:i0 + blk, :].astype(BF16))
                if a is not None:
                    o_i = o_i + _dot(a, v_ref[h, 0:i0, :].astype(BF16))
                o_blocks[h][ib] = o_i

        nhalf = ch // half
        diag = [[jnp.zeros((half, HG_DV), F32) for _ in range(nhalf)] for _ in heads]
        for s in range(half):
            for h in heads:
                for ih in range(nhalf):
                    g0 = ih * half
                    bs = b_ref[h, g0 + s:g0 + s + 1, :]
                    ks = k_ref[h, g0 + s:g0 + s + 1, :]
                    vs = v_ref[h, g0 + s:g0 + s + 1, :]
                    bi = b_ref[h, g0:g0 + half, :]
                    qi = q_ref[h, g0:g0 + half, :]
                    dec = jnp.exp(jnp.where(half_row >= s, bi - bs, NEG))
                    a_s = jnp.sum(qi * dec * ks, axis=-1, keepdims=True)
                    diag[h][ih] = diag[h][ih] + a_s * vs

        for h in heads:
            cols = slice(h * HG_DK, (h + 1) * HG_DK)
            o = jnp.concatenate(o_blocks[h], axis=0) + jnp.concatenate(diag[h], axis=0)
            o = o * lax.rsqrt(jnp.mean(o * o, axis=-1, keepdims=True) + EPS) * gain
            hg = hg_ref[0, pl.ds(r0, ch), cols]
            o_ref[pl.ds(r0, ch), cols] = (o * (hg * jax.nn.sigmoid(hg))).astype(o_ref.dtype)
        return carry

    lax.fori_loop(0, hq_ref.shape[1] // ch, chunk, 0)


def _hgrn(h_all, lb_logits, out_gain, layer, batch, seq):
    t = h_all.shape[1]
    nblk = seq // HG_ROWS
    nl = lb_logits.shape[0]
    hw = HG_HEADS * HG_DK
    hspec = lambda which: pl.BlockSpec((1, HG_ROWS, hw), lambda b, c: (which, b * nblk + c, 0))
    return pl.pallas_call(
        functools.partial(_hgrn_body, layer=layer),
        grid=(batch, nblk),
        in_specs=[hspec(0), hspec(1), hspec(2), hspec(3),
                  pl.BlockSpec((nl, hw), lambda b, c: (0, 0)),
                  pl.BlockSpec((1, HG_DV), lambda b, c: (0, 0))],
        out_specs=pl.BlockSpec((HG_ROWS, HG_HEADS * HG_DV), lambda b, c: (b * nblk + c, 0)),
        out_shape=jax.ShapeDtypeStruct((t, HG_HEADS * HG_DV), BF16),
        scratch_shapes=[pltpu.VMEM((HG_HEADS, HG_DV, HG_DK), F32)]
        + [pltpu.VMEM((HG_HEADS, HG_CHUNK, HG_DK), F32)] * 4,
        compiler_params=pltpu.CompilerParams(
            dimension_semantics=("parallel", "arbitrary"), vmem_limit_bytes=VMEM_LIMIT),
        name="hgrn",
    )(h_all, h_all, h_all, h_all, lb_logits, out_gain.reshape(1, HG_DV))


def _relayout_w_in(w):
    d = w.shape[0]
    nq = NSA_HEADS * HEAD_DIM
    nkv = 6 * NSA_GROUPS * HEAD_DIM
    ng = 3 * NSA_HEADS
    q_kv = w[:, :nq + nkv]
    gts = w[:, nq + nkv:nq + nkv + ng]
    hg = w[:, nq + nkv + ng:]
    per = 3 * NSA_REP
    gate_blocks = [jnp.pad(gts[:, gi * per:(gi + 1) * per], ((0, 0), (0, LANES - per)))
                   for gi in range(NSA_GROUPS)]
    out = jnp.concatenate([q_kv, hg] + gate_blocks, axis=1).astype(BF16)
    assert out.shape == (d, _C_END)
    return out


def _compress_weights(pos, w1, w2):
    eye = jnp.eye(NSA_GROUPS, dtype=F32)
    e = w1.shape[-1]

    def big(w1_half):
        y = jnp.einsum("sldh,gk->slgdkh", w1_half, eye)
        return y.reshape(2, CMP_STRIDE * NSA_GROUPS * HEAD_DIM, NSA_GROUPS * e).astype(BF16)

    def pos_rows(p_half):
        y = jnp.broadcast_to(p_half[:, :, None, :], (2, CMP_STRIDE, NSA_GROUPS, HEAD_DIM))
        return y.reshape(2, 1, CMP_STRIDE * NSA_GROUPS * HEAD_DIM)

    w2b = jnp.einsum("shd,gk->sghkd", w2, eye).reshape(2, NSA_GROUPS * e, NSA_GROUPS * HEAD_DIM)
    return (big(w1[:, :CMP_STRIDE]), big(w1[:, CMP_STRIDE:]),
            pos_rows(pos[:, :CMP_STRIDE]), pos_rows(pos[:, CMP_STRIDE:]), w2b.astype(BF16))


def kernel(x, ffn1_norm, ffn1_w_gate, ffn1_w_up, ffn1_w_down, mix_norm, w_in, q_norm, k_norm,
           cmp_pos, cmp_w1, cmp_w2, hgrn_lb_logits, hgrn_out_norm, w_out,
           ffn2_norm, ffn2_w_gate, ffn2_w_up, ffn2_w_down):
    batch, seq, d = x.shape
    depth = w_in.shape[0]
    assert CMP_BLOCK == 2 * CMP_STRIDE and seq % SEL_CHUNK == 0 and seq >= WINDOW + Q_TILE
    assert seq % HG_ROWS == 0 and SEL_CHUNK % SEL_BLOCK == 0 and SEL_CHUNK % (NSA_TILES * Q_TILE) == 0
    assert seq % (2 * SEL_CHUNK) == 0
    t = batch * seq
    tm_proj = 512

    inv = 1.0 / (ROPE_THETA ** (jnp.arange(0, HEAD_DIM, 2, dtype=F32) / HEAD_DIM))
    ang = jnp.arange(seq, dtype=F32)[:, None] * inv[None, :]
    ang = jnp.concatenate([ang, ang], axis=-1)
    cos1, sin1 = jnp.cos(ang), jnp.sin(ang)
    cos2, sin2 = jnp.tile(cos1, (1, 2)), jnp.tile(sin1, (1, 2))
    cost, sint = cos1.T, sin1.T

    nseg = seq // CMP_STRIDE
    nsb = seq // SEL_BLOCK
    ci = np.arange(nseg)[None, :]
    sj = np.arange(nsb)[:, None]
    ovl = ((ci * CMP_STRIDE <= sj * SEL_BLOCK + SEL_BLOCK - 1)
           & (ci * CMP_STRIDE + CMP_BLOCK - 1 >= sj * SEL_BLOCK)
           & (ci < (seq - CMP_BLOCK) // CMP_STRIDE + 1))
    ovl = jnp.asarray(ovl, dtype=BF16)

    xt = x.reshape(t, d)
    for l in range(depth):
        xt = _ffn(xt, ffn1_norm, ffn1_w_gate, ffn1_w_up, ffn1_w_down, l)
        qgt = jnp.broadcast_to(q_norm[l][:, None], (HEAD_DIM, tm_proj))
        kg2 = jnp.tile(k_norm[l], (1, 2))
        q_t, cmp_in, ksel, kwin, vt, gates_t, h_all = _inproj(
            xt, mix_norm[l], _relayout_w_in(w_in[l]), cos2, sin2, cost, sint, qgt, kg2,
            batch, seq, tm=tm_proj)
        w1a, w1b, pa, pb, w2b = _compress_weights(cmp_pos[l], cmp_w1[l], cmp_w2[l])
        cc, cc_t = _compress(cmp_in, w1a, w1b, pa, pb, w2b, batch, seq)
        qr = 2 * SEL_CHUNK
        o_nsa = jnp.zeros((t, NSA_HEADS * HEAD_DIM), BF16)
        for lo in range(0, seq, qr):
            o_nsa = _nsa(q_t, cc, cc_t, ksel, kwin, vt, gates_t, ovl, o_nsa, batch, seq, lo, lo + qr)
        o_hg = _hgrn(h_all, hgrn_lb_logits, hgrn_out_norm[l], l, batch, seq)
        xt = _ffn(xt, ffn2_norm, ffn2_w_gate, ffn2_w_up, ffn2_w_down, l, mix=(o_nsa, o_hg, w_out))
    return xt.reshape(batch, seq, d)
```

```python
import functools

import jax
import jax.numpy as jnp
import numpy as np
from jax import lax
from jax.experimental import pallas as pl
from jax.experimental.pallas import tpu as pltpu

NSA_HEADS = 8
NSA_GROUPS = 2
NSA_REP = NSA_HEADS // NSA_GROUPS
HEAD_DIM = 64
CMP_BLOCK = 32
CMP_STRIDE = 16
CMP_HIDDEN = 128
SEL_BLOCK = 64
SEL_TOPN = 16
WINDOW = 512
HG_HEADS = 4
HG_DK = 128
HG_DV = 128
ROPE_THETA = 10000.0
EPS = 1e-6
NEG = -1e30
LOG2E = 1.4426950408889634

LANES = 128
Q_TILE = 128
NSA_TILES = 2
SEL_CHUNK = 512
HG_CHUNK = 64
HG_SUB = 16
HG_ROWS = 1024
VMEM_LIMIT = 56 * 1024 * 1024

BF16 = jnp.bfloat16
F32 = jnp.float32


def _dot(a, b):
    return jnp.dot(a, b, preferred_element_type=F32)


def _dot_nt(a, b):
    return lax.dot_general(a, b, (((1,), (1,)), ((), ())), preferred_element_type=F32)


def _split_bf16(x, parts):
    out = []
    rem = x
    for _ in range(parts):
        p = rem.astype(BF16)
        out.append(p)
        rem = rem - p.astype(F32)
    return out


def _ffn_body(*refs, fuse_mix):
    if fuse_mix:
        (x_ref, ma_ref, mb_ref, woa_ref, wob_ref, g_ref, wg_ref, wu_ref, wd_ref,
         o_ref, xs_ref, hn_ref, acc_ref) = refs
    else:
        x_ref, g_ref, wg_ref, wu_ref, wd_ref, o_ref, xs_ref, hn_ref, acc_ref = refs
    f = pl.program_id(1)

    @pl.when(f == 0)
    def _():
        x = x_ref[...]
        if fuse_mix:
            x = x + _dot(ma_ref[...], woa_ref[0].astype(BF16)) + _dot(mb_ref[...], wob_ref[0].astype(BF16))
        xs_ref[...] = x
        ms = jnp.mean(x * x, axis=-1, keepdims=True)
        hn_ref[...] = (x * lax.rsqrt(ms + EPS) * g_ref[0]).astype(BF16)
        acc_ref[...] = jnp.zeros_like(acc_ref)

    hn = hn_ref[...]
    gate = _dot(hn, wg_ref[0].astype(BF16))
    up = _dot(hn, wu_ref[0].astype(BF16))
    act = (gate * jax.nn.sigmoid(gate) * up).astype(BF16)
    acc_ref[...] += _dot(act, wd_ref[0].astype(BF16))

    @pl.when(f == pl.num_programs(1) - 1)
    def _():
        o_ref[...] = xs_ref[...] + 0.5 * acc_ref[...]


def _ffn(x, g, wg, wu, wd, layer, mix=None, *, tm=1024, tf=256):
    t, d = x.shape
    ff = wg.shape[2]
    assert t % tm == 0 and ff % tf == 0
    fuse = mix is not None
    row = lambda i, f: (i, 0)
    in_specs = [pl.BlockSpec((tm, d), row)]
    args = [x]
    if fuse:
        ma, mb, wo = mix
        da, db = ma.shape[1], mb.shape[1]
        assert da == db and wo.shape[1] == da + db
        in_specs += [pl.BlockSpec((tm, da), row), pl.BlockSpec((tm, db), row),
                     pl.BlockSpec((1, da, d), lambda i, f: (layer, 0, 0)),
                     pl.BlockSpec((1, db, d), lambda i, f: (layer, 1, 0))]
        args += [ma, mb, wo, wo]
    in_specs += [pl.BlockSpec((1, 1, d), lambda i, f: (layer, 0, 0)),
                 pl.BlockSpec((1, d, tf), lambda i, f: (layer, 0, f)),
                 pl.BlockSpec((1, d, tf), lambda i, f: (layer, 0, f)),
                 pl.BlockSpec((1, tf, d), lambda i, f: (layer, f, 0))]
    args += [g.reshape(g.shape[0], 1, d), wg, wu, wd]
    return pl.pallas_call(
        functools.partial(_ffn_body, fuse_mix=fuse),
        grid=(t // tm, ff // tf),
        in_specs=in_specs,
        out_specs=pl.BlockSpec((tm, d), row),
        out_shape=jax.ShapeDtypeStruct((t, d), F32),
        scratch_shapes=[pltpu.VMEM((tm, d), F32), pltpu.VMEM((tm, d), BF16), pltpu.VMEM((tm, d), F32)],
        compiler_params=pltpu.CompilerParams(
            dimension_semantics=("parallel", "arbitrary"), vmem_limit_bytes=VMEM_LIMIT),
        name="ffn_mix" if fuse else "ffn",
    )(*args)


_C_Q = 0
_C_KV = NSA_HEADS * HEAD_DIM
_C_HG = _C_KV + 6 * LANES
_C_GATE = _C_HG + 4 * HG_HEADS * HG_DK
_C_END = _C_GATE + NSA_GROUPS * LANES
GATE_ROWS = 16
Q_SCALE = HEAD_DIM ** -0.5 * LOG2E


def _inproj_body(x_ref, g_ref, w_ref, cos_ref, sin_ref, cost_ref, sint_ref, qg_ref, kg_ref,
                 q_ref, cmp_ref, ksel_ref, kwin_ref, vt_ref, gate_ref, h_ref, seg_ref, *, tiles_per_seq):
    x = x_ref[...]
    ms = jnp.mean(x * x, axis=-1, keepdims=True)
    hn = (x * lax.rsqrt(ms + EPS) * g_ref[...]).astype(BF16)
    cos = cos_ref[...]
    sin = sin_ref[...]
    tm = x.shape[0]
    half = HEAD_DIM // 2

    def proj(c0, width):
        return _dot(hn, w_ref[:, c0:c0 + width])

    ri = (lax.broadcasted_iota(jnp.int32, (2 * LANES, LANES), 0) % LANES) // HEAD_DIM
    ci = lax.broadcasted_iota(jnp.int32, (2 * LANES, LANES), 1) // HEAD_DIM
    head_mean = jnp.where(ri == ci, 1.0 / HEAD_DIM, 0.0).astype(BF16)
    lane = lax.broadcasted_iota(jnp.int32, (tm, LANES), 1)
    first_half = (lane % HEAD_DIM) < half

    def norm_rope(y, gain):
        msq = _dot(jnp.concatenate(_split_bf16(y * y, 2), axis=1), head_mean)
        yn = y * lax.rsqrt(msq + EPS) * gain
        rot = jnp.where(first_half, -pltpu.roll(yn, LANES - half, 1), pltpu.roll(yn, half, 1))
        return yn * cos + rot * sin

    def heads_t(y):
        return y.T.reshape(2, HEAD_DIM, tm)

    qgain = (qg_ref[...] * Q_SCALE)[None]
    cost = cost_ref[...][None]
    sint = sint_ref[...][None]
    yq = proj(_C_Q, NSA_HEADS * HEAD_DIM)
    ykv = proj(_C_KV, 6 * LANES)
    kv = lambda j: ykv[:, j * LANES:(j + 1) * LANES]
    for c in range(NSA_HEADS // 2):
        y3 = heads_t(yq[:, c * LANES:(c + 1) * LANES])
        yn = y3 * lax.rsqrt(jnp.mean(y3 * y3, axis=1, keepdims=True) + EPS) * qgain
        rot = jnp.concatenate([-yn[:, half:], yn[:, :half]], axis=1)
        q_ref[0, 2 * c:2 * c + 2] = (yn * cost + rot * sint).astype(BF16)

    seg_ref[0] = norm_rope(kv(0), kg_ref[0:1, :])
    seg_ref[1] = kv(1)
    nseg_t = tm // CMP_STRIDE
    for s in range(2):
        for l in range(CMP_STRIDE):
            cmp_ref[s, :, l * LANES:(l + 1) * LANES] = seg_ref[s, pl.ds(l, nseg_t, stride=CMP_STRIDE), :].astype(BF16)
    ksel = norm_rope(kv(2), kg_ref[1:2, :])
    pos = (pl.program_id(0) % tiles_per_seq) * tm + lax.broadcasted_iota(jnp.int32, (tm, HEAD_DIM), 0)
    blk_in_chunk = (pos // SEL_BLOCK) % (SEL_CHUNK // SEL_BLOCK)
    onehot = jnp.where(lax.broadcasted_iota(jnp.int32, (tm, HEAD_DIM), 1) == blk_in_chunk, 1.0, 0.0)
    for gi in range(NSA_GROUPS):
        ksel_ref[0, gi] = jnp.concatenate(
            [ksel[:, gi * HEAD_DIM:(gi + 1) * HEAD_DIM], onehot], axis=1).astype(BF16)
    kwin = norm_rope(kv(4), kg_ref[2:3, :]).astype(BF16)
    kwin_ref[0, 0] = kwin[:, :HEAD_DIM]
    kwin_ref[0, 1] = kwin[:, HEAD_DIM:]
    for which, col in ((0, 3), (1, 5)):
        vt_ref[which, 0] = heads_t(kv(col)).astype(BF16)

    hw = HG_HEADS * HG_DK
    for i in range(4):
        h_ref[i] = proj(_C_HG + i * hw, hw)
    ygate = proj(_C_GATE, NSA_GROUPS * LANES)
    for gi in range(NSA_GROUPS):
        gt = jax.nn.sigmoid(ygate[:, gi * LANES:(gi + 1) * LANES]).T
        gate_ref[gi] = gt[:GATE_ROWS]


def _inproj(x, g, w2, cos2, sin2, cost, sint, qgt, kg2, batch, seq, *, tm=256):
    t, d = x.shape
    nst = seq // tm
    hw = HG_HEADS * HG_DK
    out_shape = (
        jax.ShapeDtypeStruct((batch, NSA_HEADS, HEAD_DIM, seq), BF16),
        jax.ShapeDtypeStruct((2, t // CMP_STRIDE, CMP_STRIDE * LANES), BF16),
        jax.ShapeDtypeStruct((batch, NSA_GROUPS, seq, LANES), BF16),
        jax.ShapeDtypeStruct((batch, NSA_GROUPS, seq, HEAD_DIM), BF16),
        jax.ShapeDtypeStruct((2, batch, NSA_GROUPS, HEAD_DIM, seq), BF16),
        jax.ShapeDtypeStruct((NSA_GROUPS, GATE_ROWS, t), F32),
        jax.ShapeDtypeStruct((4, t, hw), F32),
    )
    out_specs = (
        pl.BlockSpec((1, NSA_HEADS, HEAD_DIM, tm), lambda i: (i // nst, 0, 0, i % nst)),
        pl.BlockSpec((2, tm // CMP_STRIDE, CMP_STRIDE * LANES), lambda i: (0, i, 0)),
        pl.BlockSpec((1, NSA_GROUPS, tm, LANES), lambda i: (i // nst, 0, i % nst, 0)),
        pl.BlockSpec((1, NSA_GROUPS, tm, HEAD_DIM), lambda i: (i // nst, 0, i % nst, 0)),
        pl.BlockSpec((2, 1, NSA_GROUPS, HEAD_DIM, tm), lambda i: (0, i // nst, 0, 0, i % nst)),
        pl.BlockSpec((NSA_GROUPS, GATE_ROWS, tm), lambda i: (0, 0, i)),
        pl.BlockSpec((4, tm, hw), lambda i: (0, i, 0)),
    )
    in_specs = [
        pl.BlockSpec((tm, d), lambda i: (i, 0)),
        pl.BlockSpec((1, d), lambda i: (0, 0)),
        pl.BlockSpec(w2.shape, lambda i: (0, 0)),
        pl.BlockSpec((tm, LANES), lambda i: (i % nst, 0)),
        pl.BlockSpec((tm, LANES), lambda i: (i % nst, 0)),
        pl.BlockSpec((HEAD_DIM, tm), lambda i: (0, i % nst)),
        pl.BlockSpec((HEAD_DIM, tm), lambda i: (0, i % nst)),
        pl.BlockSpec((HEAD_DIM, tm), lambda i: (0, 0)),
        pl.BlockSpec((3, LANES), lambda i: (0, 0)),
    ]
    return pl.pallas_call(
        functools.partial(_inproj_body, tiles_per_seq=nst),
        grid=(t // tm,),
        in_specs=in_specs,
        out_specs=out_specs,
        out_shape=out_shape,
        scratch_shapes=[pltpu.VMEM((2, tm, LANES), F32)],
        compiler_params=pltpu.CompilerParams(
            dimension_semantics=("parallel",), vmem_limit_bytes=VMEM_LIMIT),
        name="inproj",
    )(x, g.reshape(1, d), w2, cos2, sin2, cost, sint, qgt, kg2)


def _compress_body(seg_ref, w1a_ref, w1b_ref, pa_ref, pb_ref, w2_ref, o_ref, ot_ref):
    segs = seg_ref[0, 0]
    w1a = w1a_ref[0]
    w1b = w1b_ref[0]
    first = _dot(segs, w1a)
    second = _dot(segs, w1b)
    pa_hi, pa_lo = _split_bf16(jnp.broadcast_to(pa_ref[0], (8, pa_ref.shape[2])), 2)
    pb_hi, pb_lo = _split_bf16(jnp.broadcast_to(pb_ref[0], (8, pb_ref.shape[2])), 2)
    cpos = _dot(pa_hi, w1a) + _dot(pa_lo, w1a) + _dot(pb_hi, w1b) + _dot(pb_lo, w1b)
    nseg = first.shape[0]
    pre = first + pltpu.roll(second, nseg - 1, 0) + cpos[0:1, :]
    hid = jax.nn.gelu(pre, approximate=True).astype(BF16)
    out = _dot(hid, w2_ref[0])
    outb = out.astype(BF16)
    o_ref[0, 0, 0] = outb[:, :HEAD_DIM]
    o_ref[0, 0, 1] = outb[:, HEAD_DIM:]
    ot_ref[0, 0] = out.T.reshape(NSA_GROUPS, HEAD_DIM, nseg).astype(BF16)


def _compress(cmp_in, w1a, w1b, pa, pb, w2, batch, seq):
    nseg = seq // CMP_STRIDE
    width = CMP_STRIDE * LANES
    segs = cmp_in.reshape(2, batch, nseg, width)
    return pl.pallas_call(
        _compress_body,
        grid=(2, batch),
        in_specs=[
            pl.BlockSpec((1, 1, nseg, width), lambda s, b: (s, b, 0, 0)),
            pl.BlockSpec((1,) + w1a.shape[1:], lambda s, b: (s, 0, 0)),
            pl.BlockSpec((1,) + w1b.shape[1:], lambda s, b: (s, 0, 0)),
            pl.BlockSpec((1, 1, width), lambda s, b: (s, 0, 0)),
            pl.BlockSpec((1, 1, width), lambda s, b: (s, 0, 0)),
            pl.BlockSpec((1,) + w2.shape[1:], lambda s, b: (s, 0, 0)),
        ],
        out_specs=(pl.BlockSpec((1, 1, NSA_GROUPS, nseg, HEAD_DIM), lambda s, b: (s, b, 0, 0, 0)),
                   pl.BlockSpec((1, 1, NSA_GROUPS, HEAD_DIM, nseg), lambda s, b: (s, b, 0, 0, 0))),
        out_shape=(jax.ShapeDtypeStruct((2, batch, NSA_GROUPS, nseg, HEAD_DIM), BF16),
                   jax.ShapeDtypeStruct((2, batch, NSA_GROUPS, HEAD_DIM, nseg), BF16)),
        compiler_params=pltpu.CompilerParams(
            dimension_semantics=("parallel", "parallel"), vmem_limit_bytes=VMEM_LIMIT),
        name="compress",
    )(segs, w1a, w1b, pa, pb, w2)


def _tile_heads(a):
    return jnp.concatenate([a] * NSA_REP, axis=1)


def _nsa_body(q_ref, kcc_ref, vcct_ref, ks_ref, kw_ref, vst_ref, vwt_ref, gate_ref, ovl_ref, prev_ref,
              o_ref, imp_ref, selb_ref, qa_ref, sa_ref, sb_ref, p_ref, m_ref, l_ref, acc_ref,
              oc_ref, p3_ref, den3_ref, *, q_lo, q_hi):
    del prev_ref
    tiles = range(NSA_TILES)
    first = q_lo + pl.program_id(2) * (NSA_TILES * Q_TILE)
    starts = [first + t * Q_TILE for t in tiles]
    lanes = lambda t: slice(t * Q_TILE, (t + 1) * Q_TILE)
    q_t = [jnp.concatenate([q_ref[0, r, :, lanes(t)] for r in range(NSA_REP)], axis=1) for t in tiles]
    tq = [starts[t] + lax.broadcasted_iota(jnp.int32, (1, Q_TILE), 1) for t in tiles]

    kcc = kcc_ref[0, 0, 0]
    ncb = kcc.shape[0]
    wkeys = WINDOW + Q_TILE
    w0 = [pl.multiple_of(jnp.maximum(starts[t] - WINDOW, 0), Q_TILE) for t in tiles]
    s1 = [_dot(kcc, q_t[t]) for t in tiles]
    s3 = [_dot(kw_ref[0, 0, 0, pl.ds(w0[t], wkeys), :], q_t[t]) for t in tiles]

    cend = lax.broadcasted_iota(jnp.int32, (ncb, 1), 0) * CMP_STRIDE + (CMP_BLOCK - 1)
    p1 = []
    for t in tiles:
        sm1 = s1[t] + _tile_heads(jnp.where(cend <= tq[t], 0.0, NEG))
        e1 = jnp.exp2(sm1 - jnp.max(sm1, axis=0, keepdims=True))
        den1 = jnp.sum(e1, axis=0, keepdims=True)
        p1.append(e1 * jnp.where(_tile_heads(tq[t] >= CMP_BLOCK - 1), 1.0 / den1, 0.0))
    ovl = ovl_ref[...]
    nsb = ovl.shape[0]
    imp = []
    for t in tiles:
        oc_ref[t] = _dot(vcct_ref[0, 0, 0], p1[t].astype(BF16))
        p1sum = p1[t][:, 0:Q_TILE]
        for r in range(1, NSA_REP):
            p1sum = p1sum + p1[t][:, r * Q_TILE:(r + 1) * Q_TILE]
        ps_hi, ps_lo = _split_bf16(p1sum, 2)
        imp.append(_dot(ovl, ps_hi) + _dot(ovl, ps_lo))

    jb = lax.broadcasted_iota(jnp.int32, (nsb, Q_TILE), 0)
    valid, impm = [], []
    for t in tiles:
        cur = tq[t] // SEL_BLOCK
        forced = (jb == 0) | (jb == cur) | (jb == cur - 1)
        valid.append(jb * SEL_BLOCK <= tq[t])
        impm.append(jnp.where(valid[t], jnp.where(forced, jnp.inf, imp[t]), -jnp.inf))
        imp_ref[t] = impm[t]
    sub = lax.broadcasted_iota(jnp.int32, (8, Q_TILE), 0)
    ranks = [[jnp.zeros((8, Q_TILE), F32) for _ in range(nsb // 8)] for _ in tiles]
    seen = q_hi // SEL_BLOCK
    for i in range(seen if seen > SEL_TOPN else 0):
        for t in tiles:
            xi = jnp.broadcast_to(imp_ref[t, i:i + 1, :], (8, Q_TILE))
            for v in range(seen // 8):
                xv = impm[t][8 * v:8 * v + 8, :]
                ge = jnp.where(xi >= xv, 1.0, 0.0)
                gt = jnp.where(xi > xv, 1.0, 0.0)
                if 8 * v > i:
                    inc = ge
                elif 8 * v + 7 < i:
                    inc = gt
                else:
                    inc = jnp.where(sub + 8 * v > i, ge, gt)
                ranks[t][v] = ranks[t][v] + inc
    for t in tiles:
        rank = jnp.concatenate(ranks[t], axis=0)
        selb_ref[t] = jnp.where((rank < SEL_TOPN) & valid[t], 0.0, NEG)

    blocks_per_chunk = SEL_CHUNK // SEL_BLOCK
    last_chunk = ks_ref.shape[3] // SEL_CHUNK - 1
    n_chunks = (first + NSA_TILES * Q_TILE + SEL_CHUNK - 1) // SEL_CHUNK
    own_row = [pl.multiple_of(starts[t] - (n_chunks - 1) * SEL_CHUNK, Q_TILE) for t in tiles]
    own_mask = _tile_heads(jnp.where(
        lax.broadcasted_iota(jnp.int32, (Q_TILE, Q_TILE), 0) <= lax.broadcasted_iota(jnp.int32, (Q_TILE, Q_TILE), 1),
        0.0, 1.0))
    bias_pad = jnp.zeros((HEAD_DIM - blocks_per_chunk, NSA_REP * Q_TILE), F32)

    def scores(t, chunk):
        blk_bias = _tile_heads(selb_ref[t, pl.ds(pl.multiple_of(chunk * blocks_per_chunk, blocks_per_chunk),
                                                 blocks_per_chunk), :])
        qa_ref[t, HEAD_DIM:, :] = jnp.concatenate([blk_bias, bias_pad], axis=0).astype(BF16)
        k0 = pl.multiple_of(chunk * SEL_CHUNK, SEL_CHUNK)
        return _dot(ks_ref[0, 0, 0, pl.ds(k0, SEL_CHUNK), :], qa_ref[t])

    def put_scores(t, chunk, s_ref):
        s_ref[t] = scores(t, chunk)
        own_neg = jnp.where(chunk == n_chunks - 1, NEG, 0.0)
        s_ref[t, pl.ds(own_row[t], Q_TILE), :] = s_ref[t, pl.ds(own_row[t], Q_TILE), :] + own_mask * own_neg

    def sel_step(c, s_cur_ref, s_next_ref):
        m_old, m_new = [], []
        for t in tiles:
            m_old.append(m_ref[t])
            m_new.append(jnp.maximum(m_old[t], jnp.max(s_cur_ref[t], axis=0, keepdims=True)))
        nxt = jnp.minimum(c + 1, last_chunk)
        for t in tiles:
            put_scores(t, nxt, s_next_ref)
        v0 = pl.multiple_of(jnp.maximum(c - 1, 0) * SEL_CHUNK, SEL_CHUNK)
        v_prev = vst_ref[0, 0, 0, :, pl.ds(v0, SEL_CHUNK)]
        pv = [_dot(v_prev, p_ref[t]) for t in tiles]
        for t in tiles:
            p = jnp.exp2(s_cur_ref[t] - m_new[t])
            alpha = jnp.exp2(m_old[t] - m_new[t])
            l_ref[t] = alpha * l_ref[t] + jnp.sum(p, axis=0, keepdims=True)
            acc_ref[t] = alpha * (acc_ref[t] + pv[t])
            m_ref[t] = m_new[t]
            p_ref[t] = p.astype(BF16)

    for t in tiles:
        qa_ref[t, 0:HEAD_DIM, :] = q_t[t]
        put_scores(t, 0, sa_ref)
    p_ref[...] = jnp.zeros_like(p_ref)

    for t in tiles:
        dist = tq[t] - (w0[t] + lax.broadcasted_iota(jnp.int32, (wkeys, 1), 0))
        sm3 = s3[t] + _tile_heads(jnp.where((dist >= 0) & (dist < WINDOW), 0.0, NEG))
        e3 = jnp.exp2(sm3 - jnp.max(sm3, axis=0, keepdims=True))
        den3_ref[t] = jnp.sum(e3, axis=0, keepdims=True)
        p3_ref[t] = e3.astype(BF16)

    m_ref[...] = jnp.full(m_ref.shape, NEG, F32)
    l_ref[...] = jnp.zeros_like(l_ref)
    acc_ref[...] = jnp.zeros_like(acc_ref)

    def sel_pair(i, carry):
        sel_step(2 * i, sa_ref, sb_ref)

        @pl.when(2 * i + 1 < n_chunks)
        def _():
            sel_step(2 * i + 1, sb_ref, sa_ref)
        return carry

    lax.fori_loop(0, (n_chunks + 1) // 2, sel_pair, 0)

    vl = pl.multiple_of((n_chunks - 1) * SEL_CHUNK, SEL_CHUNK)
    v_last = vst_ref[0, 0, 0, :, pl.ds(vl, SEL_CHUNK)]
    acc_w = [_dot(vwt_ref[0, 0, 0, :, pl.ds(w0[t], wkeys)], p3_ref[t]) for t in tiles]
    acc_s = [acc_ref[t] + _dot(v_last, p_ref[t]) for t in tiles]
    for t in tiles:
        gates = gate_ref[0, :, lanes(t)]
        g_c = jnp.concatenate([gates[3 * r:3 * r + 1, :] for r in range(NSA_REP)], axis=1)
        g_s = jnp.concatenate([gates[3 * r + 1:3 * r + 2, :] for r in range(NSA_REP)], axis=1)
        g_w = jnp.concatenate([gates[3 * r + 2:3 * r + 3, :] for r in range(NSA_REP)], axis=1)
        out_t = g_c * oc_ref[t] + (g_w / den3_ref[t]) * acc_w[t] + (g_s / l_ref[t]) * acc_s[t]
        out_t = jnp.concatenate([out_t[:, r * Q_TILE:(r + 1) * Q_TILE] for r in range(NSA_REP)], axis=0)
        o_ref[lanes(t), :] = out_t.T.astype(o_ref.dtype)


def _nsa(q_t, cc, cc_t, ksel, kwin, vt, gates_t, ovl, prev, batch, seq, q_lo, q_hi):
    span = NSA_TILES * Q_TILE
    nqb = (q_hi - q_lo) // span
    off = q_lo // span
    nq_all = seq // span
    ncb_pad = cc.shape[3]
    nsb = seq // SEL_BLOCK
    rows = NSA_REP * Q_TILE
    vspec = lambda which: pl.BlockSpec((1, 1, 1, HEAD_DIM, seq), lambda b, g, i: (which, b, g, 0, 0))
    per_tile = lambda shape, dtype: pltpu.VMEM((NSA_TILES,) + shape, dtype)
    return pl.pallas_call(
        functools.partial(_nsa_body, q_lo=q_lo, q_hi=q_hi),
        grid=(batch, NSA_GROUPS, nqb),
        in_specs=[
            pl.BlockSpec((1, NSA_REP, HEAD_DIM, span), lambda b, g, i: (b, g, 0, off + i)),
            pl.BlockSpec((1, 1, 1, ncb_pad, HEAD_DIM), lambda b, g, i: (0, b, g, 0, 0)),
            pl.BlockSpec((1, 1, 1, HEAD_DIM, ncb_pad), lambda b, g, i: (1, b, g, 0, 0)),
            pl.BlockSpec((1, 1, 1, seq, LANES), lambda b, g, i: (0, b, g, 0, 0)),
            pl.BlockSpec((1, 1, 1, seq, HEAD_DIM), lambda b, g, i: (0, b, g, 0, 0)),
            vspec(0), vspec(1),
            pl.BlockSpec((1, GATE_ROWS, span), lambda b, g, i: (g, 0, b * nq_all + off + i)),
            pl.BlockSpec(ovl.shape, lambda b, g, i: (0, 0)),
            pl.BlockSpec(memory_space=pl.ANY),
        ],
        out_specs=pl.BlockSpec((span, NSA_REP * HEAD_DIM), lambda b, g, i: (b * nq_all + off + i, g)),
        out_shape=jax.ShapeDtypeStruct(prev.shape, prev.dtype),
        input_output_aliases={9: 0},
        scratch_shapes=[per_tile((nsb, Q_TILE), F32), per_tile((nsb, Q_TILE), F32),
                        per_tile((2 * HEAD_DIM, rows), BF16),
                        per_tile((SEL_CHUNK, rows), F32), per_tile((SEL_CHUNK, rows), F32),
                        per_tile((SEL_CHUNK, rows), BF16),
                        per_tile((1, rows), F32), per_tile((1, rows), F32),
                        per_tile((HEAD_DIM, rows), F32), per_tile((HEAD_DIM, rows), F32),
                        per_tile((WINDOW + Q_TILE, rows), BF16), per_tile((1, rows), F32)],
        compiler_params=pltpu.CompilerParams(
            dimension_semantics=("parallel", "parallel", "arbitrary"), vmem_limit_bytes=VMEM_LIMIT),
        name="nsa",
    )(q_t, cc, cc_t, ksel[None], kwin[None], vt, vt, gates_t, ovl, prev)


def _hgrn_body(hq_ref, hf_ref, hi_ref, hg_ref, lbl_ref, gain_ref, o_ref,
               state_ref, b_ref, k_ref, v_ref, q_ref, *, layer):
    @pl.when(pl.program_id(1) == 0)
    def _():
        state_ref[...] = jnp.zeros_like(state_ref)

    logits = lbl_ref[...]
    ex = jnp.exp(logits - jnp.max(logits, axis=0, keepdims=True))
    sm = ex / jnp.sum(ex, axis=0, keepdims=True)
    lb_all = jnp.zeros((1, HG_HEADS * HG_DK), F32)
    for i in range(1, layer + 1):
        lb_all = lb_all + sm[i:i + 1, :]

    ch, blk, half = HG_CHUNK, HG_SUB, HG_SUB // 2
    nblk = ch // blk
    ri = lax.broadcasted_iota(jnp.int32, (ch, ch), 0)
    ci = lax.broadcasted_iota(jnp.int32, (ch, ch), 1)
    tril = jnp.where(ri >= ci, 1.0, 0.0).astype(BF16)
    blk_row = lax.broadcasted_iota(jnp.int32, (blk, HG_DK), 0)
    low_rows = blk_row >= half
    rr = lax.broadcasted_iota(jnp.int32, (blk, blk), 0)
    cc = lax.broadcasted_iota(jnp.int32, (blk, blk), 1)
    cross = (rr >= half) & (cc < half)
    half_row = lax.broadcasted_iota(jnp.int32, (half, HG_DK), 0)
    gain = gain_ref[...]
    heads = range(HG_HEADS)

    def chunk(j, carry):
        r0 = pl.multiple_of(j * ch, ch)
        for h in heads:
            cols = slice(h * HG_DK, (h + 1) * HG_DK)
            lb = lb_all[:, cols]
            hq = hq_ref[0, pl.ds(r0, ch), cols]
            hf = hf_ref[0, pl.ds(r0, ch), cols]
            f = lb + (1.0 - lb) * jax.nn.sigmoid(hf)
            lf = _split_bf16(jnp.log(jnp.maximum(f, 1e-30)), 3)
            b_ref[h] = _dot(tril, lf[0]) + _dot(tril, lf[1]) + _dot(tril, lf[2])
            k_ref[h] = 1.0 - f
            v_ref[h] = hi_ref[0, pl.ds(r0, ch), cols]
            q_ref[h] = hq * jax.nn.sigmoid(hq)

        o_blocks, tables = [], []
        for h in heads:
            inter = _dot_nt((q_ref[h] * jnp.exp(b_ref[h])).astype(BF16), state_ref[h].astype(BF16))
            o_blocks.append([inter[ib * blk:(ib + 1) * blk, :] for ib in range(nblk)])
            tables.append([])
            for ib in range(nblk):
                i0 = ib * blk
                qi = q_ref[h, i0:i0 + blk, :]
                bi = b_ref[h, i0:i0 + blk, :]
                a = None
                if ib > 0:
                    ref_b = b_ref[h, i0 - 1:i0, :]
                    qd = (qi * jnp.exp(bi - ref_b)).astype(BF16)
                    kd = (k_ref[h, 0:i0, :] * jnp.exp(ref_b - b_ref[h, 0:i0, :])).astype(BF16)
                    a = _dot_nt(qd, kd).astype(BF16)
                mid_b = b_ref[h, i0 + half - 1:i0 + half, :]
                qd2 = (qi * jnp.exp(jnp.where(low_rows, bi - mid_b, 0.0))).astype(BF16)
                kd2 = (k_ref[h, i0:i0 + blk, :] * jnp.exp(jnp.where(low_rows, 0.0, mid_b - bi))).astype(BF16)
                a2 = jnp.where(cross, _dot_nt(qd2, kd2), 0.0).astype(BF16)
                tables[h].append((a, a2))
        for h in heads:
            bl = b_ref[h, ch - 1:ch, :]
            kd_end = (k_ref[h] * jnp.exp(bl - b_ref[h])).astype(BF16)
            vt = v_ref[h].T.astype(BF16)
            state_ref[h] = state_ref[h] * jnp.exp(bl) + _dot(vt, kd_end)

        for h in heads:
            for ib in range(nblk):
                i0 = ib * blk
                a, a2 = tables[h][ib]
                o_i = o_blocks[h][ib] + _dot(a2, v_ref[h, i0:i0 + blk, :].astype(BF16))
                if a is not None:
                    o_i = o_i + _dot(a, v_ref[h, 0:i0, :].astype(BF16))
                o_blocks[h][ib] = o_i

        nhalf = ch // half
        diag = [[jnp.zeros((half, HG_DV), F32) for _ in range(nhalf)] for _ in heads]
        for s in range(half):
            for h in heads:
                for ih in range(nhalf):
                    g0 = ih * half
                    bs = b_ref[h, g0 + s:g0 + s + 1, :]
                    ks = k_ref[h, g0 + s:g0 + s + 1, :]
                    vs = v_ref[h, g0 + s:g0 + s + 1, :]
                    bi = b_ref[h, g0:g0 + half, :]
                    qi = q_ref[h, g0:g0 + half, :]
                    dec = jnp.exp(jnp.where(half_row >= s, bi - bs, NEG))
                    a_s = jnp.sum(qi * dec * ks, axis=-1, keepdims=True)
                    diag[h][ih] = diag[h][ih] + a_s * vs

        for h in heads:
            cols = slice(h * HG_DK, (h + 1) * HG_DK)
            o = jnp.concatenate(o_blocks[h], axis=0) + jnp.concatenate(diag[h], axis=0)
            o = o * lax.rsqrt(jnp.mean(o * o, axis=-1, keepdims=True) + EPS) * gain
            hg = hg_ref[0, pl.ds(r0, ch), cols]
            o_ref[pl.ds(r0, ch), cols] = (o * (hg * jax.nn.sigmoid(hg))).astype(o_ref.dtype)
        return carry

    lax.fori_loop(0, hq_ref.shape[1] // ch, chunk, 0)


def _hgrn(h_all, lb_logits, out_gain, layer, batch, seq):
    t = h_all.shape[1]
    nblk = seq // HG_ROWS
    nl = lb_logits.shape[0]
    hw = HG_HEADS * HG_DK
    hspec = lambda which: pl.BlockSpec((1, HG_ROWS, hw), lambda b, c: (which, b * nblk + c, 0))
    return pl.pallas_call(
        functools.partial(_hgrn_body, layer=layer),
        grid=(batch, nblk),
        in_specs=[hspec(0), hspec(1), hspec(2), hspec(3),
                  pl.BlockSpec((nl, hw), lambda b, c: (0, 0)),
                  pl.BlockSpec((1, HG_DV), lambda b, c: (0, 0))],
        out_specs=pl.BlockSpec((HG_ROWS, HG_HEADS * HG_DV), lambda b, c: (b * nblk + c, 0)),
        out_shape=jax.ShapeDtypeStruct((t, HG_HEADS * HG_DV), BF16),
        scratch_shapes=[pltpu.VMEM((HG_HEADS, HG_DV, HG_DK), F32)]
        + [pltpu.VMEM((HG_HEADS, HG_CHUNK, HG_DK), F32)] * 4,
        compiler_params=pltpu.CompilerParams(
            dimension_semantics=("parallel", "arbitrary"), vmem_limit_bytes=VMEM_LIMIT),
        name="hgrn",
    )(h_all, h_all, h_all, h_all, lb_logits, out_gain.reshape(1, HG_DV))


def _relayout_w_in(w):
    d = w.shape[0]
    nq = NSA_HEADS * HEAD_DIM
    nkv = 6 * NSA_GROUPS * HEAD_DIM
    ng = 3 * NSA_HEADS
    q_kv = w[:, :nq + nkv]
    gts = w[:, nq + nkv:nq + nkv + ng]
    hg = w[:, nq + nkv + ng:]
    per = 3 * NSA_REP
    gate_blocks = [jnp.pad(gts[:, gi * per:(gi + 1) * per], ((0, 0), (0, LANES - per)))
                   for gi in range(NSA_GROUPS)]
    out = jnp.concatenate([q_kv, hg] + gate_blocks, axis=1).astype(BF16)
    assert out.shape == (d, _C_END)
    return out


def _compress_weights(pos, w1, w2):
    eye = jnp.eye(NSA_GROUPS, dtype=F32)
    e = w1.shape[-1]

    def big(w1_half):
        y = jnp.einsum("sldh,gk->slgdkh", w1_half, eye)
        return y.reshape(2, CMP_STRIDE * NSA_GROUPS * HEAD_DIM, NSA_GROUPS * e).astype(BF16)

    def pos_rows(p_half):
        y = jnp.broadcast_to(p_half[:, :, None, :], (2, CMP_STRIDE, NSA_GROUPS, HEAD_DIM))
        return y.reshape(2, 1, CMP_STRIDE * NSA_GROUPS * HEAD_DIM)

    w2b = jnp.einsum("shd,gk->sghkd", w2, eye).reshape(2, NSA_GROUPS * e, NSA_GROUPS * HEAD_DIM)
    return (big(w1[:, :CMP_STRIDE]), big(w1[:, CMP_STRIDE:]),
            pos_rows(pos[:, :CMP_STRIDE]), pos_rows(pos[:, CMP_STRIDE:]), w2b.astype(BF16))


def kernel(x, ffn1_norm, ffn1_w_gate, ffn1_w_up, ffn1_w_down, mix_norm, w_in, q_norm, k_norm,
           cmp_pos, cmp_w1, cmp_w2, hgrn_lb_logits, hgrn_out_norm, w_out,
           ffn2_norm, ffn2_w_gate, ffn2_w_up, ffn2_w_down):
    batch, seq, d = x.shape
    depth = w_in.shape[0]
    assert CMP_BLOCK == 2 * CMP_STRIDE and seq % SEL_CHUNK == 0 and seq >= WINDOW + Q_TILE
    assert seq % HG_ROWS == 0 and SEL_CHUNK % SEL_BLOCK == 0 and SEL_CHUNK % (NSA_TILES * Q_TILE) == 0
    assert seq % (2 * SEL_CHUNK) == 0
    t = batch * seq
    tm_proj = 1024

    inv = 1.0 / (ROPE_THETA ** (jnp.arange(0, HEAD_DIM, 2, dtype=F32) / HEAD_DIM))
    ang = jnp.arange(seq, dtype=F32)[:, None] * inv[None, :]
    ang = jnp.concatenate([ang, ang], axis=-1)
    cos1, sin1 = jnp.cos(ang), jnp.sin(ang)
    cos2, sin2 = jnp.tile(cos1, (1, 2)), jnp.tile(sin1, (1, 2))
    cost, sint = cos1.T, sin1.T

    nseg = seq // CMP_STRIDE
    nsb = seq // SEL_BLOCK
    ci = np.arange(nseg)[None, :]
    sj = np.arange(nsb)[:, None]
    ovl = ((ci * CMP_STRIDE <= sj * SEL_BLOCK + SEL_BLOCK - 1)
           & (ci * CMP_STRIDE + CMP_BLOCK - 1 >= sj * SEL_BLOCK)
           & (ci < (seq - CMP_BLOCK) // CMP_STRIDE + 1))
    ovl = jnp.asarray(ovl, dtype=BF16)

    xt = x.reshape(t, d)
    for l in range(depth):
        xt = _ffn(xt, ffn1_norm, ffn1_w_gate, ffn1_w_up, ffn1_w_down, l)
        qgt = jnp.broadcast_to(q_norm[l][:, None], (HEAD_DIM, tm_proj))
        kg2 = jnp.tile(k_norm[l], (1, 2))
        q_t, cmp_in, ksel, kwin, vt, gates_t, h_all = _inproj(
            xt, mix_norm[l], _relayout_w_in(w_in[l]), cos2, sin2, cost, sint, qgt, kg2,
            batch, seq, tm=tm_proj)
        w1a, w1b, pa, pb, w2b = _compress_weights(cmp_pos[l], cmp_w1[l], cmp_w2[l])
        cc, cc_t = _compress(cmp_in, w1a, w1b, pa, pb, w2b, batch, seq)
        qr = 2 * SEL_CHUNK
        o_nsa = jnp.zeros((t, NSA_HEADS * HEAD_DIM), BF16)
        for lo in range(0, seq, qr):
            o_nsa = _nsa(q_t, cc, cc_t, ksel, kwin, vt, gates_t, ovl, o_nsa, batch, seq, lo, lo + qr)
        o_hg = _hgrn(h_all, hgrn_lb_logits, hgrn_out_norm[l], l, batch, seq)
        xt = _ffn(xt, ffn2_norm, ffn2_w_gate, ffn2_w_up, ffn2_w_down, l, mix=(o_nsa, o_hg, w_out))
    return xt.reshape(batch, seq, d)
```

```python
import functools

import jax
import jax.numpy as jnp
import numpy as np
from jax import lax
from jax.experimental import pallas as pl
from jax.experimental.pallas import tpu as pltpu

NSA_HEADS = 8
NSA_GROUPS = 2
NSA_REP = NSA_HEADS // NSA_GROUPS
HEAD_DIM = 64
CMP_BLOCK = 32
CMP_STRIDE = 16
CMP_HIDDEN = 128
SEL_BLOCK = 64
SEL_TOPN = 16
WINDOW = 512
HG_HEADS = 4
HG_DK = 128
HG_DV = 128
ROPE_THETA = 10000.0
EPS = 1e-6
NEG = -1e30
LOG2E = 1.4426950408889634

LANES = 128
Q_TILE = 128
NSA_TILES = 2
SEL_CHUNK = 512
HG_CHUNK = 64
HG_SUB = 16
HG_ROWS = 512
VMEM_LIMIT = 56 * 1024 * 1024

BF16 = jnp.bfloat16
F32 = jnp.float32


def _dot(a, b):
    return jnp.dot(a, b, preferred_element_type=F32)


def _dot_nt(a, b):
    return lax.dot_general(a, b, (((1,), (1,)), ((), ())), preferred_element_type=F32)


def _split_bf16(x, parts):
    out = []
    rem = x
    for _ in range(parts):
        p = rem.astype(BF16)
        out.append(p)
        rem = rem - p.astype(F32)
    return out


def _ffn_body(*refs, fuse_mix):
    if fuse_mix:
        (x_ref, ma_ref, mb_ref, wo_ref, g_ref, wg_ref, wu_ref, wd_ref,
         o_ref, xs_ref, hn_ref, acc_ref) = refs
    else:
        x_ref, g_ref, wg_ref, wu_ref, wd_ref, o_ref, xs_ref, hn_ref, acc_ref = refs
    f = pl.program_id(1)

    @pl.when(f == 0)
    def _():
        x = x_ref[...]
        if fuse_mix:
            mix = jnp.concatenate([ma_ref[...], mb_ref[...]], axis=1)
            x = x + _dot(mix, wo_ref[0].astype(BF16))
        xs_ref[...] = x
        ms = jnp.mean(x * x, axis=-1, keepdims=True)
        hn_ref[...] = (x * lax.rsqrt(ms + EPS) * g_ref[0]).astype(BF16)
        acc_ref[...] = jnp.zeros_like(acc_ref)

    hn = hn_ref[...]
    gate = _dot(hn, wg_ref[0].astype(BF16))
    up = _dot(hn, wu_ref[0].astype(BF16))
    act = (gate * jax.nn.sigmoid(gate) * up).astype(BF16)
    acc_ref[...] += _dot(act, wd_ref[0].astype(BF16))

    @pl.when(f == pl.num_programs(1) - 1)
    def _():
        o_ref[...] = xs_ref[...] + 0.5 * acc_ref[...]


def _ffn(x, g, wg, wu, wd, layer, mix=None, *, tm=1024, tf=256):
    t, d = x.shape
    ff = wg.shape[2]
    assert t % tm == 0 and ff % tf == 0
    fuse = mix is not None
    row = lambda i, f: (i, 0)
    in_specs = [pl.BlockSpec((tm, d), row)]
    args = [x]
    if fuse:
        ma, mb, wo = mix
        da, db = ma.shape[1], mb.shape[1]
        assert da == db and wo.shape[1] == da + db
        in_specs += [pl.BlockSpec((tm, da), row), pl.BlockSpec((tm, db), row),
                     pl.BlockSpec((1, da + db, d), lambda i, f: (layer, 0, 0))]
        args += [ma, mb, wo]
    in_specs += [pl.BlockSpec((1, 1, d), lambda i, f: (layer, 0, 0)),
                 pl.BlockSpec((1, d, tf), lambda i, f: (layer, 0, f)),
                 pl.BlockSpec((1, d, tf), lambda i, f: (layer, 0, f)),
                 pl.BlockSpec((1, tf, d), lambda i, f: (layer, f, 0))]
    args += [g.reshape(g.shape[0], 1, d), wg, wu, wd]
    return pl.pallas_call(
        functools.partial(_ffn_body, fuse_mix=fuse),
        grid=(t // tm, ff // tf),
        in_specs=in_specs,
        out_specs=pl.BlockSpec((tm, d), row),
        out_shape=jax.ShapeDtypeStruct((t, d), F32),
        scratch_shapes=[pltpu.VMEM((tm, d), F32), pltpu.VMEM((tm, d), BF16), pltpu.VMEM((tm, d), F32)],
        compiler_params=pltpu.CompilerParams(
            dimension_semantics=("parallel", "arbitrary"), vmem_limit_bytes=VMEM_LIMIT),
        name="ffn_mix" if fuse else "ffn",
    )(*args)


_C_Q = 0
_C_KV = NSA_HEADS * HEAD_DIM
_C_HG = _C_KV + 6 * LANES
_C_GATE = _C_HG + 4 * HG_HEADS * HG_DK
_C_END = _C_GATE + NSA_GROUPS * LANES
GATE_ROWS = 16
Q_SCALE = HEAD_DIM ** -0.5 * LOG2E


def _inproj_body(x_ref, g_ref, w_ref, cos_ref, sin_ref, cost_ref, sint_ref, qg_ref, kg_ref,
                 q_ref, cmp_ref, ksel_ref, kwin_ref, vt_ref, gate_ref, h_ref, seg_ref, *, tiles_per_seq):
    x = x_ref[...]
    ms = jnp.mean(x * x, axis=-1, keepdims=True)
    hn = (x * lax.rsqrt(ms + EPS) * g_ref[...]).astype(BF16)
    cos = cos_ref[...]
    sin = sin_ref[...]
    tm = x.shape[0]
    half = HEAD_DIM // 2

    def proj(c0, width):
        return _dot(hn, w_ref[:, c0:c0 + width])

    ri = (lax.broadcasted_iota(jnp.int32, (2 * LANES, LANES), 0) % LANES) // HEAD_DIM
    ci = lax.broadcasted_iota(jnp.int32, (2 * LANES, LANES), 1) // HEAD_DIM
    head_mean = jnp.where(ri == ci, 1.0 / HEAD_DIM, 0.0).astype(BF16)
    lane = lax.broadcasted_iota(jnp.int32, (tm, LANES), 1)
    first_half = (lane % HEAD_DIM) < half

    def norm_rope(y, gain):
        msq = _dot(jnp.concatenate(_split_bf16(y * y, 2), axis=1), head_mean)
        yn = y * lax.rsqrt(msq + EPS) * gain
        rot = jnp.where(first_half, -pltpu.roll(yn, LANES - half, 1), pltpu.roll(yn, half, 1))
        return yn * cos + rot * sin

    def heads_t(y):
        return y.T.reshape(2, HEAD_DIM, tm)

    qgain = (qg_ref[...] * Q_SCALE)[None]
    cost = cost_ref[...][None]
    sint = sint_ref[...][None]
    yq = proj(_C_Q, NSA_HEADS * HEAD_DIM)
    ykv = proj(_C_KV, 6 * LANES)
    kv = lambda j: ykv[:, j * LANES:(j + 1) * LANES]
    for c in range(NSA_HEADS // 2):
        y3 = heads_t(yq[:, c * LANES:(c + 1) * LANES])
        yn = y3 * lax.rsqrt(jnp.mean(y3 * y3, axis=1, keepdims=True) + EPS) * qgain
        rot = jnp.concatenate([-yn[:, half:], yn[:, :half]], axis=1)
        q_ref[0, 2 * c:2 * c + 2] = (yn * cost + rot * sint).astype(BF16)

    seg_ref[0] = norm_rope(kv(0), kg_ref[0:1, :])
    seg_ref[1] = kv(1)
    nseg_t = tm // CMP_STRIDE
    for s in range(2):
        for l in range(CMP_STRIDE):
            cmp_ref[s, :, l * LANES:(l + 1) * LANES] = seg_ref[s, pl.ds(l, nseg_t, stride=CMP_STRIDE), :].astype(BF16)
    ksel = norm_rope(kv(2), kg_ref[1:2, :])
    pos = (pl.program_id(0) % tiles_per_seq) * tm + lax.broadcasted_iota(jnp.int32, (tm, HEAD_DIM), 0)
    blk_in_chunk = (pos // SEL_BLOCK) % (SEL_CHUNK // SEL_BLOCK)
    onehot = jnp.where(lax.broadcasted_iota(jnp.int32, (tm, HEAD_DIM), 1) == blk_in_chunk, 1.0, 0.0)
    for gi in range(NSA_GROUPS):
        ksel_ref[0, gi] = jnp.concatenate(
            [ksel[:, gi * HEAD_DIM:(gi + 1) * HEAD_DIM], onehot], axis=1).astype(BF16)
    kwin = norm_rope(kv(4), kg_ref[2:3, :]).astype(BF16)
    kwin_ref[0, 0] = kwin[:, :HEAD_DIM]
    kwin_ref[0, 1] = kwin[:, HEAD_DIM:]
    for which, col in ((0, 3), (1, 5)):
        vt_ref[which, 0] = heads_t(kv(col)).astype(BF16)

    hw = HG_HEADS * HG_DK
    for i in range(4):
        h_ref[i] = proj(_C_HG + i * hw, hw)
    ygate = proj(_C_GATE, NSA_GROUPS * LANES)
    for gi in range(NSA_GROUPS):
        gt = jax.nn.sigmoid(ygate[:, gi * LANES:(gi + 1) * LANES]).T
        gate_ref[gi] = gt[:GATE_ROWS]


def _inproj(x, g, w2, cos2, sin2, cost, sint, qgt, kg2, batch, seq, *, tm=256):
    t, d = x.shape
    nst = seq // tm
    hw = HG_HEADS * HG_DK
    out_shape = (
        jax.ShapeDtypeStruct((batch, NSA_HEADS, HEAD_DIM, seq), BF16),
        jax.ShapeDtypeStruct((2, t // CMP_STRIDE, CMP_STRIDE * LANES), BF16),
        jax.ShapeDtypeStruct((batch, NSA_GROUPS, seq, LANES), BF16),
        jax.ShapeDtypeStruct((batch, NSA_GROUPS, seq, HEAD_DIM), BF16),
        jax.ShapeDtypeStruct((2, batch, NSA_GROUPS, HEAD_DIM, seq), BF16),
        jax.ShapeDtypeStruct((NSA_GROUPS, GATE_ROWS, t), F32),
        jax.ShapeDtypeStruct((4, t, hw), F32),
    )
    out_specs = (
        pl.BlockSpec((1, NSA_HEADS, HEAD_DIM, tm), lambda i: (i // nst, 0, 0, i % nst)),
        pl.BlockSpec((2, tm // CMP_STRIDE, CMP_STRIDE * LANES), lambda i: (0, i, 0)),
        pl.BlockSpec((1, NSA_GROUPS, tm, LANES), lambda i: (i // nst, 0, i % nst, 0)),
        pl.BlockSpec((1, NSA_GROUPS, tm, HEAD_DIM), lambda i: (i // nst, 0, i % nst, 0)),
        pl.BlockSpec((2, 1, NSA_GROUPS, HEAD_DIM, tm), lambda i: (0, i // nst, 0, 0, i % nst)),
        pl.BlockSpec((NSA_GROUPS, GATE_ROWS, tm), lambda i: (0, 0, i)),
        pl.BlockSpec((4, tm, hw), lambda i: (0, i, 0)),
    )
    in_specs = [
        pl.BlockSpec((tm, d), lambda i: (i, 0)),
        pl.BlockSpec((1, d), lambda i: (0, 0)),
        pl.BlockSpec(w2.shape, lambda i: (0, 0)),
        pl.BlockSpec((tm, LANES), lambda i: (i % nst, 0)),
        pl.BlockSpec((tm, LANES), lambda i: (i % nst, 0)),
        pl.BlockSpec((HEAD_DIM, tm), lambda i: (0, i % nst)),
        pl.BlockSpec((HEAD_DIM, tm), lambda i: (0, i % nst)),
        pl.BlockSpec((HEAD_DIM, tm), lambda i: (0, 0)),
        pl.BlockSpec((3, LANES), lambda i: (0, 0)),
    ]
    return pl.pallas_call(
        functools.partial(_inproj_body, tiles_per_seq=nst),
        grid=(t // tm,),
        in_specs=in_specs,
        out_specs=out_specs,
        out_shape=out_shape,
        scratch_shapes=[pltpu.VMEM((2, tm, LANES), F32)],
        compiler_params=pltpu.CompilerParams(
            dimension_semantics=("parallel",), vmem_limit_bytes=VMEM_LIMIT),
        name="inproj",
    )(x, g.reshape(1, d), w2, cos2, sin2, cost, sint, qgt, kg2)


def _compress_body(seg_ref, w1a_ref, w1b_ref, pa_ref, pb_ref, w2_ref, o_ref, ot_ref):
    segs = seg_ref[0, 0]
    w1a = w1a_ref[0]
    w1b = w1b_ref[0]
    first = _dot(segs, w1a)
    second = _dot(segs, w1b)
    pa_hi, pa_lo = _split_bf16(jnp.broadcast_to(pa_ref[0], (8, pa_ref.shape[2])), 2)
    pb_hi, pb_lo = _split_bf16(jnp.broadcast_to(pb_ref[0], (8, pb_ref.shape[2])), 2)
    cpos = _dot(pa_hi, w1a) + _dot(pa_lo, w1a) + _dot(pb_hi, w1b) + _dot(pb_lo, w1b)
    nseg = first.shape[0]
    pre = first + pltpu.roll(second, nseg - 1, 0) + cpos[0:1, :]
    hid = jax.nn.gelu(pre, approximate=True).astype(BF16)
    out = _dot(hid, w2_ref[0])
    outb = out.astype(BF16)
    o_ref[0, 0, 0] = outb[:, :HEAD_DIM]
    o_ref[0, 0, 1] = outb[:, HEAD_DIM:]
    ot_ref[0, 0] = out.T.reshape(NSA_GROUPS, HEAD_DIM, nseg).astype(BF16)


def _compress(cmp_in, w1a, w1b, pa, pb, w2, batch, seq):
    nseg = seq // CMP_STRIDE
    width = CMP_STRIDE * LANES
    segs = cmp_in.reshape(2, batch, nseg, width)
    return pl.pallas_call(
        _compress_body,
        grid=(2, batch),
        in_specs=[
            pl.BlockSpec((1, 1, nseg, width), lambda s, b: (s, b, 0, 0)),
            pl.BlockSpec((1,) + w1a.shape[1:], lambda s, b: (s, 0, 0)),
            pl.BlockSpec((1,) + w1b.shape[1:], lambda s, b: (s, 0, 0)),
            pl.BlockSpec((1, 1, width), lambda s, b: (s, 0, 0)),
            pl.BlockSpec((1, 1, width), lambda s, b: (s, 0, 0)),
            pl.BlockSpec((1,) + w2.shape[1:], lambda s, b: (s, 0, 0)),
        ],
        out_specs=(pl.BlockSpec((1, 1, NSA_GROUPS, nseg, HEAD_DIM), lambda s, b: (s, b, 0, 0, 0)),
                   pl.BlockSpec((1, 1, NSA_GROUPS, HEAD_DIM, nseg), lambda s, b: (s, b, 0, 0, 0))),
        out_shape=(jax.ShapeDtypeStruct((2, batch, NSA_GROUPS, nseg, HEAD_DIM), BF16),
                   jax.ShapeDtypeStruct((2, batch, NSA_GROUPS, HEAD_DIM, nseg), BF16)),
        compiler_params=pltpu.CompilerParams(
            dimension_semantics=("parallel", "parallel"), vmem_limit_bytes=VMEM_LIMIT),
        name="compress",
    )(segs, w1a, w1b, pa, pb, w2)


def _tile_heads(a):
    return jnp.concatenate([a] * NSA_REP, axis=1)


def _nsa_body(q_ref, kcc_ref, vcct_ref, ks_ref, kw_ref, vst_ref, vwt_ref, gate_ref, ovl_ref, prev_ref,
              o_ref, imp_ref, selb_ref, qa_ref, sa_ref, sb_ref, p_ref, m_ref, l_ref, acc_ref,
              oc_ref, p3_ref, den3_ref, *, q_lo, q_hi):
    del prev_ref
    tiles = range(NSA_TILES)
    first = q_lo + pl.program_id(2) * (NSA_TILES * Q_TILE)
    starts = [first + t * Q_TILE for t in tiles]
    lanes = lambda t: slice(t * Q_TILE, (t + 1) * Q_TILE)
    q_t = [jnp.concatenate([q_ref[0, r, :, lanes(t)] for r in range(NSA_REP)], axis=1) for t in tiles]
    tq = [starts[t] + lax.broadcasted_iota(jnp.int32, (1, Q_TILE), 1) for t in tiles]

    kcc = kcc_ref[0, 0, 0]
    ncb = kcc.shape[0]
    wkeys = WINDOW + Q_TILE
    w0 = [pl.multiple_of(jnp.maximum(starts[t] - WINDOW, 0), Q_TILE) for t in tiles]
    s1 = [_dot(kcc, q_t[t]) for t in tiles]
    s3 = [_dot(kw_ref[0, 0, 0, pl.ds(w0[t], wkeys), :], q_t[t]) for t in tiles]

    cend = lax.broadcasted_iota(jnp.int32, (ncb, 1), 0) * CMP_STRIDE + (CMP_BLOCK - 1)
    p1 = []
    for t in tiles:
        sm1 = s1[t] + _tile_heads(jnp.where(cend <= tq[t], 0.0, NEG))
        e1 = jnp.exp2(sm1 - jnp.max(sm1, axis=0, keepdims=True))
        den1 = jnp.sum(e1, axis=0, keepdims=True)
        p1.append(e1 * jnp.where(_tile_heads(tq[t] >= CMP_BLOCK - 1), 1.0 / den1, 0.0))
    ovl = ovl_ref[...]
    nsb = ovl.shape[0]
    imp = []
    for t in tiles:
        oc_ref[t] = _dot(vcct_ref[0, 0, 0], p1[t].astype(BF16))
        p1sum = p1[t][:, 0:Q_TILE]
        for r in range(1, NSA_REP):
            p1sum = p1sum + p1[t][:, r * Q_TILE:(r + 1) * Q_TILE]
        ps_hi, ps_lo = _split_bf16(p1sum, 2)
        imp.append(_dot(ovl, ps_hi) + _dot(ovl, ps_lo))

    jb = lax.broadcasted_iota(jnp.int32, (nsb, Q_TILE), 0)
    valid, impm = [], []
    for t in tiles:
        cur = tq[t] // SEL_BLOCK
        forced = (jb == 0) | (jb == cur) | (jb == cur - 1)
        valid.append(jb * SEL_BLOCK <= tq[t])
        impm.append(jnp.where(valid[t], jnp.where(forced, jnp.inf, imp[t]), -jnp.inf))
        imp_ref[t] = impm[t]
    sub = lax.broadcasted_iota(jnp.int32, (8, Q_TILE), 0)
    ranks = [[jnp.zeros((8, Q_TILE), F32) for _ in range(nsb // 8)] for _ in tiles]
    seen = q_hi // SEL_BLOCK
    for i in range(seen if seen > SEL_TOPN else 0):
        for t in tiles:
            xi = jnp.broadcast_to(imp_ref[t, i:i + 1, :], (8, Q_TILE))
            for v in range(seen // 8):
                xv = impm[t][8 * v:8 * v + 8, :]
                ge = jnp.where(xi >= xv, 1.0, 0.0)
                gt = jnp.where(xi > xv, 1.0, 0.0)
                if 8 * v > i:
                    inc = ge
                elif 8 * v + 7 < i:
                    inc = gt
                else:
                    inc = jnp.where(sub + 8 * v > i, ge, gt)
                ranks[t][v] = ranks[t][v] + inc
    for t in tiles:
        rank = jnp.concatenate(ranks[t], axis=0)
        selb_ref[t] = jnp.where((rank < SEL_TOPN) & valid[t], 0.0, NEG)

    blocks_per_chunk = SEL_CHUNK // SEL_BLOCK
    last_chunk = ks_ref.shape[3] // SEL_CHUNK - 1
    n_chunks = (first + NSA_TILES * Q_TILE + SEL_CHUNK - 1) // SEL_CHUNK
    own_row = [pl.multiple_of(starts[t] - (n_chunks - 1) * SEL_CHUNK, Q_TILE) for t in tiles]
    own_mask = _tile_heads(jnp.where(
        lax.broadcasted_iota(jnp.int32, (Q_TILE, Q_TILE), 0) <= lax.broadcasted_iota(jnp.int32, (Q_TILE, Q_TILE), 1),
        0.0, 1.0))
    bias_pad = jnp.zeros((HEAD_DIM - blocks_per_chunk, NSA_REP * Q_TILE), F32)

    def scores(t, chunk):
        blk_bias = _tile_heads(selb_ref[t, pl.ds(pl.multiple_of(chunk * blocks_per_chunk, blocks_per_chunk),
                                                 blocks_per_chunk), :])
        qa_ref[t, HEAD_DIM:, :] = jnp.concatenate([blk_bias, bias_pad], axis=0).astype(BF16)
        k0 = pl.multiple_of(chunk * SEL_CHUNK, SEL_CHUNK)
        return _dot(ks_ref[0, 0, 0, pl.ds(k0, SEL_CHUNK), :], qa_ref[t])

    def put_scores(t, chunk, s_ref):
        s_ref[t] = scores(t, chunk)
        own_neg = jnp.where(chunk == n_chunks - 1, NEG, 0.0)
        s_ref[t, pl.ds(own_row[t], Q_TILE), :] = s_ref[t, pl.ds(own_row[t], Q_TILE), :] + own_mask * own_neg

    def sel_step(c, s_cur_ref, s_next_ref):
        m_old, m_new = [], []
        for t in tiles:
            m_old.append(m_ref[t])
            m_new.append(jnp.maximum(m_old[t], jnp.max(s_cur_ref[t], axis=0, keepdims=True)))
        nxt = jnp.minimum(c + 1, last_chunk)
        for t in tiles:
            put_scores(t, nxt, s_next_ref)
        v0 = pl.multiple_of(jnp.maximum(c - 1, 0) * SEL_CHUNK, SEL_CHUNK)
        v_prev = vst_ref[0, 0, 0, :, pl.ds(v0, SEL_CHUNK)]
        pv = [_dot(v_prev, p_ref[t]) for t in tiles]
        for t in tiles:
            p = jnp.exp2(s_cur_ref[t] - m_new[t])
            alpha = jnp.exp2(m_old[t] - m_new[t])
            l_ref[t] = alpha * l_ref[t] + jnp.sum(p, axis=0, keepdims=True)
            acc_ref[t] = alpha * (acc_ref[t] + pv[t])
            m_ref[t] = m_new[t]
            p_ref[t] = p.astype(BF16)

    for t in tiles:
        qa_ref[t, 0:HEAD_DIM, :] = q_t[t]
        put_scores(t, 0, sa_ref)
    p_ref[...] = jnp.zeros_like(p_ref)

    for t in tiles:
        dist = tq[t] - (w0[t] + lax.broadcasted_iota(jnp.int32, (wkeys, 1), 0))
        sm3 = s3[t] + _tile_heads(jnp.where((dist >= 0) & (dist < WINDOW), 0.0, NEG))
        e3 = jnp.exp2(sm3 - jnp.max(sm3, axis=0, keepdims=True))
        den3_ref[t] = jnp.sum(e3, axis=0, keepdims=True)
        p3_ref[t] = e3.astype(BF16)

    m_ref[...] = jnp.full(m_ref.shape, NEG, F32)
    l_ref[...] = jnp.zeros_like(l_ref)
    acc_ref[...] = jnp.zeros_like(acc_ref)

    def sel_pair(i, carry):
        sel_step(2 * i, sa_ref, sb_ref)

        @pl.when(2 * i + 1 < n_chunks)
        def _():
            sel_step(2 * i + 1, sb_ref, sa_ref)
        return carry

    lax.fori_loop(0, (n_chunks + 1) // 2, sel_pair, 0)

    vl = pl.multiple_of((n_chunks - 1) * SEL_CHUNK, SEL_CHUNK)
    v_last = vst_ref[0, 0, 0, :, pl.ds(vl, SEL_CHUNK)]
    acc_w = [_dot(vwt_ref[0, 0, 0, :, pl.ds(w0[t], wkeys)], p3_ref[t]) for t in tiles]
    acc_s = [acc_ref[t] + _dot(v_last, p_ref[t]) for t in tiles]
    for t in tiles:
        gates = gate_ref[0, :, lanes(t)]
        g_c = jnp.concatenate([gates[3 * r:3 * r + 1, :] for r in range(NSA_REP)], axis=1)
        g_s = jnp.concatenate([gates[3 * r + 1:3 * r + 2, :] for r in range(NSA_REP)], axis=1)
        g_w = jnp.concatenate([gates[3 * r + 2:3 * r + 3, :] for r in range(NSA_REP)], axis=1)
        out_t = g_c * oc_ref[t] + (g_w / den3_ref[t]) * acc_w[t] + (g_s / l_ref[t]) * acc_s[t]
        out_t = jnp.concatenate([out_t[:, r * Q_TILE:(r + 1) * Q_TILE] for r in range(NSA_REP)], axis=0)
        o_ref[lanes(t), :] = out_t.T.astype(o_ref.dtype)


def _nsa(q_t, cc, cc_t, ksel, kwin, vt, gates_t, ovl, prev, batch, seq, q_lo, q_hi):
    span = NSA_TILES * Q_TILE
    nqb = (q_hi - q_lo) // span
    off = q_lo // span
    nq_all = seq // span
    ncb_pad = cc.shape[3]
    nsb = seq // SEL_BLOCK
    rows = NSA_REP * Q_TILE
    vspec = lambda which: pl.BlockSpec((1, 1, 1, HEAD_DIM, seq), lambda b, g, i: (which, b, g, 0, 0))
    per_tile = lambda shape, dtype: pltpu.VMEM((NSA_TILES,) + shape, dtype)
    return pl.pallas_call(
        functools.partial(_nsa_body, q_lo=q_lo, q_hi=q_hi),
        grid=(batch, NSA_GROUPS, nqb),
        in_specs=[
            pl.BlockSpec((1, NSA_REP, HEAD_DIM, span), lambda b, g, i: (b, g, 0, off + i)),
            pl.BlockSpec((1, 1, 1, ncb_pad, HEAD_DIM), lambda b, g, i: (0, b, g, 0, 0)),
            pl.BlockSpec((1, 1, 1, HEAD_DIM, ncb_pad), lambda b, g, i: (1, b, g, 0, 0)),
            pl.BlockSpec((1, 1, 1, seq, LANES), lambda b, g, i: (0, b, g, 0, 0)),
            pl.BlockSpec((1, 1, 1, seq, HEAD_DIM), lambda b, g, i: (0, b, g, 0, 0)),
            vspec(0), vspec(1),
            pl.BlockSpec((1, GATE_ROWS, span), lambda b, g, i: (g, 0, b * nq_all + off + i)),
            pl.BlockSpec(ovl.shape, lambda b, g, i: (0, 0)),
            pl.BlockSpec(memory_space=pl.ANY),
        ],
        out_specs=pl.BlockSpec((span, NSA_REP * HEAD_DIM), lambda b, g, i: (b * nq_all + off + i, g)),
        out_shape=jax.ShapeDtypeStruct(prev.shape, prev.dtype),
        input_output_aliases={9: 0},
        scratch_shapes=[per_tile((nsb, Q_TILE), F32), per_tile((nsb, Q_TILE), F32),
                        per_tile((2 * HEAD_DIM, rows), BF16),
                        per_tile((SEL_CHUNK, rows), F32), per_tile((SEL_CHUNK, rows), F32),
                        per_tile((SEL_CHUNK, rows), BF16),
                        per_tile((1, rows), F32), per_tile((1, rows), F32),
                        per_tile((HEAD_DIM, rows), F32), per_tile((HEAD_DIM, rows), F32),
                        per_tile((WINDOW + Q_TILE, rows), BF16), per_tile((1, rows), F32)],
        compiler_params=pltpu.CompilerParams(
            dimension_semantics=("parallel", "parallel", "arbitrary"), vmem_limit_bytes=VMEM_LIMIT),
        name="nsa",
    )(q_t, cc, cc_t, ksel[None], kwin[None], vt, vt, gates_t, ovl, prev)


def _hgrn_body(hq_ref, hf_ref, hi_ref, hg_ref, lbl_ref, gain_ref, o_ref,
               state_ref, b_ref, k_ref, v_ref, q_ref, *, layer):
    @pl.when(pl.program_id(1) == 0)
    def _():
        state_ref[...] = jnp.zeros_like(state_ref)

    logits = lbl_ref[...]
    ex = jnp.exp(logits - jnp.max(logits, axis=0, keepdims=True))
    sm = ex / jnp.sum(ex, axis=0, keepdims=True)
    lb_all = jnp.zeros((1, HG_HEADS * HG_DK), F32)
    for i in range(1, layer + 1):
        lb_all = lb_all + sm[i:i + 1, :]

    ch, blk, half = HG_CHUNK, HG_SUB, HG_SUB // 2
    nblk = ch // blk
    ri = lax.broadcasted_iota(jnp.int32, (ch, ch), 0)
    ci = lax.broadcasted_iota(jnp.int32, (ch, ch), 1)
    tril = jnp.where(ri >= ci, 1.0, 0.0).astype(BF16)
    blk_row = lax.broadcasted_iota(jnp.int32, (blk, HG_DK), 0)
    low_rows = blk_row >= half
    rr = lax.broadcasted_iota(jnp.int32, (blk, blk), 0)
    cc = lax.broadcasted_iota(jnp.int32, (blk, blk), 1)
    cross = (rr >= half) & (cc < half)
    half_row = lax.broadcasted_iota(jnp.int32, (half, HG_DK), 0)
    gain = gain_ref[...]
    heads = range(HG_HEADS)

    def chunk(j, carry):
        r0 = pl.multiple_of(j * ch, ch)
        for h in heads:
            cols = slice(h * HG_DK, (h + 1) * HG_DK)
            lb = lb_all[:, cols]
            hq = hq_ref[0, pl.ds(r0, ch), cols]
            hf = hf_ref[0, pl.ds(r0, ch), cols]
            f = lb + (1.0 - lb) * jax.nn.sigmoid(hf)
            lf = _split_bf16(jnp.log(jnp.maximum(f, 1e-30)), 3)
            b_ref[h] = _dot(tril, lf[0]) + _dot(tril, lf[1]) + _dot(tril, lf[2])
            k_ref[h] = 1.0 - f
            v_ref[h] = hi_ref[0, pl.ds(r0, ch), cols]
            q_ref[h] = hq * jax.nn.sigmoid(hq)

        o_blocks, tables = [], []
        for h in heads:
            inter = _dot_nt((q_ref[h] * jnp.exp(b_ref[h])).astype(BF16), state_ref[h].astype(BF16))
            o_blocks.append([inter[ib * blk:(ib + 1) * blk, :] for ib in range(nblk)])
            tables.append([])
            for ib in range(nblk):
                i0 = ib * blk
                qi = q_ref[h, i0:i0 + blk, :]
                bi = b_ref[h, i0:i0 + blk, :]
                a = None
                if ib > 0:
                    ref_b = b_ref[h, i0 - 1:i0, :]
                    qd = (qi * jnp.exp(bi - ref_b)).astype(BF16)
                    kd = (k_ref[h, 0:i0, :] * jnp.exp(ref_b - b_ref[h, 0:i0, :])).astype(BF16)
                    a = _dot_nt(qd, kd).astype(BF16)
                mid_b = b_ref[h, i0 + half - 1:i0 + half, :]
                qd2 = (qi * jnp.exp(jnp.where(low_rows, bi - mid_b, 0.0))).astype(BF16)
                kd2 = (k_ref[h, i0:i0 + blk, :] * jnp.exp(jnp.where(low_rows, 0.0, mid_b - bi))).astype(BF16)
                a2 = jnp.where(cross, _dot_nt(qd2, kd2), 0.0).astype(BF16)
                tables[h].append((a, a2))
        for h in heads:
            bl = b_ref[h, ch - 1:ch, :]
            kd_end = (k_ref[h] * jnp.exp(bl - b_ref[h])).astype(BF16)
            vt = v_ref[h].T.astype(BF16)
            state_ref[h] = state_ref[h] * jnp.exp(bl) + _dot(vt, kd_end)

        for h in heads:
            for ib in range(nblk):
                i0 = ib * blk
                a, a2 = tables[h][ib]
                o_i = o_blocks[h][ib] + _dot(a2, v_ref[h, i0:i0 + blk, :].astype(BF16))
                if a is not None:
                    o_i = o_i + _dot(a, v_ref[h, 0:i0, :].astype(BF16))
                o_blocks[h][ib] = o_i

        nhalf = ch // half
        diag = [[jnp.zeros((half, HG_DV), F32) for _ in range(nhalf)] for _ in heads]
        for s in range(half):
            for h in heads:
                for ih in range(nhalf):
                    g0 = ih * half
                    bs = b_ref[h, g0 + s:g0 + s + 1, :]
                    ks = k_ref[h, g0 + s:g0 + s + 1, :]
                    vs = v_ref[h, g0 + s:g0 + s + 1, :]
                    bi = b_ref[h, g0:g0 + half, :]
                    qi = q_ref[h, g0:g0 + half, :]
                    dec = jnp.exp(jnp.where(half_row >= s, bi - bs, NEG))
                    a_s = jnp.sum(qi * dec * ks, axis=-1, keepdims=True)
                    diag[h][ih] = diag[h][ih] + a_s * vs

        for h in heads:
            cols = slice(h * HG_DK, (h + 1) * HG_DK)
            o = jnp.concatenate(o_blocks[h], axis=0) + jnp.concatenate(diag[h], axis=0)
            o = o * lax.rsqrt(jnp.mean(o * o, axis=-1, keepdims=True) + EPS) * gain
            hg = hg_ref[0, pl.ds(r0, ch), cols]
            o_ref[pl.ds(r0, ch), cols] = (o * (hg * jax.nn.sigmoid(hg))).astype(o_ref.dtype)
        return carry

    lax.fori_loop(0, hq_ref.shape[1] // ch, chunk, 0)


def _hgrn(h_all, lb_logits, out_gain, layer, batch, seq):
    t = h_all.shape[1]
    nblk = seq // HG_ROWS
    nl = lb_logits.shape[0]
    hw = HG_HEADS * HG_DK
    hspec = lambda which: pl.BlockSpec((1, HG_ROWS, hw), lambda b, c: (which, b * nblk + c, 0))
    return pl.pallas_call(
        functools.partial(_hgrn_body, layer=layer),
        grid=(batch, nblk),
        in_specs=[hspec(0), hspec(1), hspec(2), hspec(3),
                  pl.BlockSpec((nl, hw), lambda b, c: (0, 0)),
                  pl.BlockSpec((1, HG_DV), lambda b, c: (0, 0))],
        out_specs=pl.BlockSpec((HG_ROWS, HG_HEADS * HG_DV), lambda b, c: (b * nblk + c, 0)),
        out_shape=jax.ShapeDtypeStruct((t, HG_HEADS * HG_DV), BF16),
        scratch_shapes=[pltpu.VMEM((HG_HEADS, HG_DV, HG_DK), F32)]
        + [pltpu.VMEM((HG_HEADS, HG_CHUNK, HG_DK), F32)] * 4,
        compiler_params=pltpu.CompilerParams(
            dimension_semantics=("parallel", "arbitrary"), vmem_limit_bytes=VMEM_LIMIT),
        name="hgrn",
    )(h_all, h_all, h_all, h_all, lb_logits, out_gain.reshape(1, HG_DV))


def _relayout_w_in(w):
    d = w.shape[0]
    nq = NSA_HEADS * HEAD_DIM
    nkv = 6 * NSA_GROUPS * HEAD_DIM
    ng = 3 * NSA_HEADS
    q_kv = w[:, :nq + nkv]
    gts = w[:, nq + nkv:nq + nkv + ng]
    hg = w[:, nq + nkv + ng:]
    per = 3 * NSA_REP
    gate_blocks = [jnp.pad(gts[:, gi * per:(gi + 1) * per], ((0, 0), (0, LANES - per)))
                   for gi in range(NSA_GROUPS)]
    out = jnp.concatenate([q_kv, hg] + gate_blocks, axis=1).astype(BF16)
    assert out.shape == (d, _C_END)
    return out


def _compress_weights(pos, w1, w2):
    eye = jnp.eye(NSA_GROUPS, dtype=F32)
    e = w1.shape[-1]

    def big(w1_half):
        y = jnp.einsum("sldh,gk->slgdkh", w1_half, eye)
        return y.reshape(2, CMP_STRIDE * NSA_GROUPS * HEAD_DIM, NSA_GROUPS * e).astype(BF16)

    def pos_rows(p_half):
        y = jnp.broadcast_to(p_half[:, :, None, :], (2, CMP_STRIDE, NSA_GROUPS, HEAD_DIM))
        return y.reshape(2, 1, CMP_STRIDE * NSA_GROUPS * HEAD_DIM)

    w2b = jnp.einsum("shd,gk->sghkd", w2, eye).reshape(2, NSA_GROUPS * e, NSA_GROUPS * HEAD_DIM)
    return (big(w1[:, :CMP_STRIDE]), big(w1[:, CMP_STRIDE:]),
            pos_rows(pos[:, :CMP_STRIDE]), pos_rows(pos[:, CMP_STRIDE:]), w2b.astype(BF16))


def kernel(x, ffn1_norm, ffn1_w_gate, ffn1_w_up, ffn1_w_down, mix_norm, w_in, q_norm, k_norm,
           cmp_pos, cmp_w1, cmp_w2, hgrn_lb_logits, hgrn_out_norm, w_out,
           ffn2_norm, ffn2_w_gate, ffn2_w_up, ffn2_w_down):
    batch, seq, d = x.shape
    depth = w_in.shape[0]
    assert CMP_BLOCK == 2 * CMP_STRIDE and seq % SEL_CHUNK == 0 and seq >= WINDOW + Q_TILE
    assert seq % HG_ROWS == 0 and SEL_CHUNK % SEL_BLOCK == 0 and SEL_CHUNK % (NSA_TILES * Q_TILE) == 0
    assert seq % (2 * SEL_CHUNK) == 0
    t = batch * seq
    tm_proj = 512

    inv = 1.0 / (ROPE_THETA ** (jnp.arange(0, HEAD_DIM, 2, dtype=F32) / HEAD_DIM))
    ang = jnp.arange(seq, dtype=F32)[:, None] * inv[None, :]
    ang = jnp.concatenate([ang, ang], axis=-1)
    cos1, sin1 = jnp.cos(ang), jnp.sin(ang)
    cos2, sin2 = jnp.tile(cos1, (1, 2)), jnp.tile(sin1, (1, 2))
    cost, sint = cos1.T, sin1.T

    nseg = seq // CMP_STRIDE
    nsb = seq // SEL_BLOCK
    ci = np.arange(nseg)[None, :]
    sj = np.arange(nsb)[:, None]
    ovl = ((ci * CMP_STRIDE <= sj * SEL_BLOCK + SEL_BLOCK - 1)
           & (ci * CMP_STRIDE + CMP_BLOCK - 1 >= sj * SEL_BLOCK)
           & (ci < (seq - CMP_BLOCK) // CMP_STRIDE + 1))
    ovl = jnp.asarray(ovl, dtype=BF16)

    xt = x.reshape(t, d)
    for l in range(depth):
        xt = _ffn(xt, ffn1_norm, ffn1_w_gate, ffn1_w_up, ffn1_w_down, l)
        qgt = jnp.broadcast_to(q_norm[l][:, None], (HEAD_DIM, tm_proj))
        kg2 = jnp.tile(k_norm[l], (1, 2))
        q_t, cmp_in, ksel, kwin, vt, gates_t, h_all = _inproj(
            xt, mix_norm[l], _relayout_w_in(w_in[l]), cos2, sin2, cost, sint, qgt, kg2,
            batch, seq, tm=tm_proj)
        w1a, w1b, pa, pb, w2b = _compress_weights(cmp_pos[l], cmp_w1[l], cmp_w2[l])
        cc, cc_t = _compress(cmp_in, w1a, w1b, pa, pb, w2b, batch, seq)
        qr = 2 * SEL_CHUNK
        o_nsa = jnp.zeros((t, NSA_HEADS * HEAD_DIM), BF16)
        for lo in range(0, seq, qr):
            o_nsa = _nsa(q_t, cc, cc_t, ksel, kwin, vt, gates_t, ovl, o_nsa, batch, seq, lo, lo + qr)
        o_hg = _hgrn(h_all, hgrn_lb_logits, hgrn_out_norm[l], l, batch, seq)
        xt = _ffn(xt, ffn2_norm, ffn2_w_gate, ffn2_w_up, ffn2_w_down, l, mix=(o_nsa, o_hg, w_out))
    return xt.reshape(batch, seq, d)
```
